```python
import math
import jax, jax.numpy as jnp
from jax import lax
import numpy as np

D_MODEL = 1024
BATCH = 8
SEQ = 4096
DEPTH = 2

CONV_CH = 512
CONV_KERNEL = 31
ATTN_HEADS = 8
ATTN_HEAD_DIM = 64
ATTN_W = ATTN_HEADS * ATTN_HEAD_DIM
SSD_HEADS = 8
SSD_HEAD_DIM = 64
SSD_W = SSD_HEADS * SSD_HEAD_DIM
SSD_GROUPS = 2
SSD_STATE = 64
SSD_CONV = 4
SSD_CONV_CH = SSD_W + 2 * SSD_GROUPS * SSD_STATE
SSD_CHUNK = 128
Q_BLOCK = 128
D_MIX = CONV_CH + ATTN_W + SSD_W
IN_SIZES = (2 * CONV_CH, ATTN_W, ATTN_W, ATTN_W, ATTN_HEADS,
            SSD_W, SSD_W, SSD_GROUPS * SSD_STATE, SSD_GROUPS * SSD_STATE, SSD_HEADS)
D_IN = sum(IN_SIZES)
MOE_GROUPS = 4
EXPERTS_PER_GROUP = 4
N_EXPERTS = MOE_GROUPS * EXPERTS_PER_GROUP
TOP_K_INNER = 2
D_EXPERT = 512
EPS = 1e-6

kernel_name = "hybrid_conv_fox_ssd_hmoe"


def rmsnorm(x, g):
    xf = x.astype(jnp.float32)
    y = xf * lax.rsqrt(jnp.mean(xf * xf, axis=-1, keepdims=True) + EPS)
    return (y * g.astype(jnp.float32)).astype(x.dtype)


def layernorm(x, g, b):
    xf = x.astype(jnp.float32)
    mu = jnp.mean(xf, axis=-1, keepdims=True)
    var = jnp.mean(jnp.square(xf - mu), axis=-1, keepdims=True)
    y = (xf - mu) * lax.rsqrt(var + EPS) * g.astype(jnp.float32) + b.astype(jnp.float32)
    return y.astype(x.dtype)


def causal_depthwise_conv(x, w, b):
    K, C = w.shape
    y = lax.conv_general_dilated(
        x, w[:, None, :].astype(x.dtype), window_strides=(1,), padding=[(K - 1, 0)],
        dimension_numbers=("NWC", "WIO", "NWC"), feature_group_count=C)
    return y + b.astype(x.dtype)


def split_cols(u, sizes):
    offs = np.cumsum(sizes)[:-1].tolist()
    return jnp.split(u, offs, axis=-1)


def conformer_conv(u, dw_w, dw_b, ln_g, ln_b):
    a, gate = jnp.split(u, 2, axis=-1)
    h = a * jax.nn.sigmoid(gate)
    h = causal_depthwise_conv(h, dw_w, dw_b)
    h = layernorm(h, ln_g, ln_b)
    return jax.nn.silu(h)


def forgetting_attention(q, k, v, f_logit):
    Bn, L, _ = q.shape
    scale = ATTN_HEAD_DIM ** -0.5
    q = q.reshape(Bn, L, ATTN_HEADS, ATTN_HEAD_DIM) * scale
    k = k.reshape(Bn, L, ATTN_HEADS, ATTN_HEAD_DIM)
    v = v.reshape(Bn, L, ATTN_HEADS, ATTN_HEAD_DIM)
    log_f = jax.nn.log_sigmoid(f_logit.astype(jnp.float32))
    cum = jnp.transpose(jnp.cumsum(log_f, axis=1), (0, 2, 1))
    outs = []
    for i in range(L // Q_BLOCK):
        q0 = i * Q_BLOCK
        end = q0 + Q_BLOCK
        s = jnp.einsum('bqhd,bkhd->bhqk', q[:, q0:end], k[:, :end]).astype(jnp.float32)
        s = s + cum[:, :, q0:end, None] - cum[:, :, None, :end]
        qpos = q0 + jnp.arange(Q_BLOCK)
        kpos = jnp.arange(end)
        s = jnp.where(kpos[None, :] <= qpos[:, None], s, -jnp.inf)
        p = jax.nn.softmax(s, axis=-1)
        outs.append(jnp.einsum('bhqk,bkhd->bqhd', p.astype(v.dtype), v[:, :end]))
    o = jnp.concatenate(outs, axis=1)
    return o.reshape(Bn, L, ATTN_W)


def ssd_chunked(x, a, Bm, Cm):
    Bn, L, H, P = x.shape
    G, N = Bm.shape[2], Bm.shape[3]
    R = H // G
    T = SSD_CHUNK
    nc = L // T
    x = x.reshape(Bn, nc, T, G, R, P)
    a = a.reshape(Bn, nc, T, G, R)
    Bm = Bm.reshape(Bn, nc, T, G, N)
    Cm = Cm.reshape(Bn, nc, T, G, N)
    a_cum = jnp.cumsum(a, axis=2)
    seg = a_cum[:, :, :, None] - a_cum[:, :, None, :]
    causal = jnp.tril(jnp.ones((T, T), dtype=bool))[None, None, :, :, None, None]
    Lmat = jnp.exp(jnp.where(causal, seg, -jnp.inf))
    CB = jnp.einsum('bctgn,bcsgn->bctsg', Cm, Bm)
    y_diag = jnp.einsum('bctsgr,bcsgrp->bctgrp', CB[..., None] * Lmat, x)
    decay_to_end = jnp.exp(a_cum[:, :, -1:] - a_cum)
    states = jnp.einsum('bctgn,bctgrp->bcgrpn', Bm, x * decay_to_end[..., None])
    chunk_decay = jnp.exp(a_cum[:, :, -1])

    def step(h, inp):
        s_c, d_c = inp
        return h * d_c[..., None, None] + s_c, h

    h0 = jnp.zeros((Bn, G, R, P, N), dtype=states.dtype)
    _, h_prev = lax.scan(step, h0, (jnp.moveaxis(states, 1, 0), jnp.moveaxis(chunk_decay, 1, 0)))
    h_prev = jnp.moveaxis(h_prev, 0, 1)
    y_off = jnp.einsum('bctgn,bcgrpn->bctgrp', Cm, h_prev) * jnp.exp(a_cum)[..., None]
    return (y_diag + y_off).reshape(Bn, L, H, P)


def ssd_mixer(z, xs, Bm, Cm, dt_raw, conv_w, conv_b, dt_bias, A_log, D_skip, norm_g):
    Bn, L, _ = xs.shape
    xbc = jax.nn.silu(causal_depthwise_conv(jnp.concatenate([xs, Bm, Cm], axis=-1), conv_w, conv_b))
    xs, Bm, Cm = split_cols(xbc, (SSD_W, SSD_GROUPS * SSD_STATE, SSD_GROUPS * SSD_STATE))
    x = xs.reshape(Bn, L, SSD_HEADS, SSD_HEAD_DIM).astype(jnp.float32)
    Bm = Bm.reshape(Bn, L, SSD_GROUPS, SSD_STATE).astype(jnp.float32)
    Cm = Cm.reshape(Bn, L, SSD_GROUPS, SSD_STATE).astype(jnp.float32)
    dt = jax.nn.softplus(dt_raw.astype(jnp.float32) + dt_bias.astype(jnp.float32))
    A = -jnp.exp(A_log.astype(jnp.float32))
    y = ssd_chunked(x * dt[..., None], dt * A, Bm, Cm)
    y = y + x * D_skip.astype(jnp.float32)[:, None]
    y = y.reshape(Bn, L, SSD_W).astype(z.dtype)
    return rmsnorm(y * jax.nn.silu(z), norm_g)


def hierarchical_moe(x, wg, bg, we, be, w_gate, w_up, w_down):
    Bn, L, _ = x.shape
    g_prob = jax.nn.softmax((x @ wg).astype(jnp.float32) + bg.astype(jnp.float32), axis=-1)
    g_val, g_idx = lax.top_k(g_prob, 1)
    e_logits = ((x @ we).astype(jnp.float32) + be.astype(jnp.float32)).reshape(
        Bn, L, MOE_GROUPS, EXPERTS_PER_GROUP)
    e_sel = jnp.take_along_axis(e_logits, g_idx[..., None], axis=2)[:, :, 0]
    e_val, e_idx = lax.top_k(e_sel, TOP_K_INNER)
    e_w = jax.nn.softmax(e_val, axis=-1) * g_val
    expert_id = g_idx * EXPERTS_PER_GROUP + e_idx
    combine = jnp.sum(jax.nn.one_hot(expert_id, N_EXPERTS, dtype=jnp.float32) * e_w[..., None], axis=-2)
    combine = combine.astype(x.dtype)
    out = jnp.zeros_like(x)
    for e in range(N_EXPERTS):
        h = jax.nn.silu(x @ w_gate[e]) * (x @ w_up[e])
        out = out + combine[..., e:e + 1] * (h @ w_down[e])
    return out


def setup_inputs(seed: int = 0) -> dict:
    key = jax.random.key(seed)
    ks = iter(jax.random.split(key, 32))
    nrm = lambda shape, s: jax.random.normal(next(ks), shape, jnp.float32) * s
    x = nrm((BATCH, SEQ, D_MODEL), 1.0)
    norm_mix = 1.0 + nrm((DEPTH, D_MODEL), 0.02)
    w_in = nrm((DEPTH, D_MODEL, D_IN), D_MODEL ** -0.5)
    conv_dw_w = nrm((DEPTH, CONV_KERNEL, CONV_CH), CONV_KERNEL ** -0.5)
    conv_dw_b = nrm((DEPTH, CONV_CH), 0.01)
    conv_ln_g = 1.0 + nrm((DEPTH, CONV_CH), 0.02)
    conv_ln_b = nrm((DEPTH, CONV_CH), 0.01)
    fgate_b = jax.random.uniform(next(ks), (DEPTH, ATTN_HEADS), jnp.float32, 1.0, 6.0)
    ssd_conv_w = nrm((DEPTH, SSD_CONV, SSD_CONV_CH), SSD_CONV ** -0.5)
    ssd_conv_b = nrm((DEPTH, SSD_CONV_CH), 0.01)
    u = jax.random.uniform(next(ks), (DEPTH, SSD_HEADS), jnp.float32)
    dt0 = jnp.exp(u * (math.log(0.1) - math.log(0.001)) + math.log(0.001))
    ssd_dt_bias = dt0 + jnp.log(-jnp.expm1(-dt0))
    ssd_A_log = jnp.log(jax.random.uniform(next(ks), (DEPTH, SSD_HEADS), jnp.float32, 1.0, 16.0))
    ssd_D = 1.0 + nrm((DEPTH, SSD_HEADS), 0.1)
    ssd_norm_g = 1.0 + nrm((DEPTH, SSD_W), 0.02)
    w_out = nrm((DEPTH, D_MIX, D_MODEL), D_MIX ** -0.5)
    norm_ffn = 1.0 + nrm((DEPTH, D_MODEL), 0.02)
    router_group_w = nrm((DEPTH, D_MODEL, MOE_GROUPS), D_MODEL ** -0.5)
    router_group_b = nrm((DEPTH, MOE_GROUPS), 0.01)
    router_expert_w = nrm((DEPTH, D_MODEL, N_EXPERTS), D_MODEL ** -0.5)
    router_expert_b = nrm((DEPTH, N_EXPERTS), 0.01)
    expert_w_gate = nrm((DEPTH, N_EXPERTS, D_MODEL, D_EXPERT), D_MODEL ** -0.5)
    expert_w_up = nrm((DEPTH, N_EXPERTS, D_MODEL, D_EXPERT), D_MODEL ** -0.5)
    expert_w_down = nrm((DEPTH, N_EXPERTS, D_EXPERT, D_MODEL), D_EXPERT ** -0.5)
    norm_final = 1.0 + nrm((D_MODEL,), 0.02)
    return {"x": x, "norm_mix": norm_mix, "w_in": w_in, "conv_dw_w": conv_dw_w, "conv_dw_b": conv_dw_b,
            "conv_ln_g": conv_ln_g, "conv_ln_b": conv_ln_b, "fgate_b": fgate_b, "ssd_conv_w": ssd_conv_w,
            "ssd_conv_b": ssd_conv_b, "ssd_dt_bias": ssd_dt_bias, "ssd_A_log": ssd_A_log, "ssd_D": ssd_D,
            "ssd_norm_g": ssd_norm_g, "w_out": w_out, "norm_ffn": norm_ffn, "router_group_w": router_group_w,
            "router_group_b": router_group_b, "router_expert_w": router_expert_w,
            "router_expert_b": router_expert_b, "expert_w_gate": expert_w_gate, "expert_w_up": expert_w_up,
            "expert_w_down": expert_w_down, "norm_final": norm_final}


def reference(x, norm_mix, w_in, conv_dw_w, conv_dw_b, conv_ln_g, conv_ln_b, fgate_b, ssd_conv_w,
              ssd_conv_b, ssd_dt_bias, ssd_A_log, ssd_D, ssd_norm_g, w_out, norm_ffn, router_group_w,
              router_group_b, router_expert_w, router_expert_b, expert_w_gate, expert_w_up,
              expert_w_down, norm_final):
    for l in range(DEPTH):
        h = rmsnorm(x, norm_mix[l])
        u = h @ w_in[l]
        (u_conv, q, k, v, f_logit, z, xs, Bm, Cm, dt_raw) = split_cols(u, IN_SIZES)
        y_conv = conformer_conv(u_conv, conv_dw_w[l], conv_dw_b[l], conv_ln_g[l], conv_ln_b[l])
        y_attn = forgetting_attention(q, k, v, f_logit + fgate_b[l].astype(f_logit.dtype))
        y_ssd = ssd_mixer(z, xs, Bm, Cm, dt_raw, ssd_conv_w[l], ssd_conv_b[l], ssd_dt_bias[l],
                          ssd_A_log[l], ssd_D[l], ssd_norm_g[l])
        y = jnp.concatenate([y_conv, y_attn.astype(x.dtype), y_ssd], axis=-1) @ w_out[l]
        x = x + y.astype(x.dtype)
        hf = rmsnorm(x, norm_ffn[l])
        x = x + hierarchical_moe(hf, router_group_w[l], router_group_b[l], router_expert_w[l],
                                 router_expert_b[l], expert_w_gate[l], expert_w_up[l],
                                 expert_w_down[l]).astype(x.dtype)
    return rmsnorm(x, norm_final)
```

```python
import functools

import jax
import jax.numpy as jnp
from jax import lax
from jax.experimental import pallas as pl
from jax.experimental.pallas import tpu as pltpu

F32 = jnp.float32
BF16 = jnp.bfloat16
HI = lax.Precision.HIGHEST
EPS = 1e-6
NEG_INF = float("-inf")

HEAD_DIM = 64
LANES = 128
CONV_CH = 512
ATTN_W = 512
SSD_W = 512
SSD_GROUPS = 2
SSD_STATE = 64
SSD_CONV_CH = SSD_W + 2 * SSD_GROUPS * SSD_STATE
MOE_GROUPS = 4
EXPERTS_PER_GROUP = 4

ROW_TILE = 512
CONV_TILE = 256
CONV_ROWS = 64
CONV_HALO = 32
CUM_TILE = 512
ATTN_TQ = 256
ATTN_TK = 256
SSD_T = 256
SSD_HALO = 8
MOE_BLOCK = 1024
MOE_ROWS = 304
VMEM_LIMIT = 56 * 1024 * 1024


def _params(sem):
    return pltpu.CompilerParams(dimension_semantics=sem, vmem_limit_bytes=VMEM_LIMIT)


def _sigmoid(x):
    return jax.nn.sigmoid(x)


def _softplus(x):
    return jnp.maximum(x, 0.0) + jnp.log1p(jnp.exp(-jnp.abs(x)))


def _inproj_kernel(x_ref, g_ref, wa_ref, ws_ref, glu_ref, q_ref, k_ref, v_ref, z_ref, xbc_ref,
                   fs_ref, ds_ref):
    xf = x_ref[...]
    ms = jnp.mean(xf * xf, axis=-1, keepdims=True)
    h = (xf * lax.rsqrt(ms + EPS) * g_ref[...]).astype(BF16)

    def mm(lo, hi):
        return jnp.dot(h, wa_ref[:, lo:hi], preferred_element_type=F32)

    glu_ref[...] = mm(0, 512) * _sigmoid(mm(512, 1024))
    q_ref[...] = (mm(1024, 1536) * (HEAD_DIM ** -0.5)).astype(BF16)
    k_ref[...] = mm(1536, 2048).astype(BF16)
    v_ref[...] = mm(2048, 2560).astype(BF16)
    z_ref[...] = mm(2560, 3072)
    xbc_ref[...] = mm(3072, 3840)
    fs_ref[...] = jnp.dot(h, ws_ref[:, 0:LANES], preferred_element_type=F32)
    ds_ref[...] = jnp.dot(h, ws_ref[:, LANES:2 * LANES], preferred_element_type=F32)


def _inproj(x, g, wa, ws):
    n, d = x.shape
    tm = ROW_TILE
    row = lambda w: pl.BlockSpec((tm, w), lambda i: (i, 0))
    full = lambda a: pl.BlockSpec(a.shape, lambda i: (0, 0))
    shapes = [(512, F32), (512, BF16), (512, BF16), (512, BF16), (512, F32), (SSD_CONV_CH, F32),
              (LANES, F32), (LANES, F32)]
    return pl.pallas_call(
        _inproj_kernel,
        grid=(n // tm,),
        in_specs=[row(d), full(g), full(wa), full(ws)],
        out_specs=[row(w) for w, _ in shapes],
        out_shape=[jax.ShapeDtypeStruct((n, w), dt) for w, dt in shapes],
        compiler_params=_params(("parallel",)),
        name="inproj",
    )(x, g, wa, ws)


def _conv_kernel(x_ref, w_ref, b_ref, lg_ref, lb_ref, o_ref, buf_ref, *, taps, tl):
    li = pl.program_id(1)

    @pl.when(li == 0)
    def _():
        buf_ref[0:CONV_HALO, :] = jnp.zeros((CONV_HALO, CONV_CH), F32)

    @pl.when(li > 0)
    def _():
        buf_ref[0:CONV_HALO, :] = buf_ref[tl:tl + CONV_HALO, :]

    buf_ref[CONV_HALO:CONV_HALO + tl, :] = x_ref[0]
    first = CONV_HALO - (taps - 1)
    for r in range(0, tl, CONV_ROWS):
        acc = jnp.broadcast_to(b_ref[...], (CONV_ROWS, CONV_CH))
        for k in range(taps):
            acc = acc + w_ref[k:k + 1, :] * buf_ref[r + first + k:r + first + k + CONV_ROWS, :]
        mu = jnp.mean(acc, axis=-1, keepdims=True)
        cen = acc - mu
        var = jnp.mean(cen * cen, axis=-1, keepdims=True)
        y = cen * lax.rsqrt(var + EPS) * lg_ref[...] + lb_ref[...]
        o_ref[0, r:r + CONV_ROWS, :] = (y * _sigmoid(y)).astype(BF16)


def _conv(glu, w, b, lg, lb):
    bsz, seq, c = glu.shape
    tl = CONV_TILE
    taps = w.shape[0]
    full = lambda a: pl.BlockSpec(a.shape, lambda bi, li: (0, 0))
    return pl.pallas_call(
        functools.partial(_conv_kernel, taps=taps, tl=tl),
        grid=(bsz, seq // tl),
        in_specs=[pl.BlockSpec((1, tl, c), lambda bi, li: (bi, li, 0)), full(w), full(b), full(lg), full(lb)],
        out_specs=pl.BlockSpec((1, tl, c), lambda bi, li: (bi, li, 0)),
        out_shape=jax.ShapeDtypeStruct((bsz, seq, c), BF16),
        scratch_shapes=[pltpu.VMEM((tl + CONV_HALO, c), F32)],
        compiler_params=_params(("parallel", "arbitrary")),
        name="dwconv",
    )(glu, w, b, lg, lb)


def _lower_tri(n, dtype, strict=False):
    r = lax.broadcasted_iota(jnp.int32, (n, n), 0)
    c = lax.broadcasted_iota(jnp.int32, (n, n), 1)
    keep = (c < r) if strict else (c <= r)
    return jnp.where(keep, 1.0, 0.0).astype(dtype)


def _cum_kernel(f_ref, b_ref, o_ref, carry_ref, *, tl):
    li = pl.program_id(1)

    @pl.when(li == 0)
    def _():
        carry_ref[...] = jnp.zeros_like(carry_ref)

    x = f_ref[0] + b_ref[...]
    lf = jnp.minimum(x, 0.0) - jnp.log1p(jnp.exp(-jnp.abs(x)))
    cum = jnp.dot(_lower_tri(tl, F32), lf, precision=HI, preferred_element_type=F32) + carry_ref[...]
    carry_ref[...] = cum[tl - 1:tl, :]
    o_ref[0] = cum.T[0:8, :]


def _cum(fs, fb):
    bsz, seq, _ = fs.shape
    tl = CUM_TILE
    return pl.pallas_call(
        functools.partial(_cum_kernel, tl=tl),
        grid=(bsz, seq // tl),
        in_specs=[pl.BlockSpec((1, tl, LANES), lambda bi, li: (bi, li, 0)),
                  pl.BlockSpec((1, LANES), lambda bi, li: (0, 0))],
        out_specs=pl.BlockSpec((1, 8, tl), lambda bi, li: (bi, 0, li)),
        out_shape=jax.ShapeDtypeStruct((bsz, 8, seq), F32),
        scratch_shapes=[pltpu.VMEM((1, LANES), F32)],
        compiler_params=_params(("parallel", "arbitrary")),
        name="fgate_cumsum",
    )(fs, fb)


def _attn_kernel(q_ref, k_ref, v_ref, c_ref, o_ref, *, tq, tk):
    hp = pl.program_id(1)
    qi = pl.program_id(2)
    q = q_ref[0]
    lane = lax.broadcasted_iota(jnp.int32, (tq, LANES), 1)
    row = lax.broadcasted_iota(jnp.int32, (tq, tk), 0)
    col = lax.broadcasted_iota(jnp.int32, (tq, tk), 1)
    per_q = tq // tk
    outs = []
    for hh in range(2):
        in_head = (lane >= HEAD_DIM) if hh else (lane < HEAD_DIM)
        qm = jnp.where(in_head, q, jnp.zeros_like(q))
        head = hp * 2 + hh

        def block(j, carry, masked):
            m, l, acc = carry
            k0 = pl.multiple_of(j * tk, tk)
            kb = k_ref[0, pl.ds(k0, tk), :]
            vb = v_ref[0, pl.ds(k0, tk), :]
            cb = c_ref[0, pl.ds(head, 1), pl.ds(k0, tk)]
            s = lax.dot_general(qm, kb, (((1,), (1,)), ((), ())), preferred_element_type=F32) - cb
            if masked:
                s = jnp.where(k0 + col <= qi * tq + row, s, NEG_INF)
            m_new = jnp.maximum(m, jnp.max(s, axis=-1, keepdims=True))
            p = jnp.exp(s - m_new)
            alpha = jnp.exp(m - m_new)
            l = alpha * l + jnp.sum(p, axis=-1, keepdims=True)
            acc = alpha * acc + jnp.dot(p.astype(BF16), vb, preferred_element_type=F32)
            return m_new, l, acc

        carry = (jnp.full((tq, 1), NEG_INF, F32), jnp.zeros((tq, 1), F32), jnp.zeros((tq, LANES), F32))
        carry = lax.fori_loop(0, qi * per_q, functools.partial(block, masked=False), carry)
        for d in range(per_q):
            carry = block(qi * per_q + d, carry, True)
        _, l, acc = carry
        outs.append(acc / l)
    o_ref[0] = jnp.where(lane < HEAD_DIM, outs[0], outs[1]).astype(BF16)


def _attention(q, k, v, cum):
    bsz, seq, w = q.shape
    tq, tk = ATTN_TQ, ATTN_TK
    pairs = w // LANES
    return pl.pallas_call(
        functools.partial(_attn_kernel, tq=tq, tk=tk),
        grid=(bsz, pairs, seq // tq),
        in_specs=[pl.BlockSpec((1, tq, LANES), lambda b, h, i: (b, i, h)),
                  pl.BlockSpec((1, seq, LANES), lambda b, h, i: (b, 0, h)),
                  pl.BlockSpec((1, seq, LANES), lambda b, h, i: (b, 0, h)),
                  pl.BlockSpec((1, 8, seq), lambda b, h, i: (b, 0, 0))],
        out_specs=pl.BlockSpec((1, tq, LANES), lambda b, h, i: (b, i, h)),
        out_shape=jax.ShapeDtypeStruct((bsz, seq, w), BF16),
        compiler_params=_params(("parallel", "parallel", "arbitrary")),
        name="fox_attention",
    )(q, k, v, cum)


def _ssd_kernel(xbc_ref, z_ref, dt_ref, cw_ref, cb_ref, dtb_ref, alog_ref, dw_ref, ng_ref, ex_ref, o_ref,
                buf_ref, h_ref, *, T, taps):
    ci = pl.program_id(1)
    gw = SSD_W // SSD_GROUPS

    @pl.when(ci == 0)
    def _():
        buf_ref[0:SSD_HALO, :] = jnp.zeros((SSD_HALO, SSD_CONV_CH), F32)
        h_ref[...] = jnp.zeros_like(h_ref)

    @pl.when(ci > 0)
    def _():
        buf_ref[0:SSD_HALO, :] = buf_ref[T:T + SSD_HALO, :]

    buf_ref[SSD_HALO:SSD_HALO + T, :] = xbc_ref[0]
    first = SSD_HALO - (taps - 1)
    acc = jnp.broadcast_to(cb_ref[...], (T, SSD_CONV_CH))
    for k in range(taps):
        acc = acc + cw_ref[k:k + 1, :] * buf_ref[first + k:first + k + T, :]
    xc = acc * _sigmoid(acc)
    xs = xc[:, 0:SSD_W]
    b_mat = xc[:, SSD_W:SSD_W + LANES]
    c_mat = xc[:, SSD_W + LANES:SSD_W + 2 * LANES]

    dt = _softplus(dt_ref[0] + dtb_ref[...])
    a = dt * (-jnp.exp(alog_ref[...]))
    acum = jnp.dot(_lower_tri(T, F32), a, precision=HI, preferred_element_type=F32)
    acum_row = acum.T
    expand = ex_ref[...]
    dt_w = jnp.dot(dt, expand, precision=HI, preferred_element_type=F32)
    acum_w = jnp.dot(acum, expand, precision=HI, preferred_element_type=F32)
    last_w = acum_w[T - 1:T, :]
    xdt = xs * dt_w
    xdt_b = xdt.astype(BF16)
    xdec_b = (xdt * jnp.exp(last_w - acum_w)).astype(BF16)
    eacum_w = jnp.exp(acum_w)
    chunk_decay = jnp.exp(last_w)
    bb = b_mat.astype(BF16)
    cc = c_mat.astype(BF16)
    bt = b_mat.T.astype(BF16)

    row = lax.broadcasted_iota(jnp.int32, (T, T), 0)
    col = lax.broadcasted_iota(jnp.int32, (T, T), 1)
    causal = col <= row
    lane = lax.broadcasted_iota(jnp.int32, (T, LANES), 1)
    h_in = h_ref[...]
    h_ref[...] = h_in * chunk_decay + jnp.dot(bt, xdec_b, preferred_element_type=F32)
    h_in_b = h_in.astype(BF16)
    pieces = []
    for g in range(SSD_GROUPS):
        in_group = (lane >= g * SSD_STATE) & (lane < (g + 1) * SSD_STATE)
        cg = jnp.where(in_group, cc, jnp.zeros_like(cc))
        cbm = lax.dot_general(cg, bb, (((1,), (1,)), ((), ())), preferred_element_type=F32)
        yoff = jnp.dot(cg, h_in_b[:, g * gw:(g + 1) * gw], preferred_element_type=F32)
        for pr in range(gw // LANES):
            lo = g * gw + pr * LANES
            xpair = xdt_b[:, lo:lo + LANES]
            res = []
            for hh in range(2):
                head = lo // HEAD_DIM + hh
                seg = acum[:, head:head + 1] - acum_row[head:head + 1, :]
                lmat = jnp.exp(jnp.where(causal, seg, NEG_INF))
                res.append(jnp.dot((cbm * lmat).astype(BF16), xpair, preferred_element_type=F32))
            ydiag = jnp.where(lane < HEAD_DIM, res[0], res[1])
            pieces.append(ydiag + yoff[:, pr * LANES:(pr + 1) * LANES] * eacum_w[:, lo:lo + LANES])
    y = jnp.concatenate(pieces, axis=1) + xs * dw_ref[...]
    zz = z_ref[0]
    gated = y * (zz * _sigmoid(zz))
    ms = jnp.mean(gated * gated, axis=-1, keepdims=True)
    o_ref[0] = (gated * lax.rsqrt(ms + EPS) * ng_ref[...]).astype(BF16)


def _ssd(xbc, z, ds, cw, cb, dtb, alog, dwide, ng, expand):
    bsz, seq, _ = xbc.shape
    T = SSD_T
    taps = cw.shape[0]
    gw = SSD_W // SSD_GROUPS
    full = lambda a: pl.BlockSpec(a.shape, lambda bi, ci: (0, 0))
    tok = lambda w: pl.BlockSpec((1, T, w), lambda bi, ci: (bi, ci, 0))
    return pl.pallas_call(
        functools.partial(_ssd_kernel, T=T, taps=taps),
        grid=(bsz, seq // T),
        in_specs=[tok(SSD_CONV_CH), tok(SSD_W), tok(LANES), full(cw), full(cb), full(dtb), full(alog),
                  full(dwide), full(ng), full(expand)],
        out_specs=tok(SSD_W),
        out_shape=jax.ShapeDtypeStruct((bsz, seq, SSD_W), BF16),
        scratch_shapes=[pltpu.VMEM((T + SSD_HALO, SSD_CONV_CH), F32),
                        pltpu.VMEM((SSD_GROUPS * SSD_STATE, SSD_W), F32)],
        compiler_params=_params(("parallel", "arbitrary")),
        name="ssd_mixer",
    )(xbc, z, ds, cw, cb, dtb, alog, dwide, ng, expand)


def _first_max(vals):
    best = vals[0]
    for v in vals[1:]:
        best = jnp.maximum(best, v)
    idx = jnp.full(best.shape, float(len(vals) - 1), F32)
    for j in range(len(vals) - 2, -1, -1):
        idx = jnp.where(vals[j] == best, float(j), idx)
    return best, idx


def _outproj_kernel(yc_ref, ya_ref, ys_ref, x_ref, wo_ref, g_ref, wr_ref, br_ref,
                    x1_ref, hf_ref, rrow_ref, rcol_ref, *, tm):
    y = jnp.dot(yc_ref[...], wo_ref[0:CONV_CH, :], preferred_element_type=F32)
    y = y + jnp.dot(ya_ref[...], wo_ref[CONV_CH:CONV_CH + ATTN_W, :], preferred_element_type=F32)
    y = y + jnp.dot(ys_ref[...], wo_ref[CONV_CH + ATTN_W:, :], preferred_element_type=F32)
    x1 = x_ref[...] + y
    x1_ref[...] = x1
    ms = jnp.mean(x1 * x1, axis=-1, keepdims=True)
    hf = x1 * lax.rsqrt(ms + EPS) * g_ref[...]
    hf_ref[...] = hf.astype(BF16)

    logits = jnp.dot(hf, wr_ref[...], precision=HI, preferred_element_type=F32) + br_ref[...]
    lt = logits.T
    gl = [lt[j:j + 1, :] for j in range(MOE_GROUPS)]
    gmax, gidx = _first_max(gl)
    denom = gl[0] * 0.0
    for v in gl:
        denom = denom + jnp.exp(v - gmax)
    gval = 1.0 / denom
    esel = []
    for j in range(EXPERTS_PER_GROUP):
        erow = lambda g: lt[MOE_GROUPS + g * EXPERTS_PER_GROUP + j:MOE_GROUPS + g * EXPERTS_PER_GROUP + j + 1, :]
        v = erow(MOE_GROUPS - 1)
        for g in range(MOE_GROUPS - 2, -1, -1):
            v = jnp.where(gidx == float(g), erow(g), v)
        esel.append(v)
    v1, i1 = _first_max(esel)
    rest = [jnp.where(i1 == float(j), NEG_INF, esel[j]) for j in range(EXPERTS_PER_GROUP)]
    v2, i2 = _first_max(rest)
    e2 = jnp.exp(v2 - v1)
    w1 = (1.0 / (1.0 + e2)) * gval
    w2 = (e2 / (1.0 + e2)) * gval
    cw = [jnp.where(i1 == float(j), w1, jnp.where(i2 == float(j), w2, 0.0)) for j in range(EXPERTS_PER_GROUP)]
    hi = [c.astype(BF16).astype(F32) for c in cw]
    mid = [(c - h).astype(BF16).astype(F32) for c, h in zip(cw, hi)]
    lo = [(c - h - m).astype(BF16).astype(F32) for c, h, m in zip(cw, hi, mid)]
    rows = [gidx] + hi + mid + lo
    ri = lax.broadcasted_iota(jnp.int32, (LANES, tm), 0)
    packed = jnp.zeros((LANES, tm), F32)
    for j, r in enumerate(rows):
        packed = jnp.where(ri == j, r, packed)
    rrow_ref[...] = packed[0:8, :]
    rcol_ref[...] = packed.T


def _outproj(yc, ya, ys, x, wo, g, wr, br):
    n, d = x.shape
    tm = ROW_TILE
    row = lambda w: pl.BlockSpec((tm, w), lambda i: (i, 0))
    full = lambda a: pl.BlockSpec(a.shape, lambda i: (0, 0))
    return pl.pallas_call(
        functools.partial(_outproj_kernel, tm=tm),
        grid=(n // tm,),
        in_specs=[row(CONV_CH), row(ATTN_W), row(SSD_W), row(d), full(wo), full(g), full(wr), full(br)],
        out_specs=[row(d), row(d), pl.BlockSpec((8, tm), lambda i: (0, i)), row(LANES)],
        out_shape=[jax.ShapeDtypeStruct((n, d), F32), jax.ShapeDtypeStruct((n, d), BF16),
                   jax.ShapeDtypeStruct((8, n), F32), jax.ShapeDtypeStruct((n, LANES), F32)],
        compiler_params=_params(("parallel",)),
        name="outproj_router",
    )(yc, ya, ys, x, wo, g, wr, br)


def _moe_kernel(cnt_ref, x1_ref, hf_ref, rrow_ref, rcol_ref, wg_ref, wu_ref, wd_ref, fg_ref, o_ref,
                xg_ref, acc_ref, cw_ref, rankr_ref, *, tb, rows, final_norm):
    i = pl.program_id(0)
    g = pl.program_id(1)
    e = pl.program_id(2)
    cnt = cnt_ref[i * MOE_GROUPS + g]
    nch = (cnt + rows - 1) // rows
    gf = g.astype(F32)

    @pl.when((g == 0) & (e == 0))
    def _():
        o_ref[...] = x1_ref[...]
        gi_row = rrow_ref[0:1, :]
        sub = lax.broadcasted_iota(jnp.int32, (8, tb), 0).astype(F32)
        m_row = jnp.where(gi_row == sub, 1.0, 0.0).astype(BF16)
        r_ = lax.broadcasted_iota(jnp.int32, (tb, tb), 0)
        c_ = lax.broadcasted_iota(jnp.int32, (tb, tb), 1)
        before = jnp.where(r_ < c_, 1.0, 0.0).astype(BF16)
        rankr_ref[...] = jnp.dot(m_row, before, preferred_element_type=F32)

    def chunk(c):
        return pl.ds(pl.multiple_of(c * rows, 16), rows)

    @pl.when(e == 0)
    def _():
        gi_row = rrow_ref[0:1, :]
        sub8 = lax.broadcasted_iota(jnp.int32, (8, tb), 0)
        rank_g = jnp.sum(jnp.where(sub8 == g, rankr_ref[...], 0.0), axis=0, keepdims=True)
        slot = lax.broadcasted_iota(jnp.int32, (rows, tb), 0).astype(F32)
        rc_b = rcol_ref[...].astype(BF16)

        def gather(c, carry):
            base = (c * rows).astype(F32)
            p = jnp.where((rank_g == slot + base) & (gi_row == gf), 1.0, 0.0).astype(BF16)
            xg_ref[chunk(c), :] = jnp.dot(p, hf_ref[...], preferred_element_type=F32).astype(BF16)
            cw_ref[chunk(c), :] = jnp.dot(p, rc_b, preferred_element_type=F32)
            acc_ref[chunk(c), :] = jnp.zeros((rows, acc_ref.shape[1]), F32)
            return carry

        lax.fori_loop(0, nch, gather, 0)

    lane_r = lax.broadcasted_iota(jnp.int32, (rows, LANES), 1)
    pick = (lane_r == 1 + e) | (lane_r == 1 + EXPERTS_PER_GROUP + e) | (lane_r == 1 + 2 * EXPERTS_PER_GROUP + e)

    def expert(c, carry):
        xc = xg_ref[chunk(c), :]
        hg = jnp.dot(xc, wg_ref[0], preferred_element_type=F32)
        hu = jnp.dot(xc, wu_ref[0], preferred_element_type=F32)
        hh = (hg * _sigmoid(hg) * hu).astype(BF16)
        y = jnp.dot(hh, wd_ref[0], preferred_element_type=F32)
        cwe = jnp.sum(jnp.where(pick, cw_ref[chunk(c), :], 0.0), axis=-1, keepdims=True)
        acc_ref[chunk(c), :] += cwe * y
        return carry

    lax.fori_loop(0, nch, expert, 0)

    @pl.when(e == EXPERTS_PER_GROUP - 1)
    def _():
        gi_row = rrow_ref[0:1, :]
        sub8 = lax.broadcasted_iota(jnp.int32, (8, tb), 0)
        rank_g = jnp.sum(jnp.where(sub8 == g, rankr_ref[...], 0.0), axis=0, keepdims=True)
        slot = lax.broadcasted_iota(jnp.int32, (rows, tb), 0).astype(F32)

        def scatter(c, carry):
            base = (c * rows).astype(F32)
            p = jnp.where((rank_g == slot + base) & (gi_row == gf), 1.0, 0.0).astype(BF16)
            a = acc_ref[chunk(c), :]
            a_hi = a.astype(BF16)
            a_lo = (a - a_hi.astype(F32)).astype(BF16)
            tn = (((0,), (0,)), ((), ()))
            o_ref[...] += (lax.dot_general(p, a_hi, tn, preferred_element_type=F32)
                           + lax.dot_general(p, a_lo, tn, preferred_element_type=F32))
            return carry

        lax.fori_loop(0, nch, scatter, 0)

    if final_norm:
        @pl.when((g == MOE_GROUPS - 1) & (e == EXPERTS_PER_GROUP - 1))
        def _():
            xo = o_ref[...]
            ms = jnp.mean(xo * xo, axis=-1, keepdims=True)
            o_ref[...] = xo * lax.rsqrt(ms + EPS) * fg_ref[...]


def _moe(counts, x1, hf, rrow, rcol, wg, wu, wd, fg, final_norm):
    n, d = x1.shape
    tb, rows = MOE_BLOCK, MOE_ROWS
    de = wg.shape[2]
    cap = -(-tb // rows) * rows
    tokens = lambda w: pl.BlockSpec((tb, w), lambda i, g, e, c: (i, 0))
    grid_spec = pltpu.PrefetchScalarGridSpec(
        num_scalar_prefetch=1,
        grid=(n // tb, MOE_GROUPS, EXPERTS_PER_GROUP),
        in_specs=[tokens(d), tokens(d), pl.BlockSpec((8, tb), lambda i, g, e, c: (0, i)), tokens(LANES),
                  pl.BlockSpec((1, d, de), lambda i, g, e, c: (g * EXPERTS_PER_GROUP + e, 0, 0)),
                  pl.BlockSpec((1, d, de), lambda i, g, e, c: (g * EXPERTS_PER_GROUP + e, 0, 0)),
                  pl.BlockSpec((1, de, d), lambda i, g, e, c: (g * EXPERTS_PER_GROUP + e, 0, 0)),
                  pl.BlockSpec((1, d), lambda i, g, e, c: (0, 0))],
        out_specs=tokens(d),
        scratch_shapes=[pltpu.VMEM((cap, d), BF16), pltpu.VMEM((cap, d), F32), pltpu.VMEM((cap, LANES), F32),
                        pltpu.VMEM((8, tb), F32)],
    )
    return pl.pallas_call(
        functools.partial(_moe_kernel, tb=tb, rows=rows, final_norm=final_norm),
        grid_spec=grid_spec,
        out_shape=jax.ShapeDtypeStruct((n, d), F32),
        compiler_params=_params(("parallel", "arbitrary", "arbitrary")),
        name="moe",
    )(counts, x1, hf, rrow, rcol, wg, wu, wd, fg)


def _pad_lanes(v, width=LANES):
    v = v.reshape(1, -1).astype(F32)
    return jnp.pad(v, ((0, 0), (0, width - v.shape[1])))


def kernel(x, norm_mix, w_in, conv_dw_w, conv_dw_b, conv_ln_g, conv_ln_b, fgate_b, ssd_conv_w, ssd_conv_b,
           ssd_dt_bias, ssd_A_log, ssd_D, ssd_norm_g, w_out, norm_ffn, router_group_w, router_group_b,
           router_expert_w, router_expert_b, expert_w_gate, expert_w_up, expert_w_down, norm_final):
    bsz, seq, d = x.shape
    n = bsz * seq
    depth = w_in.shape[0]
    heads = fgate_b.shape[1]
    ssd_heads = ssd_A_log.shape[1]
    n_exp = expert_w_gate.shape[1]
    assert heads * HEAD_DIM == ATTN_W and ssd_heads * HEAD_DIM == SSD_W
    assert n_exp == MOE_GROUPS * EXPERTS_PER_GROUP and conv_dw_w.shape[1] <= CONV_HALO + 1
    assert n % ROW_TILE == 0 and n % MOE_BLOCK == 0 and seq % max(CONV_TILE, CUM_TILE, ATTN_TQ, SSD_T) == 0

    sizes = (2 * CONV_CH, ATTN_W, ATTN_W, ATTN_W, heads, SSD_W, SSD_W, SSD_GROUPS * SSD_STATE,
             SSD_GROUPS * SSD_STATE, ssd_heads)
    offs = [0]
    for s in sizes:
        offs.append(offs[-1] + s)
    expand = jnp.repeat(jnp.eye(LANES, SSD_W // HEAD_DIM, dtype=F32), HEAD_DIM, axis=1)

    xr = x.reshape(n, d)
    for l in range(depth):
        w = w_in[l]
        wa = jnp.concatenate([w[:, offs[0]:offs[4]], w[:, offs[5]:offs[9]]], axis=1).astype(BF16)
        zpad = jnp.zeros((d, LANES - heads), F32)
        ws = jnp.concatenate([w[:, offs[4]:offs[5]], zpad, w[:, offs[9]:offs[10]], zpad], axis=1).astype(BF16)
        glu, q, k, v, z, xbc, fs, ds = _inproj(xr, norm_mix[l].reshape(1, d), wa, ws)

        seq3 = lambda a: a.reshape(bsz, seq, a.shape[-1])
        y_conv = _conv(seq3(glu), conv_dw_w[l], conv_dw_b[l].reshape(1, -1), conv_ln_g[l].reshape(1, -1),
                       conv_ln_b[l].reshape(1, -1))
        cum = _cum(seq3(fs), _pad_lanes(fgate_b[l]))
        y_attn = _attention(seq3(q), seq3(k), seq3(v), cum)
        y_ssd = _ssd(seq3(xbc), seq3(z), seq3(ds), ssd_conv_w[l], ssd_conv_b[l].reshape(1, -1),
                     _pad_lanes(ssd_dt_bias[l]), _pad_lanes(ssd_A_log[l]),
                     jnp.repeat(ssd_D[l].astype(F32), HEAD_DIM).reshape(1, -1), ssd_norm_g[l].reshape(1, -1),
                     expand)

        wr = jnp.concatenate([router_group_w[l], router_expert_w[l],
                              jnp.zeros((d, LANES - MOE_GROUPS - n_exp), F32)], axis=1)
        br = _pad_lanes(jnp.concatenate([router_group_b[l], router_expert_b[l]]))
        x1, hf, rrow, rcol = _outproj(y_conv.reshape(n, -1), y_attn.reshape(n, -1), y_ssd.reshape(n, -1), xr,
                                      w_out[l].astype(BF16), norm_ffn[l].reshape(1, d), wr, br)

        gidx = rrow[0].astype(jnp.int32).reshape(n // MOE_BLOCK, MOE_BLOCK)
        counts = jnp.sum(gidx[:, :, None] == jnp.arange(MOE_GROUPS, dtype=jnp.int32), axis=1,
                         dtype=jnp.int32).reshape(-1)
        xr = _moe(counts, x1, hf, rrow, rcol, expert_w_gate[l].astype(BF16), expert_w_up[l].astype(BF16),
                  expert_w_down[l].astype(BF16), norm_final.reshape(1, d), final_norm=(l == depth - 1))
    return xr.reshape(bsz, seq, d)
```

```python
import functools

import jax
import jax.numpy as jnp
from jax import lax
from jax.experimental import pallas as pl
from jax.experimental.pallas import tpu as pltpu

F32 = jnp.float32
BF16 = jnp.bfloat16
HI = lax.Precision.HIGHEST
EPS = 1e-6
NEG_INF = float("-inf")

HEAD_DIM = 64
LANES = 128
CONV_CH = 512
ATTN_W = 512
SSD_W = 512
SSD_GROUPS = 2
SSD_STATE = 64
SSD_CONV_CH = SSD_W + 2 * SSD_GROUPS * SSD_STATE
MOE_GROUPS = 4
EXPERTS_PER_GROUP = 4

ROW_TILE = 512
CONV_TILE = 256
CONV_ROWS = 32
CONV_HALO = 32
SUBLANES = 8
CUM_TILE = 512
ATTN_TQ = 512
SSD_T = 256
SSD_HALO = 8
MOE_BLOCK = 1024
MOE_ROWS = 304
VMEM_LIMIT = 56 * 1024 * 1024


def _params(sem):
    return pltpu.CompilerParams(dimension_semantics=sem, vmem_limit_bytes=VMEM_LIMIT)


def _sigmoid(x):
    return jax.nn.sigmoid(x)


def _softplus(x):
    return jnp.maximum(x, 0.0) + jnp.log1p(jnp.exp(-jnp.abs(x)))


def _split3(x):
    hi = x.astype(BF16)
    r = x - hi.astype(F32)
    mid = r.astype(BF16)
    lo = (r - mid.astype(F32)).astype(BF16)
    return hi, mid, lo


def _select_dot(sel, x):
    hi, mid, lo = _split3(x)
    d = lambda p: jnp.dot(sel, p, preferred_element_type=F32)
    return d(hi) + d(mid) + d(lo)


def _dot_select(x, sel):
    hi, mid, lo = _split3(x)
    d = lambda p: jnp.dot(p, sel, preferred_element_type=F32)
    return d(hi) + d(mid) + d(lo)


def _rows8(w_ref, k, rows):
    return jnp.concatenate([w_ref[k * SUBLANES:(k + 1) * SUBLANES, :]] * (rows // SUBLANES), axis=0)


def _inproj_kernel(x_ref, g_ref, wa_ref, ws_ref, glu_ref, q_ref, k_ref, v_ref, z_ref, xbc_ref,
                   fs_ref, ds_ref):
    xf = x_ref[...]
    ms = jnp.mean(xf * xf, axis=-1, keepdims=True)
    h = (xf * lax.rsqrt(ms + EPS) * g_ref[...]).astype(BF16)

    def mm(lo, hi):
        return jnp.dot(h, wa_ref[:, lo:hi], preferred_element_type=F32)

    glu_ref[...] = mm(0, 512) * _sigmoid(mm(512, 1024))
    q_ref[...] = (mm(1024, 1536) * (HEAD_DIM ** -0.5)).astype(BF16)
    k_ref[...] = mm(1536, 2048).astype(BF16)
    v_ref[...] = mm(2048, 2560).astype(BF16)
    z_ref[...] = mm(2560, 3072)
    xbc_ref[...] = mm(3072, 3840)
    fs_ref[...] = jnp.dot(h, ws_ref[:, 0:LANES], preferred_element_type=F32)
    ds_ref[...] = jnp.dot(h, ws_ref[:, LANES:2 * LANES], preferred_element_type=F32)


def _inproj(x, g, wa, ws):
    n, d = x.shape
    tm = ROW_TILE
    row = lambda w: pl.BlockSpec((tm, w), lambda i: (i, 0))
    full = lambda a: pl.BlockSpec(a.shape, lambda i: (0, 0))
    shapes = [(512, F32), (512, BF16), (512, BF16), (512, BF16), (512, F32), (SSD_CONV_CH, F32),
              (LANES, F32), (LANES, F32)]
    return pl.pallas_call(
        _inproj_kernel,
        grid=(n // tm,),
        in_specs=[row(d), full(g), full(wa), full(ws)],
        out_specs=[row(w) for w, _ in shapes],
        out_shape=[jax.ShapeDtypeStruct((n, w), dt) for w, dt in shapes],
        compiler_params=_params(("parallel",)),
        name="inproj",
    )(x, g, wa, ws)


def _conv_kernel(x_ref, w_ref, b_ref, lg_ref, lb_ref, o_ref, buf_ref, *, taps, tl):
    li = pl.program_id(1)

    @pl.when(li == 0)
    def _():
        buf_ref[0:CONV_HALO, :] = jnp.zeros((CONV_HALO, CONV_CH), F32)

    @pl.when(li > 0)
    def _():
        buf_ref[0:CONV_HALO, :] = buf_ref[tl:tl + CONV_HALO, :]

    buf_ref[CONV_HALO:CONV_HALO + tl, :] = x_ref[0]
    first = CONV_HALO - (taps - 1)
    by_shift = [[k for k in range(taps) if (first + k) % SUBLANES == s] for s in range(SUBLANES)]
    for r in range(0, tl, CONV_ROWS):
        acc = jnp.broadcast_to(b_ref[...], (CONV_ROWS, CONV_CH))
        for s, ks in enumerate(by_shift):
            if not ks:
                continue
            span = CONV_ROWS + (SUBLANES if s else 0)
            z = None
            for k in ks:
                base = r + first + k - s
                term = _rows8(w_ref, k, span) * buf_ref[base:base + span, :]
                z = term if z is None else z + term
            acc = acc + z[s:s + CONV_ROWS, :]
        mu = jnp.mean(acc, axis=-1, keepdims=True)
        cen = acc - mu
        var = jnp.mean(cen * cen, axis=-1, keepdims=True)
        y = cen * lax.rsqrt(var + EPS) * lg_ref[...] + lb_ref[...]
        o_ref[0, r:r + CONV_ROWS, :] = (y * _sigmoid(y)).astype(BF16)


def _conv(glu, w, b, lg, lb):
    bsz, seq, c = glu.shape
    tl = CONV_TILE
    taps = w.shape[0]
    w = jnp.repeat(w, SUBLANES, axis=0)
    full = lambda a: pl.BlockSpec(a.shape, lambda bi, li: (0, 0))
    return pl.pallas_call(
        functools.partial(_conv_kernel, taps=taps, tl=tl),
        grid=(bsz, seq // tl),
        in_specs=[pl.BlockSpec((1, tl, c), lambda bi, li: (bi, li, 0)), full(w), full(b), full(lg), full(lb)],
        out_specs=pl.BlockSpec((1, tl, c), lambda bi, li: (bi, li, 0)),
        out_shape=jax.ShapeDtypeStruct((bsz, seq, c), BF16),
        scratch_shapes=[pltpu.VMEM((tl + CONV_HALO, c), F32)],
        compiler_params=_params(("parallel", "arbitrary")),
        name="dwconv",
    )(glu, w, b, lg, lb)


def _lower_tri(n, dtype, strict=False):
    r = lax.broadcasted_iota(jnp.int32, (n, n), 0)
    c = lax.broadcasted_iota(jnp.int32, (n, n), 1)
    keep = (c < r) if strict else (c <= r)
    return jnp.where(keep, 1.0, 0.0).astype(dtype)


def _cum_kernel(f_ref, b_ref, o_ref, carry_ref, *, tl):
    li = pl.program_id(1)

    @pl.when(li == 0)
    def _():
        carry_ref[...] = jnp.zeros_like(carry_ref)

    x = f_ref[0] + b_ref[...]
    lf = jnp.minimum(x, 0.0) - jnp.log1p(jnp.exp(-jnp.abs(x)))
    cum = _select_dot(_lower_tri(tl, BF16), lf) + carry_ref[...]
    carry_ref[...] = cum[tl - 1:tl, :]
    o_ref[0] = cum.T[0:8, :]


def _cum(fs, fb):
    bsz, seq, _ = fs.shape
    tl = CUM_TILE
    return pl.pallas_call(
        functools.partial(_cum_kernel, tl=tl),
        grid=(bsz, seq // tl),
        in_specs=[pl.BlockSpec((1, tl, LANES), lambda bi, li: (bi, li, 0)),
                  pl.BlockSpec((1, LANES), lambda bi, li: (0, 0))],
        out_specs=pl.BlockSpec((1, 8, tl), lambda bi, li: (bi, 0, li)),
        out_shape=jax.ShapeDtypeStruct((bsz, 8, seq), F32),
        scratch_shapes=[pltpu.VMEM((1, LANES), F32)],
        compiler_params=_params(("parallel", "arbitrary")),
        name="fgate_cumsum",
    )(fs, fb)


def _attn_kernel(q_ref, k_ref, v_ref, c_ref, o_ref, qm_ref, s_ref, m_ref, l_ref, acc_ref, *, tq):
    hp = pl.program_id(1)
    qi = pl.program_id(2)
    q = q_ref[0]
    lane = lax.broadcasted_iota(jnp.int32, (tq, LANES), 1)
    row = lax.broadcasted_iota(jnp.int32, (tq, tq), 0)
    col = lax.broadcasted_iota(jnp.int32, (tq, tq), 1)
    tiles = tq // LANES
    for hh in range(2):
        in_head = (lane >= HEAD_DIM) if hh else (lane < HEAD_DIM)
        qm_ref[hh] = jnp.where(in_head, q, jnp.zeros_like(q))
        m_ref[hh] = jnp.full((tq, LANES), NEG_INF, F32)

    def scores(j, carry, masked):
        k0 = pl.multiple_of(j * tq, tq)
        kb = k_ref[0, pl.ds(k0, tq), :]
        for hh in range(2):
            cb = c_ref[0, pl.ds(hp * 2 + hh, 1), pl.ds(k0, tq)]
            s = lax.dot_general(qm_ref[hh], kb, (((1,), (1,)), ((), ())), preferred_element_type=F32) - cb
            if masked:
                s = jnp.where(col <= row, s, NEG_INF)
            s_ref[hh, j] = s
            mt = m_ref[hh]
            for t in range(tiles):
                mt = jnp.maximum(mt, s[:, t * LANES:(t + 1) * LANES])
            m_ref[hh] = mt
        return carry

    lax.fori_loop(0, qi, functools.partial(scores, masked=False), 0)
    scores(qi, 0, True)

    for hh in range(2):
        m_ref[hh] = jnp.broadcast_to(jnp.max(m_ref[hh], axis=-1, keepdims=True), (tq, LANES))
        l_ref[hh] = jnp.zeros((tq, LANES), F32)
        acc_ref[hh] = jnp.zeros((tq, LANES), F32)

    def weighted(j, carry):
        k0 = pl.multiple_of(j * tq, tq)
        vb = v_ref[0, pl.ds(k0, tq), :]
        for hh in range(2):
            mb = m_ref[hh]
            lt = l_ref[hh]
            ps = []
            for t in range(tiles):
                p = jnp.exp(s_ref[hh, j, :, t * LANES:(t + 1) * LANES] - mb)
                lt = lt + p
                ps.append(p.astype(BF16))
            l_ref[hh] = lt
            acc_ref[hh] += jnp.dot(jnp.concatenate(ps, axis=1), vb, preferred_element_type=F32)
        return carry

    lax.fori_loop(0, qi + 1, weighted, 0)
    outs = [acc_ref[hh] / jnp.sum(l_ref[hh], axis=-1, keepdims=True) for hh in range(2)]
    o_ref[0] = jnp.where(lane < HEAD_DIM, outs[0], outs[1]).astype(BF16)


def _attention(q, k, v, cum):
    bsz, seq, w = q.shape
    tq = ATTN_TQ
    pairs = w // LANES
    return pl.pallas_call(
        functools.partial(_attn_kernel, tq=tq),
        grid=(bsz, pairs, seq // tq),
        in_specs=[pl.BlockSpec((1, tq, LANES), lambda b, h, i: (b, i, h)),
                  pl.BlockSpec((1, seq, LANES), lambda b, h, i: (b, 0, h)),
                  pl.BlockSpec((1, seq, LANES), lambda b, h, i: (b, 0, h)),
                  pl.BlockSpec((1, 8, seq), lambda b, h, i: (b, 0, 0))],
        out_specs=pl.BlockSpec((1, tq, LANES), lambda b, h, i: (b, i, h)),
        out_shape=jax.ShapeDtypeStruct((bsz, seq, w), BF16),
        scratch_shapes=[pltpu.VMEM((2, tq, LANES), BF16), pltpu.VMEM((2, seq // tq, tq, tq), F32),
                        pltpu.VMEM((2, tq, LANES), F32), pltpu.VMEM((2, tq, LANES), F32),
                        pltpu.VMEM((2, tq, LANES), F32)],
        compiler_params=_params(("parallel", "parallel", "arbitrary")),
        name="fox_attention",
    )(q, k, v, cum)


def _ssd_kernel(xbc_ref, z_ref, dt_ref, cw_ref, cb_ref, dtb_ref, alog_ref, dw_ref, ng_ref, ex_ref, o_ref,
                buf_ref, h_ref, *, T, taps):
    ci = pl.program_id(1)
    gw = SSD_W // SSD_GROUPS

    @pl.when(ci == 0)
    def _():
        buf_ref[0:SSD_HALO, :] = jnp.zeros((SSD_HALO, SSD_CONV_CH), F32)
        h_ref[...] = jnp.zeros_like(h_ref)

    @pl.when(ci > 0)
    def _():
        buf_ref[0:SSD_HALO, :] = buf_ref[T:T + SSD_HALO, :]

    buf_ref[SSD_HALO:SSD_HALO + T, :] = xbc_ref[0]
    first = SSD_HALO - (taps - 1)
    acc = jnp.broadcast_to(cb_ref[...], (T, SSD_CONV_CH))
    for k in range(taps):
        acc = acc + _rows8(cw_ref, k, T) * buf_ref[first + k:first + k + T, :]
    xc = acc * _sigmoid(acc)
    xs = xc[:, 0:SSD_W]
    b_mat = xc[:, SSD_W:SSD_W + LANES]
    c_mat = xc[:, SSD_W + LANES:SSD_W + 2 * LANES]

    dt = _softplus(dt_ref[0] + dtb_ref[...])
    a = dt * (-jnp.exp(alog_ref[...]))
    acum = _select_dot(_lower_tri(T, BF16), a)
    acum_row = acum.T
    expand = ex_ref[...]
    dt_w = _dot_select(dt, expand)
    acum_w = _dot_select(acum, expand)
    last_w = acum_w[T - 1:T, :]
    xdt = xs * dt_w
    xdt_b = xdt.astype(BF16)
    xdec_b = (xdt * jnp.exp(last_w - acum_w)).astype(BF16)
    eacum_w = jnp.exp(acum_w)
    chunk_decay = jnp.exp(last_w)
    bb = b_mat.astype(BF16)
    cc = c_mat.astype(BF16)
    bt = b_mat.T.astype(BF16)

    row = lax.broadcasted_iota(jnp.int32, (T, T), 0)
    col = lax.broadcasted_iota(jnp.int32, (T, T), 1)
    causal = col <= row
    lane = lax.broadcasted_iota(jnp.int32, (T, LANES), 1)
    h_in = h_ref[...]
    h_ref[...] = h_in * chunk_decay + jnp.dot(bt, xdec_b, preferred_element_type=F32)
    h_in_b = h_in.astype(BF16)
    pieces = []
    for g in range(SSD_GROUPS):
        in_group = (lane >= g * SSD_STATE) & (lane < (g + 1) * SSD_STATE)
        cg = jnp.where(in_group, cc, jnp.zeros_like(cc))
        cbm = lax.dot_general(cg, bb, (((1,), (1,)), ((), ())), preferred_element_type=F32)
        yoff = jnp.dot(cg, h_in_b[:, g * gw:(g + 1) * gw], preferred_element_type=F32)
        for pr in range(gw // LANES):
            lo = g * gw + pr * LANES
            xpair = xdt_b[:, lo:lo + LANES]
            res = []
            for hh in range(2):
                head = lo // HEAD_DIM + hh
                seg = acum[:, head:head + 1] - acum_row[head:head + 1, :]
                lmat = jnp.exp(jnp.where(causal, seg, NEG_INF))
                res.append(jnp.dot((cbm * lmat).astype(BF16), xpair, preferred_element_type=F32))
            ydiag = jnp.where(lane < HEAD_DIM, res[0], res[1])
            pieces.append(ydiag + yoff[:, pr * LANES:(pr + 1) * LANES] * eacum_w[:, lo:lo + LANES])
    y = jnp.concatenate(pieces, axis=1) + xs * dw_ref[...]
    zz = z_ref[0]
    gated = y * (zz * _sigmoid(zz))
    ms = jnp.mean(gated * gated, axis=-1, keepdims=True)
    o_ref[0] = (gated * lax.rsqrt(ms + EPS) * ng_ref[...]).astype(BF16)


def _ssd(xbc, z, ds, cw, cb, dtb, alog, dwide, ng, expand):
    bsz, seq, _ = xbc.shape
    T = SSD_T
    taps = cw.shape[0]
    cw = jnp.repeat(cw, SUBLANES, axis=0)
    gw = SSD_W // SSD_GROUPS
    full = lambda a: pl.BlockSpec(a.shape, lambda bi, ci: (0, 0))
    tok = lambda w: pl.BlockSpec((1, T, w), lambda bi, ci: (bi, ci, 0))
    return pl.pallas_call(
        functools.partial(_ssd_kernel, T=T, taps=taps),
        grid=(bsz, seq // T),
        in_specs=[tok(SSD_CONV_CH), tok(SSD_W), tok(LANES), full(cw), full(cb), full(dtb), full(alog),
                  full(dwide), full(ng), full(expand)],
        out_specs=tok(SSD_W),
        out_shape=jax.ShapeDtypeStruct((bsz, seq, SSD_W), BF16),
        scratch_shapes=[pltpu.VMEM((T + SSD_HALO, SSD_CONV_CH), F32),
                        pltpu.VMEM((SSD_GROUPS * SSD_STATE, SSD_W), F32)],
        compiler_params=_params(("parallel", "arbitrary")),
        name="ssd_mixer",
    )(xbc, z, ds, cw, cb, dtb, alog, dwide, ng, expand)


def _first_max(vals):
    best = vals[0]
    for v in vals[1:]:
        best = jnp.maximum(best, v)
    idx = jnp.full(best.shape, float(len(vals) - 1), F32)
    for j in range(len(vals) - 2, -1, -1):
        idx = jnp.where(vals[j] == best, float(j), idx)
    return best, idx


def _outproj_kernel(yc_ref, ya_ref, ys_ref, x_ref, wo_ref, g_ref, wr_ref, br_ref,
                    x1_ref, hf_ref, rrow_ref, rcol_ref, *, tm):
    y = jnp.dot(yc_ref[...], wo_ref[0:CONV_CH, :], preferred_element_type=F32)
    y = y + jnp.dot(ya_ref[...], wo_ref[CONV_CH:CONV_CH + ATTN_W, :], preferred_element_type=F32)
    y = y + jnp.dot(ys_ref[...], wo_ref[CONV_CH + ATTN_W:, :], preferred_element_type=F32)
    x1 = x_ref[...] + y
    x1_ref[...] = x1
    ms = jnp.mean(x1 * x1, axis=-1, keepdims=True)
    hf = x1 * lax.rsqrt(ms + EPS) * g_ref[...]
    hf_hi = hf.astype(BF16)
    hf_ref[...] = hf_hi
    hf_lo = (hf - hf_hi.astype(F32)).astype(BF16)
    logits = (jnp.dot(hf_hi, wr_ref[0], preferred_element_type=F32)
              + jnp.dot(hf_hi, wr_ref[1], preferred_element_type=F32)
              + jnp.dot(hf_lo, wr_ref[0], preferred_element_type=F32)) + br_ref[...]
    lt = logits.T
    gl = [lt[j:j + 1, :] for j in range(MOE_GROUPS)]
    gmax, gidx = _first_max(gl)
    denom = gl[0] * 0.0
    for v in gl:
        denom = denom + jnp.exp(v - gmax)
    gval = 1.0 / denom
    esel = []
    for j in range(EXPERTS_PER_GROUP):
        erow = lambda g: lt[MOE_GROUPS + g * EXPERTS_PER_GROUP + j:MOE_GROUPS + g * EXPERTS_PER_GROUP + j + 1, :]
        v = erow(MOE_GROUPS - 1)
        for g in range(MOE_GROUPS - 2, -1, -1):
            v = jnp.where(gidx == float(g), erow(g), v)
        esel.append(v)
    v1, i1 = _first_max(esel)
    rest = [jnp.where(i1 == float(j), NEG_INF, esel[j]) for j in range(EXPERTS_PER_GROUP)]
    v2, i2 = _first_max(rest)
    e2 = jnp.exp(v2 - v1)
    w1 = (1.0 / (1.0 + e2)) * gval
    w2 = (e2 / (1.0 + e2)) * gval
    cw = [jnp.where(i1 == float(j), w1, jnp.where(i2 == float(j), w2, 0.0)) for j in range(EXPERTS_PER_GROUP)]
    hi = [c.astype(BF16).astype(F32) for c in cw]
    mid = [(c - h).astype(BF16).astype(F32) for c, h in zip(cw, hi)]
    lo = [(c - h - m).astype(BF16).astype(F32) for c, h, m in zip(cw, hi, mid)]
    rows = [gidx] + hi + mid + lo
    ri = lax.broadcasted_iota(jnp.int32, (LANES, tm), 0)
    packed = jnp.zeros((LANES, tm), F32)
    for j, r in enumerate(rows):
        packed = jnp.where(ri == j, r, packed)
    rrow_ref[...] = packed[0:8, :]
    rcol_ref[...] = packed.T


def _outproj(yc, ya, ys, x, wo, g, wr, br):
    n, d = x.shape
    tm = ROW_TILE
    row = lambda w: pl.BlockSpec((tm, w), lambda i: (i, 0))
    full = lambda a: pl.BlockSpec(a.shape, lambda i: (0, 0))
    return pl.pallas_call(
        functools.partial(_outproj_kernel, tm=tm),
        grid=(n // tm,),
        in_specs=[row(CONV_CH), row(ATTN_W), row(SSD_W), row(d), full(wo), full(g),
                  pl.BlockSpec(wr.shape, lambda i: (0, 0, 0)), full(br)],
        out_specs=[row(d), row(d), pl.BlockSpec((8, tm), lambda i: (0, i)), row(LANES)],
        out_shape=[jax.ShapeDtypeStruct((n, d), F32), jax.ShapeDtypeStruct((n, d), BF16),
                   jax.ShapeDtypeStruct((8, n), F32), jax.ShapeDtypeStruct((n, LANES), F32)],
        compiler_params=_params(("parallel",)),
        name="outproj_router",
    )(yc, ya, ys, x, wo, g, wr, br)


def _moe_kernel(cnt_ref, x1_ref, hf_ref, rrow_ref, rcol_ref, wg_ref, wu_ref, wd_ref, fg_ref, o_ref,
                xg_ref, acc_ref, cw_ref, rankr_ref, *, tb, rows, final_norm):
    i = pl.program_id(0)
    g = pl.program_id(1)
    e = pl.program_id(2)
    cnt = cnt_ref[i * MOE_GROUPS + g]
    nch = (cnt + rows - 1) // rows
    gf = g.astype(F32)

    @pl.when((g == 0) & (e == 0))
    def _():
        o_ref[...] = x1_ref[...]
        gi_row = rrow_ref[0:1, :]
        sub = lax.broadcasted_iota(jnp.int32, (8, tb), 0).astype(F32)
        m_row = jnp.where(gi_row == sub, 1.0, 0.0).astype(BF16)
        r_ = lax.broadcasted_iota(jnp.int32, (tb, tb), 0)
        c_ = lax.broadcasted_iota(jnp.int32, (tb, tb), 1)
        before = jnp.where(r_ < c_, 1.0, 0.0).astype(BF16)
        rankr_ref[...] = jnp.dot(m_row, before, preferred_element_type=F32)

    def chunk(c):
        return pl.ds(pl.multiple_of(c * rows, 16), rows)

    @pl.when(e == 0)
    def _():
        gi_row = rrow_ref[0:1, :]
        sub8 = lax.broadcasted_iota(jnp.int32, (8, tb), 0)
        rank_g = jnp.sum(jnp.where(sub8 == g, rankr_ref[...], 0.0), axis=0, keepdims=True)
        slot = lax.broadcasted_iota(jnp.int32, (rows, tb), 0).astype(F32)
        rc_b = rcol_ref[...].astype(BF16)

        def gather(c, carry):
            base = (c * rows).astype(F32)
            p = jnp.where((rank_g == slot + base) & (gi_row == gf), 1.0, 0.0).astype(BF16)
            xg_ref[chunk(c), :] = jnp.dot(p, hf_ref[...], preferred_element_type=F32).astype(BF16)
            cw_ref[chunk(c), :] = jnp.dot(p, rc_b, preferred_element_type=F32)
            acc_ref[chunk(c), :] = jnp.zeros((rows, acc_ref.shape[1]), F32)
            return carry

        lax.fori_loop(0, nch, gather, 0)

    lane_r = lax.broadcasted_iota(jnp.int32, (rows, LANES), 1)
    pick = (lane_r == 1 + e) | (lane_r == 1 + EXPERTS_PER_GROUP + e) | (lane_r == 1 + 2 * EXPERTS_PER_GROUP + e)

    def expert(c, carry):
        xc = xg_ref[chunk(c), :]
        hg = jnp.dot(xc, wg_ref[0], preferred_element_type=F32)
        hu = jnp.dot(xc, wu_ref[0], preferred_element_type=F32)
        hh = (hg * _sigmoid(hg) * hu).astype(BF16)
        y = jnp.dot(hh, wd_ref[0], preferred_element_type=F32)
        cwe = jnp.sum(jnp.where(pick, cw_ref[chunk(c), :], 0.0), axis=-1, keepdims=True)
        acc_ref[chunk(c), :] += cwe * y
        return carry

    lax.fori_loop(0, nch, expert, 0)

    @pl.when(e == EXPERTS_PER_GROUP - 1)
    def _():
        gi_row = rrow_ref[0:1, :]
        sub8 = lax.broadcasted_iota(jnp.int32, (8, tb), 0)
        rank_g = jnp.sum(jnp.where(sub8 == g, rankr_ref[...], 0.0), axis=0, keepdims=True)
        slot = lax.broadcasted_iota(jnp.int32, (rows, tb), 0).astype(F32)

        def scatter(c, carry):
            base = (c * rows).astype(F32)
            p = jnp.where((rank_g == slot + base) & (gi_row == gf), 1.0, 0.0).astype(BF16)
            a = acc_ref[chunk(c), :]
            a_hi = a.astype(BF16)
            a_lo = (a - a_hi.astype(F32)).astype(BF16)
            tn = (((0,), (0,)), ((), ()))
            o_ref[...] += (lax.dot_general(p, a_hi, tn, preferred_element_type=F32)
                           + lax.dot_general(p, a_lo, tn, preferred_element_type=F32))
            return carry

        lax.fori_loop(0, nch, scatter, 0)

    if final_norm:
        @pl.when((g == MOE_GROUPS - 1) & (e == EXPERTS_PER_GROUP - 1))
        def _():
            xo = o_ref[...]
            ms = jnp.mean(xo * xo, axis=-1, keepdims=True)
            o_ref[...] = xo * lax.rsqrt(ms + EPS) * fg_ref[...]


def _moe(counts, x1, hf, rrow, rcol, wg, wu, wd, fg, final_norm):
    n, d = x1.shape
    tb, rows = MOE_BLOCK, MOE_ROWS
    de = wg.shape[2]
    cap = -(-tb // rows) * rows
    tokens = lambda w: pl.BlockSpec((tb, w), lambda i, g, e, c: (i, 0))
    grid_spec = pltpu.PrefetchScalarGridSpec(
        num_scalar_prefetch=1,
        grid=(n // tb, MOE_GROUPS, EXPERTS_PER_GROUP),
        in_specs=[tokens(d), tokens(d), pl.BlockSpec((8, tb), lambda i, g, e, c: (0, i)), tokens(LANES),
                  pl.BlockSpec((1, d, de), lambda i, g, e, c: (g * EXPERTS_PER_GROUP + e, 0, 0)),
                  pl.BlockSpec((1, d, de), lambda i, g, e, c: (g * EXPERTS_PER_GROUP + e, 0, 0)),
                  pl.BlockSpec((1, de, d), lambda i, g, e, c: (g * EXPERTS_PER_GROUP + e, 0, 0)),
                  pl.BlockSpec((1, d), lambda i, g, e, c: (0, 0))],
        out_specs=tokens(d),
        scratch_shapes=[pltpu.VMEM((cap, d), BF16), pltpu.VMEM((cap, d), F32), pltpu.VMEM((cap, LANES), F32),
                        pltpu.VMEM((8, tb), F32)],
    )
    return pl.pallas_call(
        functools.partial(_moe_kernel, tb=tb, rows=rows, final_norm=final_norm),
        grid_spec=grid_spec,
        out_shape=jax.ShapeDtypeStruct((n, d), F32),
        compiler_params=_params(("parallel", "arbitrary", "arbitrary")),
        name="moe",
    )(counts, x1, hf, rrow, rcol, wg, wu, wd, fg)


def _pad_lanes(v, width=LANES):
    v = v.reshape(1, -1).astype(F32)
    return jnp.pad(v, ((0, 0), (0, width - v.shape[1])))


def kernel(x, norm_mix, w_in, conv_dw_w, conv_dw_b, conv_ln_g, conv_ln_b, fgate_b, ssd_conv_w, ssd_conv_b,
           ssd_dt_bias, ssd_A_log, ssd_D, ssd_norm_g, w_out, norm_ffn, router_group_w, router_group_b,
           router_expert_w, router_expert_b, expert_w_gate, expert_w_up, expert_w_down, norm_final):
    bsz, seq, d = x.shape
    n = bsz * seq
    depth = w_in.shape[0]
    heads = fgate_b.shape[1]
    ssd_heads = ssd_A_log.shape[1]
    n_exp = expert_w_gate.shape[1]
    assert heads * HEAD_DIM == ATTN_W and ssd_heads * HEAD_DIM == SSD_W
    assert n_exp == MOE_GROUPS * EXPERTS_PER_GROUP and conv_dw_w.shape[1] <= CONV_HALO + 1
    assert n % ROW_TILE == 0 and n % MOE_BLOCK == 0 and seq % max(CONV_TILE, CUM_TILE, ATTN_TQ, SSD_T) == 0

    sizes = (2 * CONV_CH, ATTN_W, ATTN_W, ATTN_W, heads, SSD_W, SSD_W, SSD_GROUPS * SSD_STATE,
             SSD_GROUPS * SSD_STATE, ssd_heads)
    offs = [0]
    for s in sizes:
        offs.append(offs[-1] + s)
    expand = jnp.repeat(jnp.eye(LANES, SSD_W // HEAD_DIM, dtype=BF16), HEAD_DIM, axis=1)

    xr = x.reshape(n, d)
    for l in range(depth):
        w = w_in[l]
        wa = jnp.concatenate([w[:, offs[0]:offs[4]], w[:, offs[5]:offs[9]]], axis=1).astype(BF16)
        zpad = jnp.zeros((d, LANES - heads), F32)
        ws = jnp.concatenate([w[:, offs[4]:offs[5]], zpad, w[:, offs[9]:offs[10]], zpad], axis=1).astype(BF16)
        glu, q, k, v, z, xbc, fs, ds = _inproj(xr, norm_mix[l].reshape(1, d), wa, ws)

        seq3 = lambda a: a.reshape(bsz, seq, a.shape[-1])
        y_conv = _conv(seq3(glu), conv_dw_w[l], conv_dw_b[l].reshape(1, -1), conv_ln_g[l].reshape(1, -1),
                       conv_ln_b[l].reshape(1, -1))
        cum = _cum(seq3(fs), _pad_lanes(fgate_b[l]))
        y_attn = _attention(seq3(q), seq3(k), seq3(v), cum)
        y_ssd = _ssd(seq3(xbc), seq3(z), seq3(ds), ssd_conv_w[l], ssd_conv_b[l].reshape(1, -1),
                     _pad_lanes(ssd_dt_bias[l]), _pad_lanes(ssd_A_log[l]),
                     jnp.repeat(ssd_D[l].astype(F32), HEAD_DIM).reshape(1, -1), ssd_norm_g[l].reshape(1, -1),
                     expand)

        wr = jnp.concatenate([router_group_w[l], router_expert_w[l],
                              jnp.zeros((d, LANES - MOE_GROUPS - n_exp), F32)], axis=1)
        wr_hi = wr.astype(BF16)
        wr = jnp.stack([wr_hi, (wr - wr_hi.astype(F32)).astype(BF16)])
        br = _pad_lanes(jnp.concatenate([router_group_b[l], router_expert_b[l]]))
        x1, hf, rrow, rcol = _outproj(y_conv.reshape(n, -1), y_attn.reshape(n, -1), y_ssd.reshape(n, -1), xr,
                                      w_out[l].astype(BF16), norm_ffn[l].reshape(1, d), wr, br)

        gidx = rrow[0].astype(jnp.int32).reshape(n // MOE_BLOCK, MOE_BLOCK)
        counts = jnp.sum(gidx[:, :, None] == jnp.arange(MOE_GROUPS, dtype=jnp.int32), axis=1,
                         dtype=jnp.int32).reshape(-1)
        xr = _moe(counts, x1, hf, rrow, rcol, expert_w_gate[l].astype(BF16), expert_w_up[l].astype(BF16),
                  expert_w_down[l].astype(BF16), norm_final.reshape(1, d), final_norm=(l == depth - 1))
    return xr.reshape(bsz, seq, d)
```

```python
import functools

import jax
import jax.numpy as jnp
from jax import lax
from jax.experimental import pallas as pl
from jax.experimental.pallas import tpu as pltpu

F32 = jnp.float32
BF16 = jnp.bfloat16
HI = lax.Precision.HIGHEST
EPS = 1e-6
NEG_INF = float("-inf")

HEAD_DIM = 64
LANES = 128
CONV_CH = 512
ATTN_W = 512
SSD_W = 512
SSD_GROUPS = 2
SSD_STATE = 64
SSD_CONV_CH = SSD_W + 2 * SSD_GROUPS * SSD_STATE
MOE_GROUPS = 4
EXPERTS_PER_GROUP = 4

ROW_TILE = 512
CONV_TILE = 256
CONV_ROWS = 32
CONV_HALO = 32
SUBLANES = 8
CUM_TILE = 512
ATTN_TQ = 512
SSD_T = 256
SSD_HALO = 8
MOE_BLOCK = 1024
MOE_ROWS = 288
VMEM_LIMIT = 56 * 1024 * 1024


def _params(sem):
    return pltpu.CompilerParams(dimension_semantics=sem, vmem_limit_bytes=VMEM_LIMIT)


def _sigmoid(x):
    return jax.nn.sigmoid(x)


def _softplus(x):
    return jnp.maximum(x, 0.0) + jnp.log1p(jnp.exp(-jnp.abs(x)))


def _split3(x):
    hi = x.astype(BF16)
    r = x - hi.astype(F32)
    mid = r.astype(BF16)
    lo = (r - mid.astype(F32)).astype(BF16)
    return hi, mid, lo


def _select_dot(sel, x):
    hi, mid, lo = _split3(x)
    d = lambda p: jnp.dot(sel, p, preferred_element_type=F32)
    return d(hi) + d(mid) + d(lo)


def _dot_select(x, sel):
    hi, mid, lo = _split3(x)
    d = lambda p: jnp.dot(p, sel, preferred_element_type=F32)
    return d(hi) + d(mid) + d(lo)


def _rows8(w_ref, k, rows):
    return jnp.concatenate([w_ref[k * SUBLANES:(k + 1) * SUBLANES, :]] * (rows // SUBLANES), axis=0)


def _inproj_kernel(x_ref, g_ref, wa_ref, ws_ref, glu_ref, q_ref, k_ref, v_ref, z_ref, xbc_ref,
                   fs_ref, ds_ref):
    xf = x_ref[...]
    ms = jnp.mean(xf * xf, axis=-1, keepdims=True)
    h = (xf * lax.rsqrt(ms + EPS) * g_ref[...]).astype(BF16)

    def mm(lo, hi):
        return jnp.dot(h, wa_ref[:, lo:hi], preferred_element_type=F32)

    glu_ref[...] = mm(0, 512) * _sigmoid(mm(512, 1024))
    q_ref[...] = (mm(1024, 1536) * (HEAD_DIM ** -0.5)).astype(BF16)
    k_ref[...] = mm(1536, 2048).astype(BF16)
    v_ref[...] = mm(2048, 2560).astype(BF16)
    z_ref[...] = mm(2560, 3072)
    xbc_ref[...] = mm(3072, 3840)
    fs_ref[...] = jnp.dot(h, ws_ref[:, 0:LANES], preferred_element_type=F32)
    ds_ref[...] = jnp.dot(h, ws_ref[:, LANES:2 * LANES], preferred_element_type=F32)


def _inproj(x, g, wa, ws):
    n, d = x.shape
    tm = ROW_TILE
    row = lambda w: pl.BlockSpec((tm, w), lambda i: (i, 0))
    full = lambda a: pl.BlockSpec(a.shape, lambda i: (0, 0))
    shapes = [(512, F32), (512, BF16), (512, BF16), (512, BF16), (512, F32), (SSD_CONV_CH, F32),
              (LANES, F32), (LANES, F32)]
    return pl.pallas_call(
        _inproj_kernel,
        grid=(n // tm,),
        in_specs=[row(d), full(g), full(wa), full(ws)],
        out_specs=[row(w) for w, _ in shapes],
        out_shape=[jax.ShapeDtypeStruct((n, w), dt) for w, dt in shapes],
        compiler_params=_params(("parallel",)),
        name="inproj",
    )(x, g, wa, ws)


def _conv_kernel(x_ref, w_ref, b_ref, lg_ref, lb_ref, o_ref, buf_ref, *, taps, tl):
    li = pl.program_id(1)

    @pl.when(li == 0)
    def _():
        buf_ref[0:CONV_HALO, :] = jnp.zeros((CONV_HALO, CONV_CH), F32)

    @pl.when(li > 0)
    def _():
        buf_ref[0:CONV_HALO, :] = buf_ref[tl:tl + CONV_HALO, :]

    buf_ref[CONV_HALO:CONV_HALO + tl, :] = x_ref[0]
    first = CONV_HALO - (taps - 1)
    by_shift = [[k for k in range(taps) if (first + k) % SUBLANES == s] for s in range(SUBLANES)]
    for r in range(0, tl, CONV_ROWS):
        acc = jnp.broadcast_to(b_ref[...], (CONV_ROWS, CONV_CH))
        for s, ks in enumerate(by_shift):
            if not ks:
                continue
            span = CONV_ROWS + (SUBLANES if s else 0)
            z = None
            for k in ks:
                base = r + first + k - s
                term = _rows8(w_ref, k, span) * buf_ref[base:base + span, :]
                z = term if z is None else z + term
            acc = acc + z[s:s + CONV_ROWS, :]
        mu = jnp.mean(acc, axis=-1, keepdims=True)
        cen = acc - mu
        var = jnp.mean(cen * cen, axis=-1, keepdims=True)
        y = cen * lax.rsqrt(var + EPS) * lg_ref[...] + lb_ref[...]
        o_ref[0, r:r + CONV_ROWS, :] = (y * _sigmoid(y)).astype(BF16)


def _conv(glu, w, b, lg, lb):
    bsz, seq, c = glu.shape
    tl = CONV_TILE
    taps = w.shape[0]
    w = jnp.repeat(w, SUBLANES, axis=0)
    full = lambda a: pl.BlockSpec(a.shape, lambda bi, li: (0, 0))
    return pl.pallas_call(
        functools.partial(_conv_kernel, taps=taps, tl=tl),
        grid=(bsz, seq // tl),
        in_specs=[pl.BlockSpec((1, tl, c), lambda bi, li: (bi, li, 0)), full(w), full(b), full(lg), full(lb)],
        out_specs=pl.BlockSpec((1, tl, c), lambda bi, li: (bi, li, 0)),
        out_shape=jax.ShapeDtypeStruct((bsz, seq, c), BF16),
        scratch_shapes=[pltpu.VMEM((tl + CONV_HALO, c), F32)],
        compiler_params=_params(("parallel", "arbitrary")),
        name="dwconv",
    )(glu, w, b, lg, lb)


def _lower_tri(n, dtype, strict=False):
    r = lax.broadcasted_iota(jnp.int32, (n, n), 0)
    c = lax.broadcasted_iota(jnp.int32, (n, n), 1)
    keep = (c < r) if strict else (c <= r)
    return jnp.where(keep, 1.0, 0.0).astype(dtype)


def _cum_kernel(f_ref, b_ref, o_ref, carry_ref, *, tl):
    li = pl.program_id(1)

    @pl.when(li == 0)
    def _():
        carry_ref[...] = jnp.zeros_like(carry_ref)

    x = f_ref[0] + b_ref[...]
    lf = jnp.minimum(x, 0.0) - jnp.log1p(jnp.exp(-jnp.abs(x)))
    cum = _select_dot(_lower_tri(tl, BF16), lf) + carry_ref[...]
    carry_ref[...] = cum[tl - 1:tl, :]
    o_ref[0] = cum.T[0:8, :]


def _cum(fs, fb):
    bsz, seq, _ = fs.shape
    tl = CUM_TILE
    return pl.pallas_call(
        functools.partial(_cum_kernel, tl=tl),
        grid=(bsz, seq // tl),
        in_specs=[pl.BlockSpec((1, tl, LANES), lambda bi, li: (bi, li, 0)),
                  pl.BlockSpec((1, LANES), lambda bi, li: (0, 0))],
        out_specs=pl.BlockSpec((1, 8, tl), lambda bi, li: (bi, 0, li)),
        out_shape=jax.ShapeDtypeStruct((bsz, 8, seq), F32),
        scratch_shapes=[pltpu.VMEM((1, LANES), F32)],
        compiler_params=_params(("parallel", "arbitrary")),
        name="fgate_cumsum",
    )(fs, fb)


def _attn_kernel(q_ref, qn_ref, k_ref, v_ref, c_ref, o_ref, qm_ref, s_ref, m_ref, l_ref, acc_ref, *, tq, nq):
    hp = pl.program_id(1)
    qi = pl.program_id(2)
    slot = qi % 2
    nslot = 1 - slot
    lane = lax.broadcasted_iota(jnp.int32, (tq, LANES), 1)
    row = lax.broadcasted_iota(jnp.int32, (tq, tq), 0)
    col = lax.broadcasted_iota(jnp.int32, (tq, tq), 1)
    tiles = tq // LANES

    def start_tile(sl, q):
        for hh in range(2):
            in_head = (lane >= HEAD_DIM) if hh else (lane < HEAD_DIM)
            qm_ref[hh] = jnp.where(in_head, q, jnp.zeros_like(q))
            m_ref[sl, hh] = jnp.full((tq, LANES), NEG_INF, F32)

    def scores(sl, j, masked):
        k0 = pl.multiple_of(j * tq, tq)
        kb = k_ref[0, pl.ds(k0, tq), :]
        for hh in range(2):
            cb = c_ref[0, pl.ds(hp * 2 + hh, 1), pl.ds(k0, tq)]
            s = lax.dot_general(qm_ref[hh], kb, (((1,), (1,)), ((), ())), preferred_element_type=F32) - cb
            if masked:
                s = jnp.where(col <= row, s, NEG_INF)
            s_ref[sl, hh, j] = s
            mt = m_ref[sl, hh]
            for t in range(tiles):
                mt = jnp.maximum(mt, s[:, t * LANES:(t + 1) * LANES])
            m_ref[sl, hh] = mt

    def finish_tile(sl, diag):
        scores(sl, diag, True)
        for hh in range(2):
            m_ref[sl, hh] = jnp.broadcast_to(jnp.max(m_ref[sl, hh], axis=-1, keepdims=True), (tq, LANES))

    def weighted(j):
        k0 = pl.multiple_of(j * tq, tq)
        vb = v_ref[0, pl.ds(k0, tq), :]
        for hh in range(2):
            mb = m_ref[slot, hh]
            lt = l_ref[hh]
            ps = []
            for t in range(tiles):
                p = jnp.exp(s_ref[slot, hh, j, :, t * LANES:(t + 1) * LANES] - mb)
                lt = lt + p
                ps.append(p.astype(BF16))
            l_ref[hh] = lt
            acc_ref[hh] += jnp.dot(jnp.concatenate(ps, axis=1), vb, preferred_element_type=F32)

    @pl.when(qi == 0)
    def _():
        start_tile(0, q_ref[0])
        finish_tile(0, 0)

    for hh in range(2):
        l_ref[hh] = jnp.zeros((tq, LANES), F32)
        acc_ref[hh] = jnp.zeros((tq, LANES), F32)

    @pl.when(qi + 1 < nq)
    def _():
        start_tile(nslot, qn_ref[0])

        def both(j, carry):
            weighted(j)
            scores(nslot, j, False)
            return carry

        lax.fori_loop(0, qi + 1, both, 0)
        finish_tile(nslot, qi + 1)

    @pl.when(qi + 1 == nq)
    def _():
        def last(j, carry):
            weighted(j)
            return carry

        lax.fori_loop(0, qi + 1, last, 0)

    outs = [acc_ref[hh] / jnp.sum(l_ref[hh], axis=-1, keepdims=True) for hh in range(2)]
    o_ref[0] = jnp.where(lane < HEAD_DIM, outs[0], outs[1]).astype(BF16)


def _attention(q, k, v, cum):
    bsz, seq, w = q.shape
    tq = ATTN_TQ
    pairs = w // LANES
    nq = seq // tq
    return pl.pallas_call(
        functools.partial(_attn_kernel, tq=tq, nq=nq),
        grid=(bsz, pairs, nq),
        in_specs=[pl.BlockSpec((1, tq, LANES), lambda b, h, i: (b, i, h)),
                  pl.BlockSpec((1, tq, LANES), lambda b, h, i: (b, jnp.minimum(i + 1, nq - 1), h)),
                  pl.BlockSpec((1, seq, LANES), lambda b, h, i: (b, 0, h)),
                  pl.BlockSpec((1, seq, LANES), lambda b, h, i: (b, 0, h)),
                  pl.BlockSpec((1, 8, seq), lambda b, h, i: (b, 0, 0))],
        out_specs=pl.BlockSpec((1, tq, LANES), lambda b, h, i: (b, i, h)),
        out_shape=jax.ShapeDtypeStruct((bsz, seq, w), BF16),
        scratch_shapes=[pltpu.VMEM((2, tq, LANES), BF16), pltpu.VMEM((2, 2, nq, tq, tq), F32),
                        pltpu.VMEM((2, 2, tq, LANES), F32), pltpu.VMEM((2, tq, LANES), F32),
                        pltpu.VMEM((2, tq, LANES), F32)],
        compiler_params=_params(("parallel", "parallel", "arbitrary")),
        name="fox_attention",
    )(q, q, k, v, cum)


def _ssd_kernel(xbc_ref, z_ref, dt_ref, cw_ref, cb_ref, dtb_ref, alog_ref, dw_ref, ng_ref, ex_ref, o_ref,
                buf_ref, h_ref, *, T, taps):
    ci = pl.program_id(1)
    gw = SSD_W // SSD_GROUPS

    @pl.when(ci == 0)
    def _():
        buf_ref[0:SSD_HALO, :] = jnp.zeros((SSD_HALO, SSD_CONV_CH), F32)
        h_ref[...] = jnp.zeros_like(h_ref)

    @pl.when(ci > 0)
    def _():
        buf_ref[0:SSD_HALO, :] = buf_ref[T:T + SSD_HALO, :]

    buf_ref[SSD_HALO:SSD_HALO + T, :] = xbc_ref[0]
    first = SSD_HALO - (taps - 1)
    acc = jnp.broadcast_to(cb_ref[...], (T, SSD_CONV_CH))
    for k in range(taps):
        acc = acc + _rows8(cw_ref, k, T) * buf_ref[first + k:first + k + T, :]
    xc = acc * _sigmoid(acc)
    xs = xc[:, 0:SSD_W]
    b_mat = xc[:, SSD_W:SSD_W + LANES]
    c_mat = xc[:, SSD_W + LANES:SSD_W + 2 * LANES]

    dt = _softplus(dt_ref[0] + dtb_ref[...])
    a = dt * (-jnp.exp(alog_ref[...]))
    acum = _select_dot(_lower_tri(T, BF16), a)
    acum_row = acum.T
    expand = ex_ref[...]
    dt_w = _dot_select(dt, expand)
    acum_w = _dot_select(acum, expand)
    last_w = acum_w[T - 1:T, :]
    xdt = xs * dt_w
    xdt_b = xdt.astype(BF16)
    xdec_b = (xdt * jnp.exp(last_w - acum_w)).astype(BF16)
    eacum_w = jnp.exp(acum_w)
    chunk_decay = jnp.exp(last_w)
    bb = b_mat.astype(BF16)
    cc = c_mat.astype(BF16)
    bt = b_mat.T.astype(BF16)

    row = lax.broadcasted_iota(jnp.int32, (T, T), 0)
    col = lax.broadcasted_iota(jnp.int32, (T, T), 1)
    causal = col <= row
    lane = lax.broadcasted_iota(jnp.int32, (T, LANES), 1)
    h_in = h_ref[...]
    h_ref[...] = h_in * chunk_decay + jnp.dot(bt, xdec_b, preferred_element_type=F32)
    h_in_b = h_in.astype(BF16)
    pieces = []
    for g in range(SSD_GROUPS):
        in_group = (lane >= g * SSD_STATE) & (lane < (g + 1) * SSD_STATE)
        cg = jnp.where(in_group, cc, jnp.zeros_like(cc))
        cbm = lax.dot_general(cg, bb, (((1,), (1,)), ((), ())), preferred_element_type=F32)
        yoff = jnp.dot(cg, h_in_b[:, g * gw:(g + 1) * gw], preferred_element_type=F32)
        for pr in range(gw // LANES):
            lo = g * gw + pr * LANES
            xpair = xdt_b[:, lo:lo + LANES]
            res = []
            for hh in range(2):
                head = lo // HEAD_DIM + hh
                seg = acum[:, head:head + 1] - acum_row[head:head + 1, :]
                lmat = jnp.exp(jnp.where(causal, seg, NEG_INF))
                res.append(jnp.dot((cbm * lmat).astype(BF16), xpair, preferred_element_type=F32))
            ydiag = jnp.where(lane < HEAD_DIM, res[0], res[1])
            pieces.append(ydiag + yoff[:, pr * LANES:(pr + 1) * LANES] * eacum_w[:, lo:lo + LANES])
    y = jnp.concatenate(pieces, axis=1) + xs * dw_ref[...]
    zz = z_ref[0]
    gated = y * (zz * _sigmoid(zz))
    ms = jnp.mean(gated * gated, axis=-1, keepdims=True)
    o_ref[0] = (gated * lax.rsqrt(ms + EPS) * ng_ref[...]).astype(BF16)


def _ssd(xbc, z, ds, cw, cb, dtb, alog, dwide, ng, expand):
    bsz, seq, _ = xbc.shape
    T = SSD_T
    taps = cw.shape[0]
    cw = jnp.repeat(cw, SUBLANES, axis=0)
    gw = SSD_W // SSD_GROUPS
    full = lambda a: pl.BlockSpec(a.shape, lambda bi, ci: (0, 0))
    tok = lambda w: pl.BlockSpec((1, T, w), lambda bi, ci: (bi, ci, 0))
    return pl.pallas_call(
        functools.partial(_ssd_kernel, T=T, taps=taps),
        grid=(bsz, seq // T),
        in_specs=[tok(SSD_CONV_CH), tok(SSD_W), tok(LANES), full(cw), full(cb), full(dtb), full(alog),
                  full(dwide), full(ng), full(expand)],
        out_specs=tok(SSD_W),
        out_shape=jax.ShapeDtypeStruct((bsz, seq, SSD_W), BF16),
        scratch_shapes=[pltpu.VMEM((T + SSD_HALO, SSD_CONV_CH), F32),
                        pltpu.VMEM((SSD_GROUPS * SSD_STATE, SSD_W), F32)],
        compiler_params=_params(("parallel", "arbitrary")),
        name="ssd_mixer",
    )(xbc, z, ds, cw, cb, dtb, alog, dwide, ng, expand)


def _first_max(vals):
    best = vals[0]
    for v in vals[1:]:
        best = jnp.maximum(best, v)
    idx = jnp.full(best.shape, float(len(vals) - 1), F32)
    for j in range(len(vals) - 2, -1, -1):
        idx = jnp.where(vals[j] == best, float(j), idx)
    return best, idx


def _outproj_kernel(yc_ref, ya_ref, ys_ref, x_ref, wo_ref, g_ref, wr_ref, br_ref,
                    x1_ref, hf_ref, rrow_ref, rcol_ref, *, tm):
    y = jnp.dot(yc_ref[...], wo_ref[0:CONV_CH, :], preferred_element_type=F32)
    y = y + jnp.dot(ya_ref[...], wo_ref[CONV_CH:CONV_CH + ATTN_W, :], preferred_element_type=F32)
    y = y + jnp.dot(ys_ref[...], wo_ref[CONV_CH + ATTN_W:, :], preferred_element_type=F32)
    x1 = x_ref[...] + y
    x1_ref[...] = x1
    ms = jnp.mean(x1 * x1, axis=-1, keepdims=True)
    hf = x1 * lax.rsqrt(ms + EPS) * g_ref[...]
    hf_hi = hf.astype(BF16)
    hf_ref[...] = hf_hi
    hf_lo = (hf - hf_hi.astype(F32)).astype(BF16)
    logits = (jnp.dot(hf_hi, wr_ref[0], preferred_element_type=F32)
              + jnp.dot(hf_hi, wr_ref[1], preferred_element_type=F32)
              + jnp.dot(hf_lo, wr_ref[0], preferred_element_type=F32)) + br_ref[...]
    lt = logits.T
    gl = [lt[j:j + 1, :] for j in range(MOE_GROUPS)]
    gmax, gidx = _first_max(gl)
    denom = gl[0] * 0.0
    for v in gl:
        denom = denom + jnp.exp(v - gmax)
    gval = 1.0 / denom
    esel = []
    for j in range(EXPERTS_PER_GROUP):
        erow = lambda g: lt[MOE_GROUPS + g * EXPERTS_PER_GROUP + j:MOE_GROUPS + g * EXPERTS_PER_GROUP + j + 1, :]
        v = erow(MOE_GROUPS - 1)
        for g in range(MOE_GROUPS - 2, -1, -1):
            v = jnp.where(gidx == float(g), erow(g), v)
        esel.append(v)
    v1, i1 = _first_max(esel)
    rest = [jnp.where(i1 == float(j), NEG_INF, esel[j]) for j in range(EXPERTS_PER_GROUP)]
    v2, i2 = _first_max(rest)
    e2 = jnp.exp(v2 - v1)
    w1 = (1.0 / (1.0 + e2)) * gval
    w2 = (e2 / (1.0 + e2)) * gval
    cw = [jnp.where(i1 == float(j), w1, jnp.where(i2 == float(j), w2, 0.0)) for j in range(EXPERTS_PER_GROUP)]
    hi = [c.astype(BF16).astype(F32) for c in cw]
    mid = [(c - h).astype(BF16).astype(F32) for c, h in zip(cw, hi)]
    lo = [(c - h - m).astype(BF16).astype(F32) for c, h, m in zip(cw, hi, mid)]
    rows = [gidx] + hi + mid + lo
    ri = lax.broadcasted_iota(jnp.int32, (LANES, tm), 0)
    packed = jnp.zeros((LANES, tm), F32)
    for j, r in enumerate(rows):
        packed = jnp.where(ri == j, r, packed)
    rrow_ref[...] = packed[0:8, :]
    rcol_ref[...] = packed.T


def _outproj(yc, ya, ys, x, wo, g, wr, br):
    n, d = x.shape
    tm = ROW_TILE
    row = lambda w: pl.BlockSpec((tm, w), lambda i: (i, 0))
    full = lambda a: pl.BlockSpec(a.shape, lambda i: (0, 0))
    return pl.pallas_call(
        functools.partial(_outproj_kernel, tm=tm),
        grid=(n // tm,),
        in_specs=[row(CONV_CH), row(ATTN_W), row(SSD_W), row(d), full(wo), full(g),
                  pl.BlockSpec(wr.shape, lambda i: (0, 0, 0)), full(br)],
        out_specs=[row(d), row(d), pl.BlockSpec((8, tm), lambda i: (0, i)), row(LANES)],
        out_shape=[jax.ShapeDtypeStruct((n, d), F32), jax.ShapeDtypeStruct((n, d), BF16),
                   jax.ShapeDtypeStruct((8, n), F32), jax.ShapeDtypeStruct((n, LANES), F32)],
        compiler_params=_params(("parallel",)),
        name="outproj_router",
    )(yc, ya, ys, x, wo, g, wr, br)


def _moe_kernel(cnt_ref, x1_ref, hf_ref, rrow_ref, rcol_ref, wg_ref, wu_ref, wd_ref, fg_ref, o_ref,
                xg_ref, acc_ref, cw_ref, rankr_ref, *, tb, rows, final_norm):
    i = pl.program_id(0)
    g = pl.program_id(1)
    e = pl.program_id(2)
    cnt = cnt_ref[i * MOE_GROUPS + g]
    nch = (cnt + rows - 1) // rows
    gf = g.astype(F32)

    @pl.when((g == 0) & (e == 0))
    def _():
        o_ref[...] = x1_ref[...]
        gi_row = rrow_ref[0:1, :]
        sub = lax.broadcasted_iota(jnp.int32, (8, tb), 0).astype(F32)
        m_row = jnp.where(gi_row == sub, 1.0, 0.0).astype(BF16)
        r_ = lax.broadcasted_iota(jnp.int32, (tb, tb), 0)
        c_ = lax.broadcasted_iota(jnp.int32, (tb, tb), 1)
        before = jnp.where(r_ < c_, 1.0, 0.0).astype(BF16)
        rankr_ref[...] = jnp.dot(m_row, before, preferred_element_type=F32)

    def chunk(c):
        return pl.ds(pl.multiple_of(c * rows, 16), rows)

    @pl.when(e == 0)
    def _():
        gi_row = rrow_ref[0:1, :]
        sub8 = lax.broadcasted_iota(jnp.int32, (8, tb), 0)
        rank_g = jnp.sum(jnp.where(sub8 == g, rankr_ref[...], 0.0), axis=0, keepdims=True)
        slot = lax.broadcasted_iota(jnp.int32, (rows, tb), 0).astype(F32)
        rc_b = rcol_ref[...].astype(BF16)

        def gather(c, carry):
            base = (c * rows).astype(F32)
            p = jnp.where((rank_g == slot + base) & (gi_row == gf), 1.0, 0.0).astype(BF16)
            xg_ref[chunk(c), :] = jnp.dot(p, hf_ref[...], preferred_element_type=F32).astype(BF16)
            cw_ref[chunk(c), :] = jnp.dot(p, rc_b, preferred_element_type=F32)
            acc_ref[chunk(c), :] = jnp.zeros((rows, acc_ref.shape[1]), F32)
            return carry

        lax.fori_loop(0, nch, gather, 0)

    lane_r = lax.broadcasted_iota(jnp.int32, (rows, LANES), 1)
    pick = (lane_r == 1 + e) | (lane_r == 1 + EXPERTS_PER_GROUP + e) | (lane_r == 1 + 2 * EXPERTS_PER_GROUP + e)

    def expert(c, carry):
        xc = xg_ref[chunk(c), :]
        hg = jnp.dot(xc, wg_ref[0], preferred_element_type=F32)
        hu = jnp.dot(xc, wu_ref[0], preferred_element_type=F32)
        hh = (hg * _sigmoid(hg) * hu).astype(BF16)
        y = jnp.dot(hh, wd_ref[0], preferred_element_type=F32)
        cwe = jnp.sum(jnp.where(pick, cw_ref[chunk(c), :], 0.0), axis=-1, keepdims=True)
        acc_ref[chunk(c), :] += cwe * y
        return carry

    lax.fori_loop(0, nch, expert, 0)

    @pl.when(e == EXPERTS_PER_GROUP - 1)
    def _():
        gi_row = rrow_ref[0:1, :]
        sub8 = lax.broadcasted_iota(jnp.int32, (8, tb), 0)
        rank_g = jnp.sum(jnp.where(sub8 == g, rankr_ref[...], 0.0), axis=0, keepdims=True)
        slot = lax.broadcasted_iota(jnp.int32, (rows, tb), 0).astype(F32)

        def scatter(c, carry):
            base = (c * rows).astype(F32)
            p = jnp.where((rank_g == slot + base) & (gi_row == gf), 1.0, 0.0).astype(BF16)
            tn = (((0,), (0,)), ((), ()))
            o_ref[...] += lax.dot_general(p, acc_ref[chunk(c), :].astype(BF16), tn, preferred_element_type=F32)
            return carry

        lax.fori_loop(0, nch, scatter, 0)

    if final_norm:
        @pl.when((g == MOE_GROUPS - 1) & (e == EXPERTS_PER_GROUP - 1))
        def _():
            xo = o_ref[...]
            ms = jnp.mean(xo * xo, axis=-1, keepdims=True)
            o_ref[...] = xo * lax.rsqrt(ms + EPS) * fg_ref[...]


def _moe(counts, x1, hf, rrow, rcol, wg, wu, wd, fg, final_norm):
    n, d = x1.shape
    tb, rows = MOE_BLOCK, MOE_ROWS
    de = wg.shape[2]
    cap = -(-tb // rows) * rows
    tokens = lambda w: pl.BlockSpec((tb, w), lambda i, g, e, c: (i, 0))
    grid_spec = pltpu.PrefetchScalarGridSpec(
        num_scalar_prefetch=1,
        grid=(n // tb, MOE_GROUPS, EXPERTS_PER_GROUP),
        in_specs=[tokens(d), tokens(d), pl.BlockSpec((8, tb), lambda i, g, e, c: (0, i)), tokens(LANES),
                  pl.BlockSpec((1, d, de), lambda i, g, e, c: (g * EXPERTS_PER_GROUP + e, 0, 0)),
                  pl.BlockSpec((1, d, de), lambda i, g, e, c: (g * EXPERTS_PER_GROUP + e, 0, 0)),
                  pl.BlockSpec((1, de, d), lambda i, g, e, c: (g * EXPERTS_PER_GROUP + e, 0, 0)),
                  pl.BlockSpec((1, d), lambda i, g, e, c: (0, 0))],
        out_specs=tokens(d),
        scratch_shapes=[pltpu.VMEM((cap, d), BF16), pltpu.VMEM((cap, d), F32), pltpu.VMEM((cap, LANES), F32),
                        pltpu.VMEM((8, tb), F32)],
    )
    return pl.pallas_call(
        functools.partial(_moe_kernel, tb=tb, rows=rows, final_norm=final_norm),
        grid_spec=grid_spec,
        out_shape=jax.ShapeDtypeStruct((n, d), F32),
        compiler_params=_params(("parallel", "arbitrary", "arbitrary")),
        name="moe",
    )(counts, x1, hf, rrow, rcol, wg, wu, wd, fg)


def _pad_lanes(v, width=LANES):
    v = v.reshape(1, -1).astype(F32)
    return jnp.pad(v, ((0, 0), (0, width - v.shape[1])))


def kernel(x, norm_mix, w_in, conv_dw_w, conv_dw_b, conv_ln_g, conv_ln_b, fgate_b, ssd_conv_w, ssd_conv_b,
           ssd_dt_bias, ssd_A_log, ssd_D, ssd_norm_g, w_out, norm_ffn, router_group_w, router_group_b,
           router_expert_w, router_expert_b, expert_w_gate, expert_w_up, expert_w_down, norm_final):
    bsz, seq, d = x.shape
    n = bsz * seq
    depth = w_in.shape[0]
    heads = fgate_b.shape[1]
    ssd_heads = ssd_A_log.shape[1]
    n_exp = expert_w_gate.shape[1]
    assert heads * HEAD_DIM == ATTN_W and ssd_heads * HEAD_DIM == SSD_W
    assert n_exp == MOE_GROUPS * EXPERTS_PER_GROUP and conv_dw_w.shape[1] <= CONV_HALO + 1
    assert n % ROW_TILE == 0 and n % MOE_BLOCK == 0 and seq % max(CONV_TILE, CUM_TILE, ATTN_TQ, SSD_T) == 0

    sizes = (2 * CONV_CH, ATTN_W, ATTN_W, ATTN_W, heads, SSD_W, SSD_W, SSD_GROUPS * SSD_STATE,
             SSD_GROUPS * SSD_STATE, ssd_heads)
    offs = [0]
    for s in sizes:
        offs.append(offs[-1] + s)
    expand = jnp.repeat(jnp.eye(LANES, SSD_W // HEAD_DIM, dtype=BF16), HEAD_DIM, axis=1)

    xr = x.reshape(n, d)
    for l in range(depth):
        w = w_in[l]
        wa = jnp.concatenate([w[:, offs[0]:offs[4]], w[:, offs[5]:offs[9]]], axis=1).astype(BF16)
        zpad = jnp.zeros((d, LANES - heads), F32)
        ws = jnp.concatenate([w[:, offs[4]:offs[5]], zpad, w[:, offs[9]:offs[10]], zpad], axis=1).astype(BF16)
        glu, q, k, v, z, xbc, fs, ds = _inproj(xr, norm_mix[l].reshape(1, d), wa, ws)

        seq3 = lambda a: a.reshape(bsz, seq, a.shape[-1])
        y_conv = _conv(seq3(glu), conv_dw_w[l], conv_dw_b[l].reshape(1, -1), conv_ln_g[l].reshape(1, -1),
                       conv_ln_b[l].reshape(1, -1))
        cum = _cum(seq3(fs), _pad_lanes(fgate_b[l]))
        y_attn = _attention(seq3(q), seq3(k), seq3(v), cum)
        y_ssd = _ssd(seq3(xbc), seq3(z), seq3(ds), ssd_conv_w[l], ssd_conv_b[l].reshape(1, -1),
                     _pad_lanes(ssd_dt_bias[l]), _pad_lanes(ssd_A_log[l]),
                     jnp.repeat(ssd_D[l].astype(F32), HEAD_DIM).reshape(1, -1), ssd_norm_g[l].reshape(1, -1),
                     expand)

        wr = jnp.concatenate([router_group_w[l], router_expert_w[l],
                              jnp.zeros((d, LANES - MOE_GROUPS - n_exp), F32)], axis=1)
        wr_hi = wr.astype(BF16)
        wr = jnp.stack([wr_hi, (wr - wr_hi.astype(F32)).astype(BF16)])
        br = _pad_lanes(jnp.concatenate([router_group_b[l], router_expert_b[l]]))
        x1, hf, rrow, rcol = _outproj(y_conv.reshape(n, -1), y_attn.reshape(n, -1), y_ssd.reshape(n, -1), xr,
                                      w_out[l].astype(BF16), norm_ffn[l].reshape(1, d), wr, br)

        gidx = rrow[0].astype(jnp.int32).reshape(n // MOE_BLOCK, MOE_BLOCK)
        counts = jnp.sum(gidx[:, :, None] == jnp.arange(MOE_GROUPS, dtype=jnp.int32), axis=1,
                         dtype=jnp.int32).reshape(-1)
        xr = _moe(counts, x1, hf, rrow, rcol, expert_w_gate[l].astype(BF16), expert_w_up[l].astype(BF16),
                  expert_w_down[l].astype(BF16), norm_final.reshape(1, d), final_norm=(l == depth - 1))
    return xr.reshape(bsz, seq, d)
```

```python
import functools

import jax
import jax.numpy as jnp
from jax import lax
from jax.experimental import pallas as pl
from jax.experimental.pallas import tpu as pltpu

F32 = jnp.float32
BF16 = jnp.bfloat16
EPS = 1e-6
NEG_INF = float("-inf")

HEAD_DIM = 64
LANES = 128
CONV_CH = 512
ATTN_W = 512
SSD_W = 512
SSD_GROUPS = 2
SSD_STATE = 64
SSD_CONV_CH = SSD_W + 2 * SSD_GROUPS * SSD_STATE
MOE_GROUPS = 4
EXPERTS_PER_GROUP = 4

ROW_TILE = 512
CONV_TILE = 256
CONV_ROWS = 32
CONV_HALO = 32
SUBLANES = 8
CUM_TILE = 512
ATTN_TQ = 512
SSD_T = 256
SSD_HALO = 8
MOE_BLOCK = 2048
MOE_SORT = 1024
MOE_ROWS = 288
RANK_STRIP = 256
VMEM_LIMIT = 60 * 1024 * 1024


def _params(sem):
    return pltpu.CompilerParams(dimension_semantics=sem, vmem_limit_bytes=VMEM_LIMIT)


def _sigmoid(x):
    return jax.nn.sigmoid(x)


def _softplus(x):
    return jnp.maximum(x, 0.0) + jnp.log1p(jnp.exp(-jnp.abs(x)))


def _split3(x):
    hi = x.astype(BF16)
    r = x - hi.astype(F32)
    mid = r.astype(BF16)
    lo = (r - mid.astype(F32)).astype(BF16)
    return hi, mid, lo


def _select_dot(sel, x):
    hi, mid, lo = _split3(x)
    d = lambda p: jnp.dot(sel, p, preferred_element_type=F32)
    return d(hi) + d(mid) + d(lo)


def _dot_select(x, sel):
    hi, mid, lo = _split3(x)
    d = lambda p: jnp.dot(p, sel, preferred_element_type=F32)
    return d(hi) + d(mid) + d(lo)


def _rows8(w_ref, k, rows):
    return jnp.concatenate([w_ref[k * SUBLANES:(k + 1) * SUBLANES, :]] * (rows // SUBLANES), axis=0)


def _inproj_kernel(x_ref, g_ref, wa_ref, ws_ref, glu_ref, q_ref, k_ref, v_ref, z_ref, xbc_ref,
                   fs_ref, ds_ref):
    xf = x_ref[...]
    ms = jnp.mean(xf * xf, axis=-1, keepdims=True)
    h = (xf * lax.rsqrt(ms + EPS) * g_ref[...]).astype(BF16)

    def mm(lo, hi):
        return jnp.dot(h, wa_ref[:, lo:hi], preferred_element_type=F32)

    glu_ref[...] = mm(0, 512) * _sigmoid(mm(512, 1024))
    q_ref[...] = (mm(1024, 1536) * (HEAD_DIM ** -0.5)).astype(BF16)
    k_ref[...] = mm(1536, 2048).astype(BF16)
    v_ref[...] = mm(2048, 2560).astype(BF16)
    z_ref[...] = mm(2560, 3072)
    xbc_ref[...] = mm(3072, 3840)
    fs_ref[...] = jnp.dot(h, ws_ref[:, 0:LANES], preferred_element_type=F32)
    ds_ref[...] = jnp.dot(h, ws_ref[:, LANES:2 * LANES], preferred_element_type=F32)


def _inproj(x, g, wa, ws):
    n, d = x.shape
    tm = ROW_TILE
    row = lambda w: pl.BlockSpec((tm, w), lambda i: (i, 0))
    full = lambda a: pl.BlockSpec(a.shape, lambda i: (0, 0))
    shapes = [(512, F32), (512, BF16), (512, BF16), (512, BF16), (512, F32), (SSD_CONV_CH, F32),
              (LANES, F32), (LANES, F32)]
    return pl.pallas_call(
        _inproj_kernel,
        grid=(n // tm,),
        in_specs=[row(d), full(g), full(wa), full(ws)],
        out_specs=[row(w) for w, _ in shapes],
        out_shape=[jax.ShapeDtypeStruct((n, w), dt) for w, dt in shapes],
        compiler_params=_params(("parallel",)),
        name="inproj",
    )(x, g, wa, ws)


def _conv_kernel(x_ref, w_ref, b_ref, lg_ref, lb_ref, o_ref, buf_ref, *, taps, tl):
    li = pl.program_id(1)

    @pl.when(li == 0)
    def _():
        buf_ref[0:CONV_HALO, :] = jnp.zeros((CONV_HALO, CONV_CH), F32)

    @pl.when(li > 0)
    def _():
        buf_ref[0:CONV_HALO, :] = buf_ref[tl:tl + CONV_HALO, :]

    buf_ref[CONV_HALO:CONV_HALO + tl, :] = x_ref[0]
    first = CONV_HALO - (taps - 1)
    by_shift = [[k for k in range(taps) if (first + k) % SUBLANES == s] for s in range(SUBLANES)]
    for r in range(0, tl, CONV_ROWS):
        acc = jnp.broadcast_to(b_ref[...], (CONV_ROWS, CONV_CH))
        for s, ks in enumerate(by_shift):
            if not ks:
                continue
            span = CONV_ROWS + (SUBLANES if s else 0)
            z = None
            for k in ks:
                base = r + first + k - s
                term = _rows8(w_ref, k, span) * buf_ref[base:base + span, :]
                z = term if z is None else z + term
            acc = acc + z[s:s + CONV_ROWS, :]
        mu = jnp.mean(acc, axis=-1, keepdims=True)
        cen = acc - mu
        var = jnp.mean(cen * cen, axis=-1, keepdims=True)
        y = cen * lax.rsqrt(var + EPS) * lg_ref[...] + lb_ref[...]
        o_ref[0, r:r + CONV_ROWS, :] = (y * _sigmoid(y)).astype(BF16)


def _conv(glu, w, b, lg, lb):
    bsz, seq, c = glu.shape
    tl = CONV_TILE
    taps = w.shape[0]
    w = jnp.repeat(w, SUBLANES, axis=0)
    full = lambda a: pl.BlockSpec(a.shape, lambda bi, li: (0, 0))
    return pl.pallas_call(
        functools.partial(_conv_kernel, taps=taps, tl=tl),
        grid=(bsz, seq // tl),
        in_specs=[pl.BlockSpec((1, tl, c), lambda bi, li: (bi, li, 0)), full(w), full(b), full(lg), full(lb)],
        out_specs=pl.BlockSpec((1, tl, c), lambda bi, li: (bi, li, 0)),
        out_shape=jax.ShapeDtypeStruct((bsz, seq, c), BF16),
        scratch_shapes=[pltpu.VMEM((tl + CONV_HALO, c), F32)],
        compiler_params=_params(("parallel", "arbitrary")),
        name="dwconv",
    )(glu, w, b, lg, lb)


def _lower_tri(n, dtype, strict=False):
    r = lax.broadcasted_iota(jnp.int32, (n, n), 0)
    c = lax.broadcasted_iota(jnp.int32, (n, n), 1)
    keep = (c < r) if strict else (c <= r)
    return jnp.where(keep, 1.0, 0.0).astype(dtype)


def _cum_kernel(f_ref, b_ref, o_ref, carry_ref, *, tl):
    li = pl.program_id(1)

    @pl.when(li == 0)
    def _():
        carry_ref[...] = jnp.zeros_like(carry_ref)

    x = f_ref[0] + b_ref[...]
    lf = jnp.minimum(x, 0.0) - jnp.log1p(jnp.exp(-jnp.abs(x)))
    cum = _select_dot(_lower_tri(tl, BF16), lf) + carry_ref[...]
    carry_ref[...] = cum[tl - 1:tl, :]
    o_ref[0] = cum.T[0:8, :]


def _cum(fs, fb):
    bsz, seq, _ = fs.shape
    tl = CUM_TILE
    return pl.pallas_call(
        functools.partial(_cum_kernel, tl=tl),
        grid=(bsz, seq // tl),
        in_specs=[pl.BlockSpec((1, tl, LANES), lambda bi, li: (bi, li, 0)),
                  pl.BlockSpec((1, LANES), lambda bi, li: (0, 0))],
        out_specs=pl.BlockSpec((1, 8, tl), lambda bi, li: (bi, 0, li)),
        out_shape=jax.ShapeDtypeStruct((bsz, 8, seq), F32),
        scratch_shapes=[pltpu.VMEM((1, LANES), F32)],
        compiler_params=_params(("parallel", "arbitrary")),
        name="fgate_cumsum",
    )(fs, fb)


def _attn_kernel(q_ref, qn_ref, k_ref, v_ref, c_ref, o_ref, qm_ref, s_ref, m_ref, l_ref, acc_ref, *, tq, nq):
    hp = pl.program_id(1)
    qi = pl.program_id(2)
    slot = qi % 2
    nslot = 1 - slot
    lane = lax.broadcasted_iota(jnp.int32, (tq, LANES), 1)
    row = lax.broadcasted_iota(jnp.int32, (tq, tq), 0)
    col = lax.broadcasted_iota(jnp.int32, (tq, tq), 1)
    tiles = tq // LANES

    def start_tile(sl, q):
        for hh in range(2):
            in_head = (lane >= HEAD_DIM) if hh else (lane < HEAD_DIM)
            qm_ref[hh] = jnp.where(in_head, q, jnp.zeros_like(q))
            m_ref[sl, hh] = jnp.full((tq, LANES), NEG_INF, F32)

    def scores(sl, j, masked):
        k0 = pl.multiple_of(j * tq, tq)
        kb = k_ref[0, pl.ds(k0, tq), :]
        for hh in range(2):
            cb = c_ref[0, pl.ds(hp * 2 + hh, 1), pl.ds(k0, tq)]
            s = lax.dot_general(qm_ref[hh], kb, (((1,), (1,)), ((), ())), preferred_element_type=F32) - cb
            if masked:
                s = jnp.where(col <= row, s, NEG_INF)
            s_ref[sl, hh, j] = s
            mt = m_ref[sl, hh]
            for t in range(tiles):
                mt = jnp.maximum(mt, s[:, t * LANES:(t + 1) * LANES])
            m_ref[sl, hh] = mt

    def finish_tile(sl, diag):
        scores(sl, diag, True)
        for hh in range(2):
            m_ref[sl, hh] = jnp.broadcast_to(jnp.max(m_ref[sl, hh], axis=-1, keepdims=True), (tq, LANES))

    def weighted(j):
        k0 = pl.multiple_of(j * tq, tq)
        vb = v_ref[0, pl.ds(k0, tq), :]
        for hh in range(2):
            mb = m_ref[slot, hh]
            lt = l_ref[hh]
            ps = []
            for t in range(tiles):
                p = jnp.exp(s_ref[slot, hh, j, :, t * LANES:(t + 1) * LANES] - mb)
                lt = lt + p
                ps.append(p.astype(BF16))
            l_ref[hh] = lt
            acc_ref[hh] += jnp.dot(jnp.concatenate(ps, axis=1), vb, preferred_element_type=F32)

    @pl.when(qi == 0)
    def _():
        start_tile(0, q_ref[0])
        finish_tile(0, 0)

    for hh in range(2):
        l_ref[hh] = jnp.zeros((tq, LANES), F32)
        acc_ref[hh] = jnp.zeros((tq, LANES), F32)

    @pl.when(qi + 1 < nq)
    def _():
        start_tile(nslot, qn_ref[0])

        def both(j, carry):
            weighted(j)
            scores(nslot, j, False)
            return carry

        lax.fori_loop(0, qi + 1, both, 0)
        finish_tile(nslot, qi + 1)

    @pl.when(qi + 1 == nq)
    def _():
        def last(j, carry):
            weighted(j)
            return carry

        lax.fori_loop(0, qi + 1, last, 0)

    outs = [acc_ref[hh] / jnp.sum(l_ref[hh], axis=-1, keepdims=True) for hh in range(2)]
    o_ref[0] = jnp.where(lane < HEAD_DIM, outs[0], outs[1]).astype(BF16)


def _attention(q, k, v, cum):
    bsz, seq, w = q.shape
    tq = ATTN_TQ
    pairs = w // LANES
    nq = seq // tq
    return pl.pallas_call(
        functools.partial(_attn_kernel, tq=tq, nq=nq),
        grid=(bsz, pairs, nq),
        in_specs=[pl.BlockSpec((1, tq, LANES), lambda b, h, i: (b, i, h)),
                  pl.BlockSpec((1, tq, LANES), lambda b, h, i: (b, jnp.minimum(i + 1, nq - 1), h)),
                  pl.BlockSpec((1, seq, LANES), lambda b, h, i: (b, 0, h)),
                  pl.BlockSpec((1, seq, LANES), lambda b, h, i: (b, 0, h)),
                  pl.BlockSpec((1, 8, seq), lambda b, h, i: (b, 0, 0))],
        out_specs=pl.BlockSpec((1, tq, LANES), lambda b, h, i: (b, i, h)),
        out_shape=jax.ShapeDtypeStruct((bsz, seq, w), BF16),
        scratch_shapes=[pltpu.VMEM((2, tq, LANES), BF16), pltpu.VMEM((2, 2, nq, tq, tq), F32),
                        pltpu.VMEM((2, 2, tq, LANES), F32), pltpu.VMEM((2, tq, LANES), F32),
                        pltpu.VMEM((2, tq, LANES), F32)],
        compiler_params=_params(("parallel", "parallel", "arbitrary")),
        name="fox_attention",
    )(q, q, k, v, cum)


def _ssd_kernel(xbc_ref, z_ref, dt_ref, cw_ref, cb_ref, dtb_ref, alog_ref, dw_ref, ng_ref, ex_ref, o_ref,
                buf_ref, h_ref, *, T, taps):
    ci = pl.program_id(1)
    gw = SSD_W // SSD_GROUPS

    @pl.when(ci == 0)
    def _():
        buf_ref[0:SSD_HALO, :] = jnp.zeros((SSD_HALO, SSD_CONV_CH), F32)
        h_ref[...] = jnp.zeros_like(h_ref)

    @pl.when(ci > 0)
    def _():
        buf_ref[0:SSD_HALO, :] = buf_ref[T:T + SSD_HALO, :]

    buf_ref[SSD_HALO:SSD_HALO + T, :] = xbc_ref[0]
    first = SSD_HALO - (taps - 1)
    acc = jnp.broadcast_to(cb_ref[...], (T, SSD_CONV_CH))
    for k in range(taps):
        acc = acc + _rows8(cw_ref, k, T) * buf_ref[first + k:first + k + T, :]
    xc = acc * _sigmoid(acc)
    xs = xc[:, 0:SSD_W]
    b_mat = xc[:, SSD_W:SSD_W + LANES]
    c_mat = xc[:, SSD_W + LANES:SSD_W + 2 * LANES]

    dt = _softplus(dt_ref[0] + dtb_ref[...])
    a = dt * (-jnp.exp(alog_ref[...]))
    acum = _select_dot(_lower_tri(T, BF16), a)
    acum_row = acum.T
    expand = ex_ref[...]
    dt_w = _dot_select(dt, expand)
    acum_w = _dot_select(acum, expand)
    last_w = acum_w[T - 1:T, :]
    xdt = xs * dt_w
    xdt_b = xdt.astype(BF16)
    xdec_b = (xdt * jnp.exp(last_w - acum_w)).astype(BF16)
    eacum_w = jnp.exp(acum_w)
    chunk_decay = jnp.exp(last_w)
    bb = b_mat.astype(BF16)
    cc = c_mat.astype(BF16)
    bt = b_mat.T.astype(BF16)

    row = lax.broadcasted_iota(jnp.int32, (T, T), 0)
    col = lax.broadcasted_iota(jnp.int32, (T, T), 1)
    causal = col <= row
    lane = lax.broadcasted_iota(jnp.int32, (T, LANES), 1)
    h_in = h_ref[...]
    h_ref[...] = h_in * chunk_decay + jnp.dot(bt, xdec_b, preferred_element_type=F32)
    h_in_b = h_in.astype(BF16)
    pieces = []
    for g in range(SSD_GROUPS):
        in_group = (lane >= g * SSD_STATE) & (lane < (g + 1) * SSD_STATE)
        cg = jnp.where(in_group, cc, jnp.zeros_like(cc))
        cbm = lax.dot_general(cg, bb, (((1,), (1,)), ((), ())), preferred_element_type=F32)
        yoff = jnp.dot(cg, h_in_b[:, g * gw:(g + 1) * gw], preferred_element_type=F32)
        for pr in range(gw // LANES):
            lo = g * gw + pr * LANES
            xpair = xdt_b[:, lo:lo + LANES]
            res = []
            for hh in range(2):
                head = lo // HEAD_DIM + hh
                seg = acum[:, head:head + 1] - acum_row[head:head + 1, :]
                lmat = jnp.exp(jnp.where(causal, seg, NEG_INF))
                res.append(jnp.dot((cbm * lmat).astype(BF16), xpair, preferred_element_type=F32))
            ydiag = jnp.where(lane < HEAD_DIM, res[0], res[1])
            pieces.append(ydiag + yoff[:, pr * LANES:(pr + 1) * LANES] * eacum_w[:, lo:lo + LANES])
    y = jnp.concatenate(pieces, axis=1) + xs * dw_ref[...]
    zz = z_ref[0]
    gated = y * (zz * _sigmoid(zz))
    ms = jnp.mean(gated * gated, axis=-1, keepdims=True)
    o_ref[0] = (gated * lax.rsqrt(ms + EPS) * ng_ref[...]).astype(BF16)


def _ssd(xbc, z, ds, cw, cb, dtb, alog, dwide, ng, expand):
    bsz, seq, _ = xbc.shape
    T = SSD_T
    taps = cw.shape[0]
    cw = jnp.repeat(cw, SUBLANES, axis=0)
    full = lambda a: pl.BlockSpec(a.shape, lambda bi, ci: (0, 0))
    tok = lambda w: pl.BlockSpec((1, T, w), lambda bi, ci: (bi, ci, 0))
    return pl.pallas_call(
        functools.partial(_ssd_kernel, T=T, taps=taps),
        grid=(bsz, seq // T),
        in_specs=[tok(SSD_CONV_CH), tok(SSD_W), tok(LANES), full(cw), full(cb), full(dtb), full(alog),
                  full(dwide), full(ng), full(expand)],
        out_specs=tok(SSD_W),
        out_shape=jax.ShapeDtypeStruct((bsz, seq, SSD_W), BF16),
        scratch_shapes=[pltpu.VMEM((T + SSD_HALO, SSD_CONV_CH), F32),
                        pltpu.VMEM((SSD_GROUPS * SSD_STATE, SSD_W), F32)],
        compiler_params=_params(("parallel", "arbitrary")),
        name="ssd_mixer",
    )(xbc, z, ds, cw, cb, dtb, alog, dwide, ng, expand)


def _first_max(vals):
    best = vals[0]
    for v in vals[1:]:
        best = jnp.maximum(best, v)
    idx = jnp.full(best.shape, float(len(vals) - 1), F32)
    for j in range(len(vals) - 2, -1, -1):
        idx = jnp.where(vals[j] == best, float(j), idx)
    return best, idx


def _outproj_kernel(yc_ref, ya_ref, ys_ref, x_ref, wo_ref, g_ref, wr_ref, br_ref,
                    x1_ref, hf_ref, rrow_ref, rcol_ref, *, tm):
    y = jnp.dot(yc_ref[...], wo_ref[0:CONV_CH, :], preferred_element_type=F32)
    y = y + jnp.dot(ya_ref[...], wo_ref[CONV_CH:CONV_CH + ATTN_W, :], preferred_element_type=F32)
    y = y + jnp.dot(ys_ref[...], wo_ref[CONV_CH + ATTN_W:, :], preferred_element_type=F32)
    x1 = x_ref[...] + y
    x1_ref[...] = x1
    ms = jnp.mean(x1 * x1, axis=-1, keepdims=True)
    hf = x1 * lax.rsqrt(ms + EPS) * g_ref[...]
    hf_hi = hf.astype(BF16)
    hf_ref[...] = hf_hi
    hf_lo = (hf - hf_hi.astype(F32)).astype(BF16)
    logits = (jnp.dot(hf_hi, wr_ref[0], preferred_element_type=F32)
              + jnp.dot(hf_hi, wr_ref[1], preferred_element_type=F32)
              + jnp.dot(hf_lo, wr_ref[0], preferred_element_type=F32)) + br_ref[...]
    lt = logits.T
    gl = [lt[j:j + 1, :] for j in range(MOE_GROUPS)]
    gmax, gidx = _first_max(gl)
    denom = gl[0] * 0.0
    for v in gl:
        denom = denom + jnp.exp(v - gmax)
    gval = 1.0 / denom
    esel = []
    for j in range(EXPERTS_PER_GROUP):
        erow = lambda g: lt[MOE_GROUPS + g * EXPERTS_PER_GROUP + j:MOE_GROUPS + g * EXPERTS_PER_GROUP + j + 1, :]
        v = erow(MOE_GROUPS - 1)
        for g in range(MOE_GROUPS - 2, -1, -1):
            v = jnp.where(gidx == float(g), erow(g), v)
        esel.append(v)
    v1, i1 = _first_max(esel)
    rest = [jnp.where(i1 == float(j), NEG_INF, esel[j]) for j in range(EXPERTS_PER_GROUP)]
    v2, i2 = _first_max(rest)
    e2 = jnp.exp(v2 - v1)
    w1 = (1.0 / (1.0 + e2)) * gval
    w2 = (e2 / (1.0 + e2)) * gval
    cw = [jnp.where(i1 == float(j), w1, jnp.where(i2 == float(j), w2, 0.0)) for j in range(EXPERTS_PER_GROUP)]
    hi = [c.astype(BF16).astype(F32) for c in cw]
    mid = [(c - h).astype(BF16).astype(F32) for c, h in zip(cw, hi)]
    lo = [(c - h - m).astype(BF16).astype(F32) for c, h, m in zip(cw, hi, mid)]
    rows = [gidx] + hi + mid + lo
    ri = lax.broadcasted_iota(jnp.int32, (LANES, tm), 0)
    packed = jnp.zeros((LANES, tm), F32)
    for j, r in enumerate(rows):
        packed = jnp.where(ri == j, r, packed)
    rrow_ref[...] = packed[0:8, :]
    rcol_ref[...] = packed.T


def _outproj(yc, ya, ys, x, wo, g, wr, br):
    n, d = x.shape
    tm = ROW_TILE
    row = lambda w: pl.BlockSpec((tm, w), lambda i: (i, 0))
    full = lambda a: pl.BlockSpec(a.shape, lambda i: (0, 0))
    return pl.pallas_call(
        functools.partial(_outproj_kernel, tm=tm),
        grid=(n // tm,),
        in_specs=[row(CONV_CH), row(ATTN_W), row(SSD_W), row(d), full(wo), full(g),
                  pl.BlockSpec(wr.shape, lambda i: (0, 0, 0)), full(br)],
        out_specs=[row(d), row(d), pl.BlockSpec((8, tm), lambda i: (0, i)), row(LANES)],
        out_shape=[jax.ShapeDtypeStruct((n, d), F32), jax.ShapeDtypeStruct((n, d), BF16),
                   jax.ShapeDtypeStruct((8, n), F32), jax.ShapeDtypeStruct((n, LANES), F32)],
        compiler_params=_params(("parallel",)),
        name="outproj_router",
    )(yc, ya, ys, x, wo, g, wr, br)


def _moe_kernel(cnt_ref, x1_ref, hf_ref, rrow_ref, rcol_ref, wg_ref, wu_ref, wd_ref, fg_ref, o_ref,
                xg_ref, acc_ref, cw_ref, rankr_ref, *, ts, spans, rows, cap, final_norm):
    i = pl.program_id(0)
    g = pl.program_id(1)
    e = pl.program_id(2)
    gf = g.astype(F32)
    nch = [(cnt_ref[(i * spans + sb) * MOE_GROUPS + g] + rows - 1) // rows for sb in range(spans)]

    def tok(sb):
        return slice(sb * ts, (sb + 1) * ts)

    def chunk(sb, c):
        return pl.ds(pl.multiple_of(sb * cap + c * rows, 16), rows)

    def group_rank(sb):
        sub8 = lax.broadcasted_iota(jnp.int32, (8, ts), 0)
        return jnp.sum(jnp.where(sub8 == g, rankr_ref[:, tok(sb)], 0.0), axis=0, keepdims=True)

    def placement(sb, c, rank_g):
        slot = lax.broadcasted_iota(jnp.int32, (rows, ts), 0).astype(F32) + (c * rows).astype(F32)
        return jnp.where((rank_g == slot) & (rrow_ref[0:1, tok(sb)] == gf), 1.0, 0.0).astype(BF16)

    @pl.when((g == 0) & (e == 0))
    def _():
        o_ref[...] = x1_ref[...]
        sub = lax.broadcasted_iota(jnp.int32, (8, ts), 0).astype(F32)
        for c0 in range(0, ts, RANK_STRIP):
            r_ = lax.broadcasted_iota(jnp.int32, (ts, RANK_STRIP), 0)
            c_ = lax.broadcasted_iota(jnp.int32, (ts, RANK_STRIP), 1) + c0
            before = jnp.where(r_ < c_, 1.0, 0.0).astype(BF16)
            for sb in range(spans):
                m_row = jnp.where(rrow_ref[0:1, tok(sb)] == sub, 1.0, 0.0).astype(BF16)
                rankr_ref[:, sb * ts + c0:sb * ts + c0 + RANK_STRIP] = jnp.dot(
                    m_row, before, preferred_element_type=F32)

    @pl.when(e == 0)
    def _():
        for sb in range(spans):
            rank_g = group_rank(sb)
            rc_b = rcol_ref[tok(sb), :].astype(BF16)

            def gather(c, carry, sb=sb, rank_g=rank_g, rc_b=rc_b):
                p = placement(sb, c, rank_g)
                xg_ref[chunk(sb, c), :] = jnp.dot(p, hf_ref[tok(sb), :], preferred_element_type=F32).astype(BF16)
                cw_ref[chunk(sb, c), :] = jnp.dot(p, rc_b, preferred_element_type=F32)
                acc_ref[chunk(sb, c), :] = jnp.zeros((rows, acc_ref.shape[1]), F32)
                return carry

            lax.fori_loop(0, nch[sb], gather, 0)

    lane_r = lax.broadcasted_iota(jnp.int32, (rows, LANES), 1)
    pick = (lane_r == 1 + e) | (lane_r == 1 + EXPERTS_PER_GROUP + e) | (lane_r == 1 + 2 * EXPERTS_PER_GROUP + e)
    for sb in range(spans):
        def expert(c, carry, sb=sb):
            xc = xg_ref[chunk(sb, c), :]
            hg = jnp.dot(xc, wg_ref[0], preferred_element_type=F32)
            hu = jnp.dot(xc, wu_ref[0], preferred_element_type=F32)
            hh = (hg * _sigmoid(hg) * hu).astype(BF16)
            y = jnp.dot(hh, wd_ref[0], preferred_element_type=F32)
            cwe = jnp.sum(jnp.where(pick, cw_ref[chunk(sb, c), :], 0.0), axis=-1, keepdims=True)
            acc_ref[chunk(sb, c), :] += cwe * y
            return carry

        lax.fori_loop(0, nch[sb], expert, 0)

    @pl.when(e == EXPERTS_PER_GROUP - 1)
    def _():
        for sb in range(spans):
            rank_g = group_rank(sb)

            def scatter(c, carry, sb=sb, rank_g=rank_g):
                p = placement(sb, c, rank_g)
                tn = (((0,), (0,)), ((), ()))
                o_ref[tok(sb), :] += lax.dot_general(p, acc_ref[chunk(sb, c), :].astype(BF16), tn,
                                                     preferred_element_type=F32)
                return carry

            lax.fori_loop(0, nch[sb], scatter, 0)

    if final_norm:
        @pl.when((g == MOE_GROUPS - 1) & (e == EXPERTS_PER_GROUP - 1))
        def _():
            xo = o_ref[...]
            ms = jnp.mean(xo * xo, axis=-1, keepdims=True)
            o_ref[...] = xo * lax.rsqrt(ms + EPS) * fg_ref[...]


def _moe(counts, x1, hf, rrow, rcol, wg, wu, wd, fg, final_norm):
    n, d = x1.shape
    tb, ts, rows = MOE_BLOCK, MOE_SORT, MOE_ROWS
    spans = tb // ts
    de = wg.shape[2]
    cap = -(-ts // rows) * rows
    once = pl.Buffered(1)
    tokens = lambda w, **kw: pl.BlockSpec((tb, w), lambda i, g, e, c: (i, 0), **kw)
    expert = lambda shape: pl.BlockSpec(shape, lambda i, g, e, c: (g * EXPERTS_PER_GROUP + e, 0, 0))
    grid_spec = pltpu.PrefetchScalarGridSpec(
        num_scalar_prefetch=1,
        grid=(n // tb, MOE_GROUPS, EXPERTS_PER_GROUP),
        in_specs=[tokens(d, pipeline_mode=once), tokens(d, pipeline_mode=once),
                  pl.BlockSpec((8, tb), lambda i, g, e, c: (0, i)), tokens(LANES),
                  expert((1, d, de)), expert((1, d, de)), expert((1, de, d)),
                  pl.BlockSpec((1, d), lambda i, g, e, c: (0, 0))],
        out_specs=tokens(d),
        scratch_shapes=[pltpu.VMEM((spans * cap, d), BF16), pltpu.VMEM((spans * cap, d), F32),
                        pltpu.VMEM((spans * cap, LANES), F32), pltpu.VMEM((8, tb), F32)],
    )
    return pl.pallas_call(
        functools.partial(_moe_kernel, ts=ts, spans=spans, rows=rows, cap=cap, final_norm=final_norm),
        grid_spec=grid_spec,
        out_shape=jax.ShapeDtypeStruct((n, d), F32),
        compiler_params=_params(("parallel", "arbitrary", "arbitrary")),
        name="moe",
    )(counts, x1, hf, rrow, rcol, wg, wu, wd, fg)


def _pad_lanes(v, width=LANES):
    v = v.reshape(1, -1).astype(F32)
    return jnp.pad(v, ((0, 0), (0, width - v.shape[1])))


def kernel(x, norm_mix, w_in, conv_dw_w, conv_dw_b, conv_ln_g, conv_ln_b, fgate_b, ssd_conv_w, ssd_conv_b,
           ssd_dt_bias, ssd_A_log, ssd_D, ssd_norm_g, w_out, norm_ffn, router_group_w, router_group_b,
           router_expert_w, router_expert_b, expert_w_gate, expert_w_up, expert_w_down, norm_final):
    bsz, seq, d = x.shape
    n = bsz * seq
    depth = w_in.shape[0]
    heads = fgate_b.shape[1]
    ssd_heads = ssd_A_log.shape[1]
    n_exp = expert_w_gate.shape[1]
    assert heads * HEAD_DIM == ATTN_W and ssd_heads * HEAD_DIM == SSD_W
    assert n_exp == MOE_GROUPS * EXPERTS_PER_GROUP and conv_dw_w.shape[1] <= CONV_HALO + 1
    assert n % ROW_TILE == 0 and n % MOE_BLOCK == 0 and seq % max(CONV_TILE, CUM_TILE, ATTN_TQ, SSD_T) == 0

    sizes = (2 * CONV_CH, ATTN_W, ATTN_W, ATTN_W, heads, SSD_W, SSD_W, SSD_GROUPS * SSD_STATE,
             SSD_GROUPS * SSD_STATE, ssd_heads)
    offs = [0]
    for s in sizes:
        offs.append(offs[-1] + s)
    expand = jnp.repeat(jnp.eye(LANES, SSD_W // HEAD_DIM, dtype=BF16), HEAD_DIM, axis=1)

    xr = x.reshape(n, d)
    for l in range(depth):
        w = w_in[l]
        wa = jnp.concatenate([w[:, offs[0]:offs[4]], w[:, offs[5]:offs[9]]], axis=1).astype(BF16)
        zpad = jnp.zeros((d, LANES - heads), F32)
        ws = jnp.concatenate([w[:, offs[4]:offs[5]], zpad, w[:, offs[9]:offs[10]], zpad], axis=1).astype(BF16)
        glu, q, k, v, z, xbc, fs, ds = _inproj(xr, norm_mix[l].reshape(1, d), wa, ws)

        seq3 = lambda a: a.reshape(bsz, seq, a.shape[-1])
        y_conv = _conv(seq3(glu), conv_dw_w[l], conv_dw_b[l].reshape(1, -1), conv_ln_g[l].reshape(1, -1),
                       conv_ln_b[l].reshape(1, -1))
        cum = _cum(seq3(fs), _pad_lanes(fgate_b[l]))
        y_attn = _attention(seq3(q), seq3(k), seq3(v), cum)
        y_ssd = _ssd(seq3(xbc), seq3(z), seq3(ds), ssd_conv_w[l], ssd_conv_b[l].reshape(1, -1),
                     _pad_lanes(ssd_dt_bias[l]), _pad_lanes(ssd_A_log[l]),
                     jnp.repeat(ssd_D[l].astype(F32), HEAD_DIM).reshape(1, -1), ssd_norm_g[l].reshape(1, -1),
                     expand)

        wr = jnp.concatenate([router_group_w[l], router_expert_w[l],
                              jnp.zeros((d, LANES - MOE_GROUPS - n_exp), F32)], axis=1)
        wr_hi = wr.astype(BF16)
        wr = jnp.stack([wr_hi, (wr - wr_hi.astype(F32)).astype(BF16)])
        br = _pad_lanes(jnp.concatenate([router_group_b[l], router_expert_b[l]]))
        x1, hf, rrow, rcol = _outproj(y_conv.reshape(n, -1), y_attn.reshape(n, -1), y_ssd.reshape(n, -1), xr,
                                      w_out[l].astype(BF16), norm_ffn[l].reshape(1, d), wr, br)

        gidx = rrow[0].astype(jnp.int32).reshape(n // MOE_SORT, MOE_SORT)
        counts = jnp.sum(gidx[:, :, None] == jnp.arange(MOE_GROUPS, dtype=jnp.int32), axis=1,
                         dtype=jnp.int32).reshape(-1)
        xr = _moe(counts, x1, hf, rrow, rcol, expert_w_gate[l].astype(BF16), expert_w_up[l].astype(BF16),
                  expert_w_down[l].astype(BF16), norm_final.reshape(1, d), final_norm=(l == depth - 1))
    return xr.reshape(bsz, seq, d)
```

```python
import functools

import jax
import jax.numpy as jnp
from jax import lax
from jax.experimental import pallas as pl
from jax.experimental.pallas import tpu as pltpu

F32 = jnp.float32
BF16 = jnp.bfloat16
EPS = 1e-6
NEG_INF = float("-inf")

HEAD_DIM = 64
LANES = 128
CONV_CH = 512
ATTN_W = 512
SSD_W = 512
SSD_GROUPS = 2
SSD_STATE = 64
SSD_CONV_CH = SSD_W + 2 * SSD_GROUPS * SSD_STATE
MOE_GROUPS = 4
EXPERTS_PER_GROUP = 4

ROW_TILE = 512
CONV_TILE = 256
CONV_ROWS = 32
CONV_HALO = 32
SUBLANES = 8
CUM_TILE = 512
ATTN_TQ = 512
SSD_T = 256
SSD_HALO = 8
MOE_BLOCK = 2048
MOE_SORT = 1024
MOE_ROWS = (256, 288, 320, 384)
RANK_STRIP = 256
VMEM_LIMIT = 60 * 1024 * 1024


def _params(sem):
    return pltpu.CompilerParams(dimension_semantics=sem, vmem_limit_bytes=VMEM_LIMIT)


def _sigmoid(x):
    return jax.nn.sigmoid(x)


def _softplus(x):
    return jnp.maximum(x, 0.0) + jnp.log1p(jnp.exp(-jnp.abs(x)))


def _split3(x):
    hi = x.astype(BF16)
    r = x - hi.astype(F32)
    mid = r.astype(BF16)
    lo = (r - mid.astype(F32)).astype(BF16)
    return hi, mid, lo


def _select_dot(sel, x):
    hi, mid, lo = _split3(x)
    d = lambda p: jnp.dot(sel, p, preferred_element_type=F32)
    return d(hi) + d(mid) + d(lo)


def _dot_select(x, sel):
    hi, mid, lo = _split3(x)
    d = lambda p: jnp.dot(p, sel, preferred_element_type=F32)
    return d(hi) + d(mid) + d(lo)


def _rows8(w_ref, k, rows):
    return jnp.concatenate([w_ref[k * SUBLANES:(k + 1) * SUBLANES, :]] * (rows // SUBLANES), axis=0)


def _inproj_kernel(x_ref, g_ref, wa_ref, ws_ref, glu_ref, q_ref, k_ref, v_ref, z_ref, xbc_ref,
                   fs_ref, ds_ref):
    xf = x_ref[...]
    ms = jnp.mean(xf * xf, axis=-1, keepdims=True)
    h = (xf * lax.rsqrt(ms + EPS) * g_ref[...]).astype(BF16)

    def mm(lo, hi):
        return jnp.dot(h, wa_ref[:, lo:hi], preferred_element_type=F32)

    glu_ref[...] = mm(0, 512) * _sigmoid(mm(512, 1024))
    q_ref[...] = (mm(1024, 1536) * (HEAD_DIM ** -0.5)).astype(BF16)
    k_ref[...] = mm(1536, 2048).astype(BF16)
    v_ref[...] = mm(2048, 2560).astype(BF16)
    z_ref[...] = mm(2560, 3072)
    xbc_ref[...] = mm(3072, 3840)
    fs_ref[...] = jnp.dot(h, ws_ref[:, 0:LANES], preferred_element_type=F32)
    ds_ref[...] = jnp.dot(h, ws_ref[:, LANES:2 * LANES], preferred_element_type=F32)


def _inproj(x, g, wa, ws):
    n, d = x.shape
    tm = ROW_TILE
    row = lambda w: pl.BlockSpec((tm, w), lambda i: (i, 0))
    full = lambda a: pl.BlockSpec(a.shape, lambda i: (0, 0))
    shapes = [(512, F32), (512, BF16), (512, BF16), (512, BF16), (512, F32), (SSD_CONV_CH, F32),
              (LANES, F32), (LANES, F32)]
    return pl.pallas_call(
        _inproj_kernel,
        grid=(n // tm,),
        in_specs=[row(d), full(g), full(wa), full(ws)],
        out_specs=[row(w) for w, _ in shapes],
        out_shape=[jax.ShapeDtypeStruct((n, w), dt) for w, dt in shapes],
        compiler_params=_params(("parallel",)),
        name="inproj",
    )(x, g, wa, ws)


def _conv_kernel(x_ref, w_ref, b_ref, lg_ref, lb_ref, o_ref, buf_ref, *, taps, tl):
    li = pl.program_id(1)

    @pl.when(li == 0)
    def _():
        buf_ref[0:CONV_HALO, :] = jnp.zeros((CONV_HALO, CONV_CH), F32)

    @pl.when(li > 0)
    def _():
        buf_ref[0:CONV_HALO, :] = buf_ref[tl:tl + CONV_HALO, :]

    buf_ref[CONV_HALO:CONV_HALO + tl, :] = x_ref[0]
    first = CONV_HALO - (taps - 1)
    by_shift = [[k for k in range(taps) if (first + k) % SUBLANES == s] for s in range(SUBLANES)]
    for r in range(0, tl, CONV_ROWS):
        acc = jnp.broadcast_to(b_ref[...], (CONV_ROWS, CONV_CH))
        for s, ks in enumerate(by_shift):
            if not ks:
                continue
            span = CONV_ROWS + (SUBLANES if s else 0)
            z = None
            for k in ks:
                base = r + first + k - s
                term = _rows8(w_ref, k, span) * buf_ref[base:base + span, :]
                z = term if z is None else z + term
            acc = acc + z[s:s + CONV_ROWS, :]
        mu = jnp.mean(acc, axis=-1, keepdims=True)
        cen = acc - mu
        var = jnp.mean(cen * cen, axis=-1, keepdims=True)
        y = cen * lax.rsqrt(var + EPS) * lg_ref[...] + lb_ref[...]
        o_ref[0, r:r + CONV_ROWS, :] = (y * _sigmoid(y)).astype(BF16)


def _conv(glu, w, b, lg, lb):
    bsz, seq, c = glu.shape
    tl = CONV_TILE
    taps = w.shape[0]
    w = jnp.repeat(w, SUBLANES, axis=0)
    full = lambda a: pl.BlockSpec(a.shape, lambda bi, li: (0, 0))
    return pl.pallas_call(
        functools.partial(_conv_kernel, taps=taps, tl=tl),
        grid=(bsz, seq // tl),
        in_specs=[pl.BlockSpec((1, tl, c), lambda bi, li: (bi, li, 0)), full(w), full(b), full(lg), full(lb)],
        out_specs=pl.BlockSpec((1, tl, c), lambda bi, li: (bi, li, 0)),
        out_shape=jax.ShapeDtypeStruct((bsz, seq, c), BF16),
        scratch_shapes=[pltpu.VMEM((tl + CONV_HALO, c), F32)],
        compiler_params=_params(("parallel", "arbitrary")),
        name="dwconv",
    )(glu, w, b, lg, lb)


def _lower_tri(n, dtype, strict=False):
    r = lax.broadcasted_iota(jnp.int32, (n, n), 0)
    c = lax.broadcasted_iota(jnp.int32, (n, n), 1)
    keep = (c < r) if strict else (c <= r)
    return jnp.where(keep, 1.0, 0.0).astype(dtype)


def _cum_kernel(f_ref, b_ref, o_ref, carry_ref, *, tl):
    li = pl.program_id(1)

    @pl.when(li == 0)
    def _():
        carry_ref[...] = jnp.zeros_like(carry_ref)

    x = f_ref[0] + b_ref[...]
    lf = jnp.minimum(x, 0.0) - jnp.log1p(jnp.exp(-jnp.abs(x)))
    cum = _select_dot(_lower_tri(tl, BF16), lf) + carry_ref[...]
    carry_ref[...] = cum[tl - 1:tl, :]
    o_ref[0] = cum.T[0:8, :]


def _cum(fs, fb):
    bsz, seq, _ = fs.shape
    tl = CUM_TILE
    return pl.pallas_call(
        functools.partial(_cum_kernel, tl=tl),
        grid=(bsz, seq // tl),
        in_specs=[pl.BlockSpec((1, tl, LANES), lambda bi, li: (bi, li, 0)),
                  pl.BlockSpec((1, LANES), lambda bi, li: (0, 0))],
        out_specs=pl.BlockSpec((1, 8, tl), lambda bi, li: (bi, 0, li)),
        out_shape=jax.ShapeDtypeStruct((bsz, 8, seq), F32),
        scratch_shapes=[pltpu.VMEM((1, LANES), F32)],
        compiler_params=_params(("parallel", "arbitrary")),
        name="fgate_cumsum",
    )(fs, fb)


def _attn_kernel(q_ref, qn_ref, k_ref, v_ref, c_ref, o_ref, qm_ref, s_ref, m_ref, l_ref, acc_ref, *, tq, nq):
    hp = pl.program_id(1)
    qi = pl.program_id(2)
    slot = qi % 2
    nslot = 1 - slot
    lane = lax.broadcasted_iota(jnp.int32, (tq, LANES), 1)
    row = lax.broadcasted_iota(jnp.int32, (tq, tq), 0)
    col = lax.broadcasted_iota(jnp.int32, (tq, tq), 1)
    tiles = tq // LANES

    def start_tile(sl, q):
        for hh in range(2):
            in_head = (lane >= HEAD_DIM) if hh else (lane < HEAD_DIM)
            qm_ref[hh] = jnp.where(in_head, q, jnp.zeros_like(q))
            m_ref[sl, hh] = jnp.full((tq, LANES), NEG_INF, F32)

    def scores(sl, j, masked):
        k0 = pl.multiple_of(j * tq, tq)
        kb = k_ref[0, pl.ds(k0, tq), :]
        for hh in range(2):
            cb = c_ref[0, pl.ds(hp * 2 + hh, 1), pl.ds(k0, tq)]
            s = lax.dot_general(qm_ref[hh], kb, (((1,), (1,)), ((), ())), preferred_element_type=F32) - cb
            if masked:
                s = jnp.where(col <= row, s, NEG_INF)
            s_ref[sl, hh, j] = s
            mt = m_ref[sl, hh]
            for t in range(tiles):
                mt = jnp.maximum(mt, s[:, t * LANES:(t + 1) * LANES])
            m_ref[sl, hh] = mt

    def finish_tile(sl, diag):
        scores(sl, diag, True)
        for hh in range(2):
            m_ref[sl, hh] = jnp.broadcast_to(jnp.max(m_ref[sl, hh], axis=-1, keepdims=True), (tq, LANES))

    def weighted(j):
        k0 = pl.multiple_of(j * tq, tq)
        vb = v_ref[0, pl.ds(k0, tq), :]
        for hh in range(2):
            mb = m_ref[slot, hh]
            lt = l_ref[hh]
            ps = []
            for t in range(tiles):
                p = jnp.exp(s_ref[slot, hh, j, :, t * LANES:(t + 1) * LANES] - mb)
                lt = lt + p
                ps.append(p.astype(BF16))
            l_ref[hh] = lt
            acc_ref[hh] += jnp.dot(jnp.concatenate(ps, axis=1), vb, preferred_element_type=F32)

    @pl.when(qi == 0)
    def _():
        start_tile(0, q_ref[0])
        finish_tile(0, 0)

    for hh in range(2):
        l_ref[hh] = jnp.zeros((tq, LANES), F32)
        acc_ref[hh] = jnp.zeros((tq, LANES), F32)

    @pl.when(qi + 1 < nq)
    def _():
        start_tile(nslot, qn_ref[0])

        def both(j, carry):
            weighted(j)
            scores(nslot, j, False)
            return carry

        lax.fori_loop(0, qi + 1, both, 0)
        finish_tile(nslot, qi + 1)

    @pl.when(qi + 1 == nq)
    def _():
        def last(j, carry):
            weighted(j)
            return carry

        lax.fori_loop(0, qi + 1, last, 0)

    outs = [acc_ref[hh] / jnp.sum(l_ref[hh], axis=-1, keepdims=True) for hh in range(2)]
    o_ref[0] = jnp.where(lane < HEAD_DIM, outs[0], outs[1]).astype(BF16)


def _attention(q, k, v, cum):
    bsz, seq, w = q.shape
    tq = ATTN_TQ
    pairs = w // LANES
    nq = seq // tq
    return pl.pallas_call(
        functools.partial(_attn_kernel, tq=tq, nq=nq),
        grid=(bsz, pairs, nq),
        in_specs=[pl.BlockSpec((1, tq, LANES), lambda b, h, i: (b, i, h)),
                  pl.BlockSpec((1, tq, LANES), lambda b, h, i: (b, jnp.minimum(i + 1, nq - 1), h)),
                  pl.BlockSpec((1, seq, LANES), lambda b, h, i: (b, 0, h)),
                  pl.BlockSpec((1, seq, LANES), lambda b, h, i: (b, 0, h)),
                  pl.BlockSpec((1, 8, seq), lambda b, h, i: (b, 0, 0))],
        out_specs=pl.BlockSpec((1, tq, LANES), lambda b, h, i: (b, i, h)),
        out_shape=jax.ShapeDtypeStruct((bsz, seq, w), BF16),
        scratch_shapes=[pltpu.VMEM((2, tq, LANES), BF16), pltpu.VMEM((2, 2, nq, tq, tq), F32),
                        pltpu.VMEM((2, 2, tq, LANES), F32), pltpu.VMEM((2, tq, LANES), F32),
                        pltpu.VMEM((2, tq, LANES), F32)],
        compiler_params=_params(("parallel", "parallel", "arbitrary")),
        name="fox_attention",
    )(q, q, k, v, cum)


def _ssd_kernel(xbc_ref, z_ref, dt_ref, cw_ref, cb_ref, dtb_ref, alog_ref, dw_ref, ng_ref, ex_ref, o_ref,
                buf_ref, h_ref, *, T, taps):
    ci = pl.program_id(1)
    gw = SSD_W // SSD_GROUPS

    @pl.when(ci == 0)
    def _():
        buf_ref[0:SSD_HALO, :] = jnp.zeros((SSD_HALO, SSD_CONV_CH), F32)
        h_ref[...] = jnp.zeros_like(h_ref)

    @pl.when(ci > 0)
    def _():
        buf_ref[0:SSD_HALO, :] = buf_ref[T:T + SSD_HALO, :]

    buf_ref[SSD_HALO:SSD_HALO + T, :] = xbc_ref[0]
    first = SSD_HALO - (taps - 1)
    acc = jnp.broadcast_to(cb_ref[...], (T, SSD_CONV_CH))
    for k in range(taps):
        acc = acc + _rows8(cw_ref, k, T) * buf_ref[first + k:first + k + T, :]
    xc = acc * _sigmoid(acc)
    xs = xc[:, 0:SSD_W]
    b_mat = xc[:, SSD_W:SSD_W + LANES]
    c_mat = xc[:, SSD_W + LANES:SSD_W + 2 * LANES]

    dt = _softplus(dt_ref[0] + dtb_ref[...])
    a = dt * (-jnp.exp(alog_ref[...]))
    acum = _select_dot(_lower_tri(T, BF16), a)
    acum_row = acum.T
    expand = ex_ref[...]
    dt_w = _dot_select(dt, expand)
    acum_w = _dot_select(acum, expand)
    last_w = acum_w[T - 1:T, :]
    xdt = xs * dt_w
    xdt_b = xdt.astype(BF16)
    xdec_b = (xdt * jnp.exp(last_w - acum_w)).astype(BF16)
    eacum_w = jnp.exp(acum_w)
    chunk_decay = jnp.exp(last_w)
    bb = b_mat.astype(BF16)
    cc = c_mat.astype(BF16)
    bt = b_mat.T.astype(BF16)

    row = lax.broadcasted_iota(jnp.int32, (T, T), 0)
    col = lax.broadcasted_iota(jnp.int32, (T, T), 1)
    causal = col <= row
    lane = lax.broadcasted_iota(jnp.int32, (T, LANES), 1)
    h_in = h_ref[...]
    h_ref[...] = h_in * chunk_decay + jnp.dot(bt, xdec_b, preferred_element_type=F32)
    h_in_b = h_in.astype(BF16)
    pieces = []
    for g in range(SSD_GROUPS):
        in_group = (lane >= g * SSD_STATE) & (lane < (g + 1) * SSD_STATE)
        cg = jnp.where(in_group, cc, jnp.zeros_like(cc))
        cbm = lax.dot_general(cg, bb, (((1,), (1,)), ((), ())), preferred_element_type=F32)
        yoff = jnp.dot(cg, h_in_b[:, g * gw:(g + 1) * gw], preferred_element_type=F32)
        for pr in range(gw // LANES):
            lo = g * gw + pr * LANES
            xpair = xdt_b[:, lo:lo + LANES]
            res = []
            for hh in range(2):
                head = lo // HEAD_DIM + hh
                seg = acum[:, head:head + 1] - acum_row[head:head + 1, :]
                lmat = jnp.exp(jnp.where(causal, seg, NEG_INF))
                res.append(jnp.dot((cbm * lmat).astype(BF16), xpair, preferred_element_type=F32))
            ydiag = jnp.where(lane < HEAD_DIM, res[0], res[1])
            pieces.append(ydiag + yoff[:, pr * LANES:(pr + 1) * LANES] * eacum_w[:, lo:lo + LANES])
    y = jnp.concatenate(pieces, axis=1) + xs * dw_ref[...]
    zz = z_ref[0]
    gated = y * (zz * _sigmoid(zz))
    ms = jnp.mean(gated * gated, axis=-1, keepdims=True)
    o_ref[0] = (gated * lax.rsqrt(ms + EPS) * ng_ref[...]).astype(BF16)


def _ssd(xbc, z, ds, cw, cb, dtb, alog, dwide, ng, expand):
    bsz, seq, _ = xbc.shape
    T = SSD_T
    taps = cw.shape[0]
    cw = jnp.repeat(cw, SUBLANES, axis=0)
    full = lambda a: pl.BlockSpec(a.shape, lambda bi, ci: (0, 0))
    tok = lambda w: pl.BlockSpec((1, T, w), lambda bi, ci: (bi, ci, 0))
    return pl.pallas_call(
        functools.partial(_ssd_kernel, T=T, taps=taps),
        grid=(bsz, seq // T),
        in_specs=[tok(SSD_CONV_CH), tok(SSD_W), tok(LANES), full(cw), full(cb), full(dtb), full(alog),
                  full(dwide), full(ng), full(expand)],
        out_specs=tok(SSD_W),
        out_shape=jax.ShapeDtypeStruct((bsz, seq, SSD_W), BF16),
        scratch_shapes=[pltpu.VMEM((T + SSD_HALO, SSD_CONV_CH), F32),
                        pltpu.VMEM((SSD_GROUPS * SSD_STATE, SSD_W), F32)],
        compiler_params=_params(("parallel", "arbitrary")),
        name="ssd_mixer",
    )(xbc, z, ds, cw, cb, dtb, alog, dwide, ng, expand)


def _first_max(vals):
    best = vals[0]
    for v in vals[1:]:
        best = jnp.maximum(best, v)
    idx = jnp.full(best.shape, float(len(vals) - 1), F32)
    for j in range(len(vals) - 2, -1, -1):
        idx = jnp.where(vals[j] == best, float(j), idx)
    return best, idx


def _outproj_kernel(yc_ref, ya_ref, ys_ref, x_ref, wo_ref, g_ref, wr_ref, br_ref,
                    x1_ref, hf_ref, rrow_ref, rcol_ref, *, tm):
    y = jnp.dot(yc_ref[...], wo_ref[0:CONV_CH, :], preferred_element_type=F32)
    y = y + jnp.dot(ya_ref[...], wo_ref[CONV_CH:CONV_CH + ATTN_W, :], preferred_element_type=F32)
    y = y + jnp.dot(ys_ref[...], wo_ref[CONV_CH + ATTN_W:, :], preferred_element_type=F32)
    x1 = x_ref[...] + y
    x1_ref[...] = x1
    ms = jnp.mean(x1 * x1, axis=-1, keepdims=True)
    hf = x1 * lax.rsqrt(ms + EPS) * g_ref[...]
    hf_hi = hf.astype(BF16)
    hf_ref[...] = hf_hi
    hf_lo = (hf - hf_hi.astype(F32)).astype(BF16)
    logits = (jnp.dot(hf_hi, wr_ref[0], preferred_element_type=F32)
              + jnp.dot(hf_hi, wr_ref[1], preferred_element_type=F32)
              + jnp.dot(hf_lo, wr_ref[0], preferred_element_type=F32)) + br_ref[...]
    lt = logits.T
    gl = [lt[j:j + 1, :] for j in range(MOE_GROUPS)]
    gmax, gidx = _first_max(gl)
    denom = gl[0] * 0.0
    for v in gl:
        denom = denom + jnp.exp(v - gmax)
    gval = 1.0 / denom
    esel = []
    for j in range(EXPERTS_PER_GROUP):
        erow = lambda g: lt[MOE_GROUPS + g * EXPERTS_PER_GROUP + j:MOE_GROUPS + g * EXPERTS_PER_GROUP + j + 1, :]
        v = erow(MOE_GROUPS - 1)
        for g in range(MOE_GROUPS - 2, -1, -1):
            v = jnp.where(gidx == float(g), erow(g), v)
        esel.append(v)
    v1, i1 = _first_max(esel)
    rest = [jnp.where(i1 == float(j), NEG_INF, esel[j]) for j in range(EXPERTS_PER_GROUP)]
    v2, i2 = _first_max(rest)
    e2 = jnp.exp(v2 - v1)
    w1 = (1.0 / (1.0 + e2)) * gval
    w2 = (e2 / (1.0 + e2)) * gval
    cw = [jnp.where(i1 == float(j), w1, jnp.where(i2 == float(j), w2, 0.0)) for j in range(EXPERTS_PER_GROUP)]
    hi = [c.astype(BF16).astype(F32) for c in cw]
    mid = [(c - h).astype(BF16).astype(F32) for c, h in zip(cw, hi)]
    lo = [(c - h - m).astype(BF16).astype(F32) for c, h, m in zip(cw, hi, mid)]
    rows = [gidx] + hi + mid + lo
    ri = lax.broadcasted_iota(jnp.int32, (LANES, tm), 0)
    packed = jnp.zeros((LANES, tm), F32)
    for j, r in enumerate(rows):
        packed = jnp.where(ri == j, r, packed)
    rrow_ref[...] = packed[0:8, :]
    rcol_ref[...] = packed.T


def _outproj(yc, ya, ys, x, wo, g, wr, br):
    n, d = x.shape
    tm = ROW_TILE
    row = lambda w: pl.BlockSpec((tm, w), lambda i: (i, 0))
    full = lambda a: pl.BlockSpec(a.shape, lambda i: (0, 0))
    return pl.pallas_call(
        functools.partial(_outproj_kernel, tm=tm),
        grid=(n // tm,),
        in_specs=[row(CONV_CH), row(ATTN_W), row(SSD_W), row(d), full(wo), full(g),
                  pl.BlockSpec(wr.shape, lambda i: (0, 0, 0)), full(br)],
        out_specs=[row(d), row(d), pl.BlockSpec((8, tm), lambda i: (0, i)), row(LANES)],
        out_shape=[jax.ShapeDtypeStruct((n, d), F32), jax.ShapeDtypeStruct((n, d), BF16),
                   jax.ShapeDtypeStruct((8, n), F32), jax.ShapeDtypeStruct((n, LANES), F32)],
        compiler_params=_params(("parallel",)),
        name="outproj_router",
    )(yc, ya, ys, x, wo, g, wr, br)


def _moe_kernel(cnt_ref, x1_ref, hf_ref, rrow_ref, rcol_ref, wg_ref, wu_ref, wd_ref, fg_ref, o_ref,
                xg_ref, acc_ref, cw_ref, rankr_ref, *, ts, spans, rows, cap, final_norm):
    i = pl.program_id(0)
    g = pl.program_id(1)
    e = pl.program_id(2)
    gf = g.astype(F32)

    def tok(sb):
        return slice(sb * ts, (sb + 1) * ts)

    def chunk(sb, c, r):
        if isinstance(c, int):
            return slice(sb * cap + c * r, sb * cap + (c + 1) * r)
        return pl.ds(pl.multiple_of(sb * cap + c * r, 16), r)

    def group_rank(sb):
        sub8 = lax.broadcasted_iota(jnp.int32, (8, ts), 0)
        return jnp.sum(jnp.where(sub8 == g, rankr_ref[:, tok(sb)], 0.0), axis=0, keepdims=True)

    def placement(sb, c, r):
        first = float(c * r) if isinstance(c, int) else (c * r).astype(F32)
        slot = lax.broadcasted_iota(jnp.int32, (r, ts), 0).astype(F32) + first
        return jnp.where((group_rank(sb) == slot) & (rrow_ref[0:1, tok(sb)] == gf), 1.0, 0.0).astype(BF16)

    def gather(sb, c, r):
        p = placement(sb, c, r)
        xg_ref[chunk(sb, c, r), :] = jnp.dot(p, hf_ref[tok(sb), :], preferred_element_type=F32).astype(BF16)
        cw_ref[chunk(sb, c, r), :] = jnp.dot(p, rcol_ref[tok(sb), :].astype(BF16), preferred_element_type=F32)
        acc_ref[chunk(sb, c, r), :] = jnp.zeros((r, acc_ref.shape[1]), F32)

    def expert(sb, c, r):
        lane_r = lax.broadcasted_iota(jnp.int32, (r, LANES), 1)
        pick = ((lane_r == 1 + e) | (lane_r == 1 + EXPERTS_PER_GROUP + e)
                | (lane_r == 1 + 2 * EXPERTS_PER_GROUP + e))
        xc = xg_ref[chunk(sb, c, r), :]
        hg = jnp.dot(xc, wg_ref[0], preferred_element_type=F32)
        hu = jnp.dot(xc, wu_ref[0], preferred_element_type=F32)
        hh = (hg * _sigmoid(hg) * hu).astype(BF16)
        y = jnp.dot(hh, wd_ref[0], preferred_element_type=F32)
        cwe = jnp.sum(jnp.where(pick, cw_ref[chunk(sb, c, r), :], 0.0), axis=-1, keepdims=True)
        acc_ref[chunk(sb, c, r), :] += cwe * y

    def scatter(sb, c, r):
        tn = (((0,), (0,)), ((), ()))
        o_ref[tok(sb), :] += lax.dot_general(placement(sb, c, r), acc_ref[chunk(sb, c, r), :].astype(BF16), tn,
                                             preferred_element_type=F32)

    def step(sb, c, r):
        @pl.when(e == 0)
        def _():
            gather(sb, c, r)

        expert(sb, c, r)

        @pl.when(e == EXPERTS_PER_GROUP - 1)
        def _():
            scatter(sb, c, r)

    @pl.when((g == 0) & (e == 0))
    def _():
        o_ref[...] = x1_ref[...]
        sub = lax.broadcasted_iota(jnp.int32, (8, ts), 0).astype(F32)
        for c0 in range(0, ts, RANK_STRIP):
            r_ = lax.broadcasted_iota(jnp.int32, (ts, RANK_STRIP), 0)
            c_ = lax.broadcasted_iota(jnp.int32, (ts, RANK_STRIP), 1) + c0
            before = jnp.where(r_ < c_, 1.0, 0.0).astype(BF16)
            for sb in range(spans):
                m_row = jnp.where(rrow_ref[0:1, tok(sb)] == sub, 1.0, 0.0).astype(BF16)
                rankr_ref[:, sb * ts + c0:sb * ts + c0 + RANK_STRIP] = jnp.dot(
                    m_row, before, preferred_element_type=F32)

    for sb in range(spans):
        cnt = cnt_ref[(i * spans + sb) * MOE_GROUPS + g]
        lo = 0
        for r in rows[:-1]:
            @pl.when((cnt > lo) & (cnt <= r))
            def _(sb=sb, r=r):
                step(sb, 0, r)
            lo = r

        @pl.when(cnt > lo)
        def _(sb=sb, cnt=cnt):
            r = rows[-1]

            def body(c, carry):
                step(sb, c, r)
                return carry

            lax.fori_loop(0, (cnt + r - 1) // r, body, 0)

    if final_norm:
        @pl.when((g == MOE_GROUPS - 1) & (e == EXPERTS_PER_GROUP - 1))
        def _():
            xo = o_ref[...]
            ms = jnp.mean(xo * xo, axis=-1, keepdims=True)
            o_ref[...] = xo * lax.rsqrt(ms + EPS) * fg_ref[...]


def _moe(counts, x1, hf, rrow, rcol, wg, wu, wd, fg, final_norm):
    n, d = x1.shape
    tb, ts, rows = MOE_BLOCK, MOE_SORT, MOE_ROWS
    spans = tb // ts
    de = wg.shape[2]
    cap = -(-ts // rows[-1]) * rows[-1]
    once = pl.Buffered(1)
    tokens = lambda w, **kw: pl.BlockSpec((tb, w), lambda i, g, e, c: (i, 0), **kw)
    expert = lambda shape: pl.BlockSpec(shape, lambda i, g, e, c: (g * EXPERTS_PER_GROUP + e, 0, 0))
    grid_spec = pltpu.PrefetchScalarGridSpec(
        num_scalar_prefetch=1,
        grid=(n // tb, MOE_GROUPS, EXPERTS_PER_GROUP),
        in_specs=[tokens(d, pipeline_mode=once), tokens(d, pipeline_mode=once),
                  pl.BlockSpec((8, tb), lambda i, g, e, c: (0, i)), tokens(LANES),
                  expert((1, d, de)), expert((1, d, de)), expert((1, de, d)),
                  pl.BlockSpec((1, d), lambda i, g, e, c: (0, 0))],
        out_specs=tokens(d),
        scratch_shapes=[pltpu.VMEM((spans * cap, d), BF16), pltpu.VMEM((spans * cap, d), F32),
                        pltpu.VMEM((spans * cap, LANES), F32), pltpu.VMEM((8, tb), F32)],
    )
    return pl.pallas_call(
        functools.partial(_moe_kernel, ts=ts, spans=spans, rows=rows, cap=cap, final_norm=final_norm),
        grid_spec=grid_spec,
        out_shape=jax.ShapeDtypeStruct((n, d), F32),
        compiler_params=_params(("parallel", "arbitrary", "arbitrary")),
        name="moe",
    )(counts, x1, hf, rrow, rcol, wg, wu, wd, fg)


def _pad_lanes(v, width=LANES):
    v = v.reshape(1, -1).astype(F32)
    return jnp.pad(v, ((0, 0), (0, width - v.shape[1])))


def kernel(x, norm_mix, w_in, conv_dw_w, conv_dw_b, conv_ln_g, conv_ln_b, fgate_b, ssd_conv_w, ssd_conv_b,
           ssd_dt_bias, ssd_A_log, ssd_D, ssd_norm_g, w_out, norm_ffn, router_group_w, router_group_b,
           router_expert_w, router_expert_b, expert_w_gate, expert_w_up, expert_w_down, norm_final):
    bsz, seq, d = x.shape
    n = bsz * seq
    depth = w_in.shape[0]
    heads = fgate_b.shape[1]
    ssd_heads = ssd_A_log.shape[1]
    n_exp = expert_w_gate.shape[1]
    assert heads * HEAD_DIM == ATTN_W and ssd_heads * HEAD_DIM == SSD_W
    assert n_exp == MOE_GROUPS * EXPERTS_PER_GROUP and conv_dw_w.shape[1] <= CONV_HALO + 1
    assert n % ROW_TILE == 0 and n % MOE_BLOCK == 0 and seq % max(CONV_TILE, CUM_TILE, ATTN_TQ, SSD_T) == 0

    sizes = (2 * CONV_CH, ATTN_W, ATTN_W, ATTN_W, heads, SSD_W, SSD_W, SSD_GROUPS * SSD_STATE,
             SSD_GROUPS * SSD_STATE, ssd_heads)
    offs = [0]
    for s in sizes:
        offs.append(offs[-1] + s)
    expand = jnp.repeat(jnp.eye(LANES, SSD_W // HEAD_DIM, dtype=BF16), HEAD_DIM, axis=1)

    xr = x.reshape(n, d)
    for l in range(depth):
        w = w_in[l]
        wa = jnp.concatenate([w[:, offs[0]:offs[4]], w[:, offs[5]:offs[9]]], axis=1).astype(BF16)
        zpad = jnp.zeros((d, LANES - heads), F32)
        ws = jnp.concatenate([w[:, offs[4]:offs[5]], zpad, w[:, offs[9]:offs[10]], zpad], axis=1).astype(BF16)
        glu, q, k, v, z, xbc, fs, ds = _inproj(xr, norm_mix[l].reshape(1, d), wa, ws)

        seq3 = lambda a: a.reshape(bsz, seq, a.shape[-1])
        y_conv = _conv(seq3(glu), conv_dw_w[l], conv_dw_b[l].reshape(1, -1), conv_ln_g[l].reshape(1, -1),
                       conv_ln_b[l].reshape(1, -1))
        cum = _cum(seq3(fs), _pad_lanes(fgate_b[l]))
        y_attn = _attention(seq3(q), seq3(k), seq3(v), cum)
        y_ssd = _ssd(seq3(xbc), seq3(z), seq3(ds), ssd_conv_w[l], ssd_conv_b[l].reshape(1, -1),
                     _pad_lanes(ssd_dt_bias[l]), _pad_lanes(ssd_A_log[l]),
                     jnp.repeat(ssd_D[l].astype(F32), HEAD_DIM).reshape(1, -1), ssd_norm_g[l].reshape(1, -1),
                     expand)

        wr = jnp.concatenate([router_group_w[l], router_expert_w[l],
                              jnp.zeros((d, LANES - MOE_GROUPS - n_exp), F32)], axis=1)
        wr_hi = wr.astype(BF16)
        wr = jnp.stack([wr_hi, (wr - wr_hi.astype(F32)).astype(BF16)])
        br = _pad_lanes(jnp.concatenate([router_group_b[l], router_expert_b[l]]))
        x1, hf, rrow, rcol = _outproj(y_conv.reshape(n, -1), y_attn.reshape(n, -1), y_ssd.reshape(n, -1), xr,
                                      w_out[l].astype(BF16), norm_ffn[l].reshape(1, d), wr, br)

        gidx = rrow[0].astype(jnp.int32).reshape(n // MOE_SORT, MOE_SORT)
        counts = jnp.sum(gidx[:, :, None] == jnp.arange(MOE_GROUPS, dtype=jnp.int32), axis=1,
                         dtype=jnp.int32).reshape(-1)
        xr = _moe(counts, x1, hf, rrow, rcol, expert_w_gate[l].astype(BF16), expert_w_up[l].astype(BF16),
                  expert_w_down[l].astype(BF16), norm_final.reshape(1, d), final_norm=(l == depth - 1))
    return xr.reshape(bsz, seq, d)
```

```python
import functools

import jax
import jax.numpy as jnp
from jax import lax
from jax.experimental import pallas as pl
from jax.experimental.pallas import tpu as pltpu

F32 = jnp.float32
BF16 = jnp.bfloat16
EPS = 1e-6
NEG_INF = float("-inf")

HEAD_DIM = 64
LANES = 128
CONV_CH = 512
ATTN_W = 512
SSD_W = 512
SSD_GROUPS = 2
SSD_STATE = 64
SSD_CONV_CH = SSD_W + 2 * SSD_GROUPS * SSD_STATE
MOE_GROUPS = 4
EXPERTS_PER_GROUP = 4

ROW_TILE = 512
CONV_TILE = 256
CONV_ROWS = 32
CONV_HALO = 32
SUBLANES = 8
CUM_TILE = 512
ATTN_TQ = 512
SSD_T = 256
SSD_HALO = 8
MOE_BLOCK = 2048
MOE_SORT = 1024
MOE_ROWS = (256, 288, 320, 384)
RANK_STRIP = 256
VMEM_LIMIT = 60 * 1024 * 1024


def _params(sem):
    return pltpu.CompilerParams(dimension_semantics=sem, vmem_limit_bytes=VMEM_LIMIT)


def _sigmoid(x):
    return jax.nn.sigmoid(x)


def _softplus(x):
    return jnp.maximum(x, 0.0) + jnp.log1p(jnp.exp(-jnp.abs(x)))


def _split3(x):
    hi = x.astype(BF16)
    r = x - hi.astype(F32)
    mid = r.astype(BF16)
    lo = (r - mid.astype(F32)).astype(BF16)
    return hi, mid, lo


def _select_dot(sel, x):
    hi, mid, lo = _split3(x)
    d = lambda p: jnp.dot(sel, p, preferred_element_type=F32)
    return d(hi) + d(mid) + d(lo)


def _dot_select(x, sel):
    hi, mid, lo = _split3(x)
    d = lambda p: jnp.dot(p, sel, preferred_element_type=F32)
    return d(hi) + d(mid) + d(lo)


def _rows8(w_ref, k, rows):
    return jnp.concatenate([w_ref[k * SUBLANES:(k + 1) * SUBLANES, :]] * (rows // SUBLANES), axis=0)


def _inproj_kernel(x_ref, g_ref, wa_ref, ws_ref, glu_ref, q_ref, k_ref, v_ref, z_ref, xbc_ref,
                   fs_ref, ds_ref):
    xf = x_ref[...]
    ms = jnp.mean(xf * xf, axis=-1, keepdims=True)
    h = (xf * lax.rsqrt(ms + EPS) * g_ref[...]).astype(BF16)

    def mm(lo, hi):
        return jnp.dot(h, wa_ref[:, lo:hi], preferred_element_type=F32)

    glu_ref[...] = mm(0, 512) * _sigmoid(mm(512, 1024))
    q_ref[...] = (mm(1024, 1536) * (HEAD_DIM ** -0.5)).astype(BF16)
    k_ref[...] = mm(1536, 2048).astype(BF16)
    v_ref[...] = mm(2048, 2560).astype(BF16)
    z_ref[...] = mm(2560, 3072)
    xbc_ref[...] = mm(3072, 3840)
    small = jnp.dot(h, ws_ref[...], preferred_element_type=F32)
    fs_ref[...] = small[:, 0:LANES]
    ds_ref[...] = small[:, LANES:2 * LANES]


def _inproj(x, g, wa, ws):
    n, d = x.shape
    tm = ROW_TILE
    row = lambda w: pl.BlockSpec((tm, w), lambda i: (i, 0))
    full = lambda a: pl.BlockSpec(a.shape, lambda i: (0, 0))
    shapes = [(512, F32), (512, BF16), (512, BF16), (512, BF16), (512, F32), (SSD_CONV_CH, F32),
              (LANES, F32), (LANES, F32)]
    return pl.pallas_call(
        _inproj_kernel,
        grid=(n // tm,),
        in_specs=[row(d), full(g), full(wa), full(ws)],
        out_specs=[row(w) for w, _ in shapes],
        out_shape=[jax.ShapeDtypeStruct((n, w), dt) for w, dt in shapes],
        compiler_params=_params(("parallel",)),
        name="inproj",
    )(x, g, wa, ws)


def _conv_kernel(x_ref, w_ref, b_ref, lg_ref, lb_ref, o_ref, buf_ref, *, taps, tl):
    li = pl.program_id(1)

    @pl.when(li == 0)
    def _():
        buf_ref[0:CONV_HALO, :] = jnp.zeros((CONV_HALO, CONV_CH), F32)

    @pl.when(li > 0)
    def _():
        buf_ref[0:CONV_HALO, :] = buf_ref[tl:tl + CONV_HALO, :]

    buf_ref[CONV_HALO:CONV_HALO + tl, :] = x_ref[0]
    first = CONV_HALO - (taps - 1)
    by_shift = [[k for k in range(taps) if (first + k) % SUBLANES == s] for s in range(SUBLANES)]
    for r in range(0, tl, CONV_ROWS):
        acc = jnp.broadcast_to(b_ref[...], (CONV_ROWS, CONV_CH))
        for s, ks in enumerate(by_shift):
            if not ks:
                continue
            span = CONV_ROWS + (SUBLANES if s else 0)
            z = None
            for k in ks:
                base = r + first + k - s
                term = _rows8(w_ref, k, span) * buf_ref[base:base + span, :]
                z = term if z is None else z + term
            acc = acc + z[s:s + CONV_ROWS, :]
        mu = jnp.mean(acc, axis=-1, keepdims=True)
        cen = acc - mu
        var = jnp.mean(cen * cen, axis=-1, keepdims=True)
        y = cen * lax.rsqrt(var + EPS) * lg_ref[...] + lb_ref[...]
        o_ref[0, r:r + CONV_ROWS, :] = (y * _sigmoid(y)).astype(BF16)


def _conv(glu, w, b, lg, lb):
    bsz, seq, c = glu.shape
    tl = CONV_TILE
    taps = w.shape[0]
    w = jnp.repeat(w, SUBLANES, axis=0)
    full = lambda a: pl.BlockSpec(a.shape, lambda bi, li: (0, 0))
    return pl.pallas_call(
        functools.partial(_conv_kernel, taps=taps, tl=tl),
        grid=(bsz, seq // tl),
        in_specs=[pl.BlockSpec((1, tl, c), lambda bi, li: (bi, li, 0)), full(w), full(b), full(lg), full(lb)],
        out_specs=pl.BlockSpec((1, tl, c), lambda bi, li: (bi, li, 0)),
        out_shape=jax.ShapeDtypeStruct((bsz, seq, c), BF16),
        scratch_shapes=[pltpu.VMEM((tl + CONV_HALO, c), F32)],
        compiler_params=_params(("parallel", "arbitrary")),
        name="dwconv",
    )(glu, w, b, lg, lb)


def _lower_tri(n, dtype, strict=False):
    r = lax.broadcasted_iota(jnp.int32, (n, n), 0)
    c = lax.broadcasted_iota(jnp.int32, (n, n), 1)
    keep = (c < r) if strict else (c <= r)
    return jnp.where(keep, 1.0, 0.0).astype(dtype)


def _cum_kernel(f_ref, b_ref, o_ref, carry_ref, *, tl):
    li = pl.program_id(1)

    @pl.when(li == 0)
    def _():
        carry_ref[...] = jnp.zeros_like(carry_ref)

    x = f_ref[0] + b_ref[...]
    lf = jnp.minimum(x, 0.0) - jnp.log1p(jnp.exp(-jnp.abs(x)))
    cum = _select_dot(_lower_tri(tl, BF16), lf) + carry_ref[...]
    carry_ref[...] = cum[tl - 1:tl, :]
    o_ref[0] = cum.T[0:8, :]


def _cum(fs, fb):
    bsz, seq, _ = fs.shape
    tl = CUM_TILE
    return pl.pallas_call(
        functools.partial(_cum_kernel, tl=tl),
        grid=(bsz, seq // tl),
        in_specs=[pl.BlockSpec((1, tl, LANES), lambda bi, li: (bi, li, 0)),
                  pl.BlockSpec((1, LANES), lambda bi, li: (0, 0))],
        out_specs=pl.BlockSpec((1, 8, tl), lambda bi, li: (bi, 0, li)),
        out_shape=jax.ShapeDtypeStruct((bsz, 8, seq), F32),
        scratch_shapes=[pltpu.VMEM((1, LANES), F32)],
        compiler_params=_params(("parallel", "arbitrary")),
        name="fgate_cumsum",
    )(fs, fb)


def _attn_kernel(q_ref, qn_ref, k_ref, v_ref, c_ref, o_ref, qm_ref, s_ref, m_ref, l_ref, acc_ref, *, tq, nq):
    hp = pl.program_id(1)
    qi = pl.program_id(2)
    slot = qi % 2
    nslot = 1 - slot
    lane = lax.broadcasted_iota(jnp.int32, (tq, LANES), 1)
    row = lax.broadcasted_iota(jnp.int32, (tq, tq), 0)
    col = lax.broadcasted_iota(jnp.int32, (tq, tq), 1)
    tiles = tq // LANES

    def start_tile(sl, q):
        for hh in range(2):
            in_head = (lane >= HEAD_DIM) if hh else (lane < HEAD_DIM)
            qm_ref[hh] = jnp.where(in_head, q, jnp.zeros_like(q))
            m_ref[sl, hh] = jnp.full((tq, LANES), NEG_INF, F32)

    def scores(sl, j, masked):
        k0 = pl.multiple_of(j * tq, tq)
        kb = k_ref[0, pl.ds(k0, tq), :]
        for hh in range(2):
            cb = c_ref[0, pl.ds(hp * 2 + hh, 1), pl.ds(k0, tq)]
            s = lax.dot_general(qm_ref[hh], kb, (((1,), (1,)), ((), ())), preferred_element_type=F32) - cb
            if masked:
                s = jnp.where(col <= row, s, NEG_INF)
            s_ref[sl, hh, j] = s
            mt = m_ref[sl, hh]
            for t in range(tiles):
                mt = jnp.maximum(mt, s[:, t * LANES:(t + 1) * LANES])
            m_ref[sl, hh] = mt

    def finish_tile(sl, diag):
        scores(sl, diag, True)
        for hh in range(2):
            m_ref[sl, hh] = jnp.broadcast_to(jnp.max(m_ref[sl, hh], axis=-1, keepdims=True), (tq, LANES))

    def weighted(j):
        k0 = pl.multiple_of(j * tq, tq)
        vb = v_ref[0, pl.ds(k0, tq), :]
        for hh in range(2):
            mb = m_ref[slot, hh]
            lt = l_ref[hh]
            ps = []
            for t in range(tiles):
                p = jnp.exp(s_ref[slot, hh, j, :, t * LANES:(t + 1) * LANES] - mb)
                lt = lt + p
                ps.append(p.astype(BF16))
            l_ref[hh] = lt
            acc_ref[hh] += jnp.dot(jnp.concatenate(ps, axis=1), vb, preferred_element_type=F32)

    @pl.when(qi == 0)
    def _():
        start_tile(0, q_ref[0])
        finish_tile(0, 0)

    for hh in range(2):
        l_ref[hh] = jnp.zeros((tq, LANES), F32)
        acc_ref[hh] = jnp.zeros((tq, LANES), F32)

    @pl.when(qi + 1 < nq)
    def _():
        start_tile(nslot, qn_ref[0])

        def pair(jj, carry):
            for d in range(2):
                weighted(2 * jj + d)
                scores(nslot, 2 * jj + d, False)
            return carry

        lax.fori_loop(0, (qi + 1) // 2, pair, 0)

        @pl.when((qi + 1) % 2 == 1)
        def _():
            weighted(qi)
            scores(nslot, qi, False)

        finish_tile(nslot, qi + 1)

    @pl.when(qi + 1 == nq)
    def _():
        def last(jj, carry):
            for d in range(2):
                weighted(2 * jj + d)
            return carry

        lax.fori_loop(0, nq // 2, last, 0)

    outs = [acc_ref[hh] / jnp.sum(l_ref[hh], axis=-1, keepdims=True) for hh in range(2)]
    o_ref[0] = jnp.where(lane < HEAD_DIM, outs[0], outs[1]).astype(BF16)


def _attention(q, k, v, cum):
    bsz, seq, w = q.shape
    tq = ATTN_TQ
    pairs = w // LANES
    nq = seq // tq
    return pl.pallas_call(
        functools.partial(_attn_kernel, tq=tq, nq=nq),
        grid=(bsz, pairs, nq),
        in_specs=[pl.BlockSpec((1, tq, LANES), lambda b, h, i: (b, i, h)),
                  pl.BlockSpec((1, tq, LANES), lambda b, h, i: (b, jnp.minimum(i + 1, nq - 1), h)),
                  pl.BlockSpec((1, seq, LANES), lambda b, h, i: (b, 0, h)),
                  pl.BlockSpec((1, seq, LANES), lambda b, h, i: (b, 0, h)),
                  pl.BlockSpec((1, 8, seq), lambda b, h, i: (b, 0, 0))],
        out_specs=pl.BlockSpec((1, tq, LANES), lambda b, h, i: (b, i, h)),
        out_shape=jax.ShapeDtypeStruct((bsz, seq, w), BF16),
        scratch_shapes=[pltpu.VMEM((2, tq, LANES), BF16), pltpu.VMEM((2, 2, nq, tq, tq), F32),
                        pltpu.VMEM((2, 2, tq, LANES), F32), pltpu.VMEM((2, tq, LANES), F32),
                        pltpu.VMEM((2, tq, LANES), F32)],
        compiler_params=_params(("parallel", "parallel", "arbitrary")),
        name="fox_attention",
    )(q, q, k, v, cum)


def _ssd_kernel(xbc_ref, z_ref, dt_ref, cw_ref, cb_ref, dtb_ref, alog_ref, dw_ref, ng_ref, ex_ref, o_ref,
                buf_ref, h_ref, *, T, taps):
    ci = pl.program_id(1)
    gw = SSD_W // SSD_GROUPS

    @pl.when(ci == 0)
    def _():
        buf_ref[0:SSD_HALO, :] = jnp.zeros((SSD_HALO, SSD_CONV_CH), F32)
        h_ref[...] = jnp.zeros_like(h_ref)

    @pl.when(ci > 0)
    def _():
        buf_ref[0:SSD_HALO, :] = buf_ref[T:T + SSD_HALO, :]

    buf_ref[SSD_HALO:SSD_HALO + T, :] = xbc_ref[0]
    first = SSD_HALO - (taps - 1)
    acc = jnp.broadcast_to(cb_ref[...], (T, SSD_CONV_CH))
    for k in range(taps):
        acc = acc + _rows8(cw_ref, k, T) * buf_ref[first + k:first + k + T, :]
    xc = acc * _sigmoid(acc)
    xs = xc[:, 0:SSD_W]
    b_mat = xc[:, SSD_W:SSD_W + LANES]
    c_mat = xc[:, SSD_W + LANES:SSD_W + 2 * LANES]

    dt = _softplus(dt_ref[0] + dtb_ref[...])
    a = dt * (-jnp.exp(alog_ref[...]))
    acum = _select_dot(_lower_tri(T, BF16), a)
    acum_row = acum.T
    expand = ex_ref[...]
    dt_w = _dot_select(dt, expand)
    acum_w = _dot_select(acum, expand)
    last_w = acum_w[T - 1:T, :]
    xdt = xs * dt_w
    xdt_b = xdt.astype(BF16)
    xdec_b = (xdt * jnp.exp(last_w - acum_w)).astype(BF16)
    eacum_w = jnp.exp(acum_w)
    chunk_decay = jnp.exp(last_w)
    bb = b_mat.astype(BF16)
    cc = c_mat.astype(BF16)
    bt = b_mat.T.astype(BF16)

    row = lax.broadcasted_iota(jnp.int32, (T, T), 0)
    col = lax.broadcasted_iota(jnp.int32, (T, T), 1)
    causal = col <= row
    lane = lax.broadcasted_iota(jnp.int32, (T, LANES), 1)
    h_in = h_ref[...]
    h_ref[...] = h_in * chunk_decay + jnp.dot(bt, xdec_b, preferred_element_type=F32)
    h_in_b = h_in.astype(BF16)
    pieces = []
    for g in range(SSD_GROUPS):
        in_group = (lane >= g * SSD_STATE) & (lane < (g + 1) * SSD_STATE)
        cg = jnp.where(in_group, cc, jnp.zeros_like(cc))
        cbm = lax.dot_general(cg, bb, (((1,), (1,)), ((), ())), preferred_element_type=F32)
        yoff = jnp.dot(cg, h_in_b[:, g * gw:(g + 1) * gw], preferred_element_type=F32)
        for pr in range(gw // LANES):
            lo = g * gw + pr * LANES
            xpair = xdt_b[:, lo:lo + LANES]
            res = []
            for hh in range(2):
                head = lo // HEAD_DIM + hh
                seg = acum[:, head:head + 1] - acum_row[head:head + 1, :]
                lmat = jnp.exp(jnp.where(causal, seg, NEG_INF))
                res.append(jnp.dot((cbm * lmat).astype(BF16), xpair, preferred_element_type=F32))
            ydiag = jnp.where(lane < HEAD_DIM, res[0], res[1])
            pieces.append(ydiag + yoff[:, pr * LANES:(pr + 1) * LANES] * eacum_w[:, lo:lo + LANES])
    y = jnp.concatenate(pieces, axis=1) + xs * dw_ref[...]
    zz = z_ref[0]
    gated = y * (zz * _sigmoid(zz))
    ms = jnp.mean(gated * gated, axis=-1, keepdims=True)
    o_ref[0] = (gated * lax.rsqrt(ms + EPS) * ng_ref[...]).astype(BF16)


def _ssd(xbc, z, ds, cw, cb, dtb, alog, dwide, ng, expand):
    bsz, seq, _ = xbc.shape
    T = SSD_T
    taps = cw.shape[0]
    cw = jnp.repeat(cw, SUBLANES, axis=0)
    full = lambda a: pl.BlockSpec(a.shape, lambda bi, ci: (0, 0))
    tok = lambda w: pl.BlockSpec((1, T, w), lambda bi, ci: (bi, ci, 0))
    return pl.pallas_call(
        functools.partial(_ssd_kernel, T=T, taps=taps),
        grid=(bsz, seq // T),
        in_specs=[tok(SSD_CONV_CH), tok(SSD_W), tok(LANES), full(cw), full(cb), full(dtb), full(alog),
                  full(dwide), full(ng), full(expand)],
        out_specs=tok(SSD_W),
        out_shape=jax.ShapeDtypeStruct((bsz, seq, SSD_W), BF16),
        scratch_shapes=[pltpu.VMEM((T + SSD_HALO, SSD_CONV_CH), F32),
                        pltpu.VMEM((SSD_GROUPS * SSD_STATE, SSD_W), F32)],
        compiler_params=_params(("parallel", "arbitrary")),
        name="ssd_mixer",
    )(xbc, z, ds, cw, cb, dtb, alog, dwide, ng, expand)


def _first_max(vals):
    best = vals[0]
    for v in vals[1:]:
        best = jnp.maximum(best, v)
    idx = jnp.full(best.shape, float(len(vals) - 1), F32)
    for j in range(len(vals) - 2, -1, -1):
        idx = jnp.where(vals[j] == best, float(j), idx)
    return best, idx


def _outproj_kernel(yc_ref, ya_ref, ys_ref, x_ref, wo_ref, g_ref, wr_ref, br_ref,
                    x1_ref, hf_ref, rrow_ref, rcol_ref, *, tm):
    y = jnp.dot(yc_ref[...], wo_ref[0:CONV_CH, :], preferred_element_type=F32)
    y = y + jnp.dot(ya_ref[...], wo_ref[CONV_CH:CONV_CH + ATTN_W, :], preferred_element_type=F32)
    y = y + jnp.dot(ys_ref[...], wo_ref[CONV_CH + ATTN_W:, :], preferred_element_type=F32)
    x1 = x_ref[...] + y
    x1_ref[...] = x1
    ms = jnp.mean(x1 * x1, axis=-1, keepdims=True)
    hf = x1 * lax.rsqrt(ms + EPS) * g_ref[...]
    hf_hi = hf.astype(BF16)
    hf_ref[...] = hf_hi
    hf_lo = (hf - hf_hi.astype(F32)).astype(BF16)
    logits = (jnp.dot(hf_hi, wr_ref[0], preferred_element_type=F32)
              + jnp.dot(hf_hi, wr_ref[1], preferred_element_type=F32)
              + jnp.dot(hf_lo, wr_ref[0], preferred_element_type=F32)) + br_ref[...]
    lt = logits.T
    gl = [lt[j:j + 1, :] for j in range(MOE_GROUPS)]
    gmax, gidx = _first_max(gl)
    denom = gl[0] * 0.0
    for v in gl:
        denom = denom + jnp.exp(v - gmax)
    gval = 1.0 / denom
    esel = []
    for j in range(EXPERTS_PER_GROUP):
        erow = lambda g: lt[MOE_GROUPS + g * EXPERTS_PER_GROUP + j:MOE_GROUPS + g * EXPERTS_PER_GROUP + j + 1, :]
        v = erow(MOE_GROUPS - 1)
        for g in range(MOE_GROUPS - 2, -1, -1):
            v = jnp.where(gidx == float(g), erow(g), v)
        esel.append(v)
    v1, i1 = _first_max(esel)
    rest = [jnp.where(i1 == float(j), NEG_INF, esel[j]) for j in range(EXPERTS_PER_GROUP)]
    v2, i2 = _first_max(rest)
    e2 = jnp.exp(v2 - v1)
    w1 = (1.0 / (1.0 + e2)) * gval
    w2 = (e2 / (1.0 + e2)) * gval
    cw = [jnp.where(i1 == float(j), w1, jnp.where(i2 == float(j), w2, 0.0)) for j in range(EXPERTS_PER_GROUP)]
    hi = [c.astype(BF16).astype(F32) for c in cw]
    mid = [(c - h).astype(BF16).astype(F32) for c, h in zip(cw, hi)]
    lo = [(c - h - m).astype(BF16).astype(F32) for c, h, m in zip(cw, hi, mid)]
    rows = [gidx] + hi + mid + lo
    ri = lax.broadcasted_iota(jnp.int32, (LANES, tm), 0)
    packed = jnp.zeros((LANES, tm), F32)
    for j, r in enumerate(rows):
        packed = jnp.where(ri == j, r, packed)
    rrow_ref[...] = packed[0:8, :]
    rcol_ref[...] = packed.T


def _outproj(yc, ya, ys, x, wo, g, wr, br):
    n, d = x.shape
    tm = ROW_TILE
    row = lambda w: pl.BlockSpec((tm, w), lambda i: (i, 0))
    full = lambda a: pl.BlockSpec(a.shape, lambda i: (0, 0))
    return pl.pallas_call(
        functools.partial(_outproj_kernel, tm=tm),
        grid=(n // tm,),
        in_specs=[row(CONV_CH), row(ATTN_W), row(SSD_W), row(d), full(wo), full(g),
                  pl.BlockSpec(wr.shape, lambda i: (0, 0, 0)), full(br)],
        out_specs=[row(d), row(d), pl.BlockSpec((8, tm), lambda i: (0, i)), row(LANES)],
        out_shape=[jax.ShapeDtypeStruct((n, d), F32), jax.ShapeDtypeStruct((n, d), BF16),
                   jax.ShapeDtypeStruct((8, n), F32), jax.ShapeDtypeStruct((n, LANES), F32)],
        compiler_params=_params(("parallel",)),
        name="outproj_router",
    )(yc, ya, ys, x, wo, g, wr, br)


def _moe_kernel(cnt_ref, x1_ref, hf_ref, rrow_ref, rcol_ref, wg_ref, wu_ref, wd_ref, fg_ref, o_ref,
                xg_ref, acc_ref, cw_ref, rankr_ref, *, ts, spans, rows, cap, final_norm):
    i = pl.program_id(0)
    g = pl.program_id(1)
    e = pl.program_id(2)
    gf = g.astype(F32)

    def tok(sb):
        return slice(sb * ts, (sb + 1) * ts)

    def chunk(sb, c, r):
        if isinstance(c, int):
            return slice(sb * cap + c * r, sb * cap + (c + 1) * r)
        return pl.ds(pl.multiple_of(sb * cap + c * r, 16), r)

    def group_rank(sb):
        sub8 = lax.broadcasted_iota(jnp.int32, (8, ts), 0)
        return jnp.sum(jnp.where(sub8 == g, rankr_ref[:, tok(sb)], 0.0), axis=0, keepdims=True)

    def placement(sb, c, r):
        first = float(c * r) if isinstance(c, int) else (c * r).astype(F32)
        slot = lax.broadcasted_iota(jnp.int32, (r, ts), 0).astype(F32) + first
        return jnp.where((group_rank(sb) == slot) & (rrow_ref[0:1, tok(sb)] == gf), 1.0, 0.0).astype(BF16)

    def gather(sb, c, r):
        p = placement(sb, c, r)
        xg_ref[chunk(sb, c, r), :] = jnp.dot(p, hf_ref[tok(sb), :], preferred_element_type=F32).astype(BF16)
        cw_ref[chunk(sb, c, r), :] = jnp.dot(p, rcol_ref[tok(sb), :].astype(BF16), preferred_element_type=F32)
        acc_ref[chunk(sb, c, r), :] = jnp.zeros((r, acc_ref.shape[1]), F32)

    def expert(sb, c, r):
        lane_r = lax.broadcasted_iota(jnp.int32, (r, LANES), 1)
        pick = ((lane_r == 1 + e) | (lane_r == 1 + EXPERTS_PER_GROUP + e)
                | (lane_r == 1 + 2 * EXPERTS_PER_GROUP + e))
        xc = xg_ref[chunk(sb, c, r), :]
        hg = jnp.dot(xc, wg_ref[0], preferred_element_type=F32)
        hu = jnp.dot(xc, wu_ref[0], preferred_element_type=F32)
        hh = (hg * _sigmoid(hg) * hu).astype(BF16)
        y = jnp.dot(hh, wd_ref[0], preferred_element_type=F32)
        cwe = jnp.sum(jnp.where(pick, cw_ref[chunk(sb, c, r), :], 0.0), axis=-1, keepdims=True)
        acc_ref[chunk(sb, c, r), :] += cwe * y

    def scatter(sb, c, r):
        tn = (((0,), (0,)), ((), ()))
        o_ref[tok(sb), :] += lax.dot_general(placement(sb, c, r), acc_ref[chunk(sb, c, r), :].astype(BF16), tn,
                                             preferred_element_type=F32)

    def step(sb, c, r):
        @pl.when(e == 0)
        def _():
            gather(sb, c, r)

        expert(sb, c, r)

        @pl.when(e == EXPERTS_PER_GROUP - 1)
        def _():
            scatter(sb, c, r)

    @pl.when((g == 0) & (e == 0))
    def _():
        o_ref[...] = x1_ref[...]
        sub = lax.broadcasted_iota(jnp.int32, (8, ts), 0).astype(F32)
        for c0 in range(0, ts, RANK_STRIP):
            r_ = lax.broadcasted_iota(jnp.int32, (ts, RANK_STRIP), 0)
            c_ = lax.broadcasted_iota(jnp.int32, (ts, RANK_STRIP), 1) + c0
            before = jnp.where(r_ < c_, 1.0, 0.0).astype(BF16)
            for sb in range(spans):
                m_row = jnp.where(rrow_ref[0:1, tok(sb)] == sub, 1.0, 0.0).astype(BF16)
                rankr_ref[:, sb * ts + c0:sb * ts + c0 + RANK_STRIP] = jnp.dot(
                    m_row, before, preferred_element_type=F32)

    for sb in range(spans):
        cnt = cnt_ref[(i * spans + sb) * MOE_GROUPS + g]
        lo = 0
        for r in rows[:-1]:
            @pl.when((cnt > lo) & (cnt <= r))
            def _(sb=sb, r=r):
                step(sb, 0, r)
            lo = r

        @pl.when(cnt > lo)
        def _(sb=sb, cnt=cnt):
            r = rows[-1]

            def body(c, carry):
                step(sb, c, r)
                return carry

            lax.fori_loop(0, (cnt + r - 1) // r, body, 0)

    if final_norm:
        @pl.when((g == MOE_GROUPS - 1) & (e == EXPERTS_PER_GROUP - 1))
        def _():
            xo = o_ref[...]
            ms = jnp.mean(xo * xo, axis=-1, keepdims=True)
            o_ref[...] = xo * lax.rsqrt(ms + EPS) * fg_ref[...]


def _moe(counts, x1, hf, rrow, rcol, wg, wu, wd, fg, final_norm):
    n, d = x1.shape
    tb, ts, rows = MOE_BLOCK, MOE_SORT, MOE_ROWS
    spans = tb // ts
    de = wg.shape[2]
    cap = -(-ts // rows[-1]) * rows[-1]
    once = pl.Buffered(1)
    tokens = lambda w, **kw: pl.BlockSpec((tb, w), lambda i, g, e, c: (i, 0), **kw)
    expert = lambda shape: pl.BlockSpec(shape, lambda i, g, e, c: (g * EXPERTS_PER_GROUP + e, 0, 0))
    grid_spec = pltpu.PrefetchScalarGridSpec(
        num_scalar_prefetch=1,
        grid=(n // tb, MOE_GROUPS, EXPERTS_PER_GROUP),
        in_specs=[tokens(d, pipeline_mode=once), tokens(d, pipeline_mode=once),
                  pl.BlockSpec((8, tb), lambda i, g, e, c: (0, i)), tokens(LANES),
                  expert((1, d, de)), expert((1, d, de)), expert((1, de, d)),
                  pl.BlockSpec((1, d), lambda i, g, e, c: (0, 0))],
        out_specs=tokens(d),
        scratch_shapes=[pltpu.VMEM((spans * cap, d), BF16), pltpu.VMEM((spans * cap, d), F32),
                        pltpu.VMEM((spans * cap, LANES), F32), pltpu.VMEM((8, tb), F32)],
    )
    return pl.pallas_call(
        functools.partial(_moe_kernel, ts=ts, spans=spans, rows=rows, cap=cap, final_norm=final_norm),
        grid_spec=grid_spec,
        out_shape=jax.ShapeDtypeStruct((n, d), F32),
        compiler_params=_params(("parallel", "arbitrary", "arbitrary")),
        name="moe",
    )(counts, x1, hf, rrow, rcol, wg, wu, wd, fg)


def _pad_lanes(v, width=LANES):
    v = v.reshape(1, -1).astype(F32)
    return jnp.pad(v, ((0, 0), (0, width - v.shape[1])))


def kernel(x, norm_mix, w_in, conv_dw_w, conv_dw_b, conv_ln_g, conv_ln_b, fgate_b, ssd_conv_w, ssd_conv_b,
           ssd_dt_bias, ssd_A_log, ssd_D, ssd_norm_g, w_out, norm_ffn, router_group_w, router_group_b,
           router_expert_w, router_expert_b, expert_w_gate, expert_w_up, expert_w_down, norm_final):
    bsz, seq, d = x.shape
    n = bsz * seq
    depth = w_in.shape[0]
    heads = fgate_b.shape[1]
    ssd_heads = ssd_A_log.shape[1]
    n_exp = expert_w_gate.shape[1]
    assert heads * HEAD_DIM == ATTN_W and ssd_heads * HEAD_DIM == SSD_W
    assert n_exp == MOE_GROUPS * EXPERTS_PER_GROUP and conv_dw_w.shape[1] <= CONV_HALO + 1
    assert n % ROW_TILE == 0 and n % MOE_BLOCK == 0 and seq % max(CONV_TILE, CUM_TILE, 2 * ATTN_TQ, SSD_T) == 0

    sizes = (2 * CONV_CH, ATTN_W, ATTN_W, ATTN_W, heads, SSD_W, SSD_W, SSD_GROUPS * SSD_STATE,
             SSD_GROUPS * SSD_STATE, ssd_heads)
    offs = [0]
    for s in sizes:
        offs.append(offs[-1] + s)
    expand = jnp.repeat(jnp.eye(LANES, SSD_W // HEAD_DIM, dtype=BF16), HEAD_DIM, axis=1)

    xr = x.reshape(n, d)
    for l in range(depth):
        w = w_in[l]
        wa = jnp.concatenate([w[:, offs[0]:offs[4]], w[:, offs[5]:offs[9]]], axis=1).astype(BF16)
        zpad = jnp.zeros((d, LANES - heads), F32)
        ws = jnp.concatenate([w[:, offs[4]:offs[5]], zpad, w[:, offs[9]:offs[10]], zpad], axis=1).astype(BF16)
        glu, q, k, v, z, xbc, fs, ds = _inproj(xr, norm_mix[l].reshape(1, d), wa, ws)

        seq3 = lambda a: a.reshape(bsz, seq, a.shape[-1])
        y_conv = _conv(seq3(glu), conv_dw_w[l], conv_dw_b[l].reshape(1, -1), conv_ln_g[l].reshape(1, -1),
                       conv_ln_b[l].reshape(1, -1))
        cum = _cum(seq3(fs), _pad_lanes(fgate_b[l]))
        y_attn = _attention(seq3(q), seq3(k), seq3(v), cum)
        y_ssd = _ssd(seq3(xbc), seq3(z), seq3(ds), ssd_conv_w[l], ssd_conv_b[l].reshape(1, -1),
                     _pad_lanes(ssd_dt_bias[l]), _pad_lanes(ssd_A_log[l]),
                     jnp.repeat(ssd_D[l].astype(F32), HEAD_DIM).reshape(1, -1), ssd_norm_g[l].reshape(1, -1),
                     expand)

        wr = jnp.concatenate([router_group_w[l], router_expert_w[l],
                              jnp.zeros((d, LANES - MOE_GROUPS - n_exp), F32)], axis=1)
        wr_hi = wr.astype(BF16)
        wr = jnp.stack([wr_hi, (wr - wr_hi.astype(F32)).astype(BF16)])
        br = _pad_lanes(jnp.concatenate([router_group_b[l], router_expert_b[l]]))
        x1, hf, rrow, rcol = _outproj(y_conv.reshape(n, -1), y_attn.reshape(n, -1), y_ssd.reshape(n, -1), xr,
                                      w_out[l].astype(BF16), norm_ffn[l].reshape(1, d), wr, br)

        gidx = rrow[0].astype(jnp.int32).reshape(n // MOE_SORT, MOE_SORT)
        counts = jnp.sum(gidx[:, :, None] == jnp.arange(MOE_GROUPS, dtype=jnp.int32), axis=1,
                         dtype=jnp.int32).reshape(-1)
        xr = _moe(counts, x1, hf, rrow, rcol, expert_w_gate[l].astype(BF16), expert_w_up[l].astype(BF16),
                  expert_w_down[l].astype(BF16), norm_final.reshape(1, d), final_norm=(l == depth - 1))
    return xr.reshape(bsz, seq, d)
```

```python
import functools

import jax
import jax.numpy as jnp
from jax import lax
from jax.experimental import pallas as pl
from jax.experimental.pallas import tpu as pltpu

F32 = jnp.float32
BF16 = jnp.bfloat16
EPS = 1e-6
NEG_INF = float("-inf")

HEAD_DIM = 64
LANES = 128
CONV_CH = 512
ATTN_W = 512
SSD_W = 512
SSD_GROUPS = 2
SSD_STATE = 64
SSD_CONV_CH = SSD_W + 2 * SSD_GROUPS * SSD_STATE
MOE_GROUPS = 4
EXPERTS_PER_GROUP = 4

ROW_TILE = 512
CONV_TILE = 512
CONV_ROWS = 32
CONV_HALO = 32
SUBLANES = 8
CUM_TILE = 512
ATTN_TQ = 512
SSD_T = 256
SSD_CHUNKS_PER_STEP = 2
SSD_HALO = 8
MOE_BLOCK = 2048
MOE_SORT = 1024
MOE_ROWS = (256, 288, 320, 384)
RANK_STRIP = 256
VMEM_LIMIT = 60 * 1024 * 1024


def _params(sem):
    return pltpu.CompilerParams(dimension_semantics=sem, vmem_limit_bytes=VMEM_LIMIT)


def _sigmoid(x):
    return jax.nn.sigmoid(x)


def _softplus(x):
    return jnp.maximum(x, 0.0) + jnp.log1p(jnp.exp(-jnp.abs(x)))


def _split3(x):
    hi = x.astype(BF16)
    r = x - hi.astype(F32)
    mid = r.astype(BF16)
    lo = (r - mid.astype(F32)).astype(BF16)
    return hi, mid, lo


def _select_dot(sel, x):
    hi, mid, lo = _split3(x)
    d = lambda p: jnp.dot(sel, p, preferred_element_type=F32)
    return d(hi) + d(mid) + d(lo)


def _dot_select(x, sel):
    hi, mid, lo = _split3(x)
    d = lambda p: jnp.dot(p, sel, preferred_element_type=F32)
    return d(hi) + d(mid) + d(lo)


def _rows8(w_ref, k, rows):
    return jnp.concatenate([w_ref[k * SUBLANES:(k + 1) * SUBLANES, :]] * (rows // SUBLANES), axis=0)


def _inproj_kernel(x_ref, g_ref, wa_ref, ws_ref, glu_ref, q_ref, k_ref, v_ref, z_ref, xbc_ref,
                   fs_ref, ds_ref):
    xf = x_ref[...]
    ms = jnp.mean(xf * xf, axis=-1, keepdims=True)
    h = (xf * lax.rsqrt(ms + EPS) * g_ref[...]).astype(BF16)

    def mm(lo, hi):
        return jnp.dot(h, wa_ref[:, lo:hi], preferred_element_type=F32)

    glu_ref[...] = mm(0, 512) * _sigmoid(mm(512, 1024))
    q_ref[...] = (mm(1024, 1536) * (HEAD_DIM ** -0.5)).astype(BF16)
    k_ref[...] = mm(1536, 2048).astype(BF16)
    v_ref[...] = mm(2048, 2560).astype(BF16)
    z_ref[...] = mm(2560, 3072)
    xbc_ref[...] = mm(3072, 3840)
    small = jnp.dot(h, ws_ref[...], preferred_element_type=F32)
    fs_ref[...] = small[:, 0:LANES]
    ds_ref[...] = small[:, LANES:2 * LANES]


def _inproj(x, g, wa, ws):
    n, d = x.shape
    tm = ROW_TILE
    row = lambda w: pl.BlockSpec((tm, w), lambda i: (i, 0))
    full = lambda a: pl.BlockSpec(a.shape, lambda i: (0, 0))
    shapes = [(512, F32), (512, BF16), (512, BF16), (512, BF16), (512, F32), (SSD_CONV_CH, F32),
              (LANES, F32), (LANES, F32)]
    return pl.pallas_call(
        _inproj_kernel,
        grid=(n // tm,),
        in_specs=[row(d), full(g), full(wa), full(ws)],
        out_specs=[row(w) for w, _ in shapes],
        out_shape=[jax.ShapeDtypeStruct((n, w), dt) for w, dt in shapes],
        compiler_params=_params(("parallel",)),
        name="inproj",
    )(x, g, wa, ws)


def _conv_kernel(x_ref, w_ref, b_ref, lg_ref, lb_ref, o_ref, buf_ref, *, taps, tl):
    li = pl.program_id(1)

    @pl.when(li == 0)
    def _():
        buf_ref[0:CONV_HALO, :] = jnp.zeros((CONV_HALO, CONV_CH), F32)

    @pl.when(li > 0)
    def _():
        buf_ref[0:CONV_HALO, :] = buf_ref[tl:tl + CONV_HALO, :]

    buf_ref[CONV_HALO:CONV_HALO + tl, :] = x_ref[0]
    first = CONV_HALO - (taps - 1)
    by_shift = [[k for k in range(taps) if (first + k) % SUBLANES == s] for s in range(SUBLANES)]
    for r in range(0, tl, CONV_ROWS):
        acc = jnp.broadcast_to(b_ref[...], (CONV_ROWS, CONV_CH))
        for s, ks in enumerate(by_shift):
            if not ks:
                continue
            span = CONV_ROWS + (SUBLANES if s else 0)
            z = None
            for k in ks:
                base = r + first + k - s
                term = _rows8(w_ref, k, span) * buf_ref[base:base + span, :]
                z = term if z is None else z + term
            acc = acc + z[s:s + CONV_ROWS, :]
        mu = jnp.mean(acc, axis=-1, keepdims=True)
        cen = acc - mu
        var = jnp.mean(cen * cen, axis=-1, keepdims=True)
        y = cen * lax.rsqrt(var + EPS) * lg_ref[...] + lb_ref[...]
        o_ref[0, r:r + CONV_ROWS, :] = (y * _sigmoid(y)).astype(BF16)


def _conv(glu, w, b, lg, lb):
    bsz, seq, c = glu.shape
    tl = CONV_TILE
    taps = w.shape[0]
    w = jnp.repeat(w, SUBLANES, axis=0)
    full = lambda a: pl.BlockSpec(a.shape, lambda bi, li: (0, 0))
    return pl.pallas_call(
        functools.partial(_conv_kernel, taps=taps, tl=tl),
        grid=(bsz, seq // tl),
        in_specs=[pl.BlockSpec((1, tl, c), lambda bi, li: (bi, li, 0)), full(w), full(b), full(lg), full(lb)],
        out_specs=pl.BlockSpec((1, tl, c), lambda bi, li: (bi, li, 0)),
        out_shape=jax.ShapeDtypeStruct((bsz, seq, c), BF16),
        scratch_shapes=[pltpu.VMEM((tl + CONV_HALO, c), F32)],
        compiler_params=_params(("parallel", "arbitrary")),
        name="dwconv",
    )(glu, w, b, lg, lb)


def _lower_tri(n, dtype, strict=False):
    r = lax.broadcasted_iota(jnp.int32, (n, n), 0)
    c = lax.broadcasted_iota(jnp.int32, (n, n), 1)
    keep = (c < r) if strict else (c <= r)
    return jnp.where(keep, 1.0, 0.0).astype(dtype)


def _cum_kernel(f_ref, b_ref, o_ref, carry_ref, *, tl):
    li = pl.program_id(1)

    @pl.when(li == 0)
    def _():
        carry_ref[...] = jnp.zeros_like(carry_ref)

    x = f_ref[0] + b_ref[...]
    lf = jnp.minimum(x, 0.0) - jnp.log1p(jnp.exp(-jnp.abs(x)))
    cum = _select_dot(_lower_tri(tl, BF16), lf) + carry_ref[...]
    carry_ref[...] = cum[tl - 1:tl, :]
    o_ref[0] = cum.T[0:8, :]


def _cum(fs, fb):
    bsz, seq, _ = fs.shape
    tl = CUM_TILE
    return pl.pallas_call(
        functools.partial(_cum_kernel, tl=tl),
        grid=(bsz, seq // tl),
        in_specs=[pl.BlockSpec((1, tl, LANES), lambda bi, li: (bi, li, 0)),
                  pl.BlockSpec((1, LANES), lambda bi, li: (0, 0))],
        out_specs=pl.BlockSpec((1, 8, tl), lambda bi, li: (bi, 0, li)),
        out_shape=jax.ShapeDtypeStruct((bsz, 8, seq), F32),
        scratch_shapes=[pltpu.VMEM((1, LANES), F32)],
        compiler_params=_params(("parallel", "arbitrary")),
        name="fgate_cumsum",
    )(fs, fb)


def _attn_kernel(q_ref, qn_ref, k_ref, v_ref, c_ref, o_ref, qm_ref, s_ref, m_ref, l_ref, acc_ref, *, tq, nq):
    hp = pl.program_id(1)
    qi = pl.program_id(2)
    slot = qi % 2
    nslot = 1 - slot
    lane = lax.broadcasted_iota(jnp.int32, (tq, LANES), 1)
    row = lax.broadcasted_iota(jnp.int32, (tq, tq), 0)
    col = lax.broadcasted_iota(jnp.int32, (tq, tq), 1)
    tiles = tq // LANES

    def start_tile(sl, q):
        for hh in range(2):
            in_head = (lane >= HEAD_DIM) if hh else (lane < HEAD_DIM)
            qm_ref[hh] = jnp.where(in_head, q, jnp.zeros_like(q))
            m_ref[sl, hh] = jnp.full((tq, LANES), NEG_INF, F32)

    def scores(sl, j, masked):
        k0 = pl.multiple_of(j * tq, tq)
        kb = k_ref[0, pl.ds(k0, tq), :]
        for hh in range(2):
            cb = c_ref[0, pl.ds(hp * 2 + hh, 1), pl.ds(k0, tq)]
            s = lax.dot_general(qm_ref[hh], kb, (((1,), (1,)), ((), ())), preferred_element_type=F32) - cb
            if masked:
                s = jnp.where(col <= row, s, NEG_INF)
            s_ref[sl, hh, j] = s
            mt = m_ref[sl, hh]
            for t in range(tiles):
                mt = jnp.maximum(mt, s[:, t * LANES:(t + 1) * LANES])
            m_ref[sl, hh] = mt

    def finish_tile(sl, diag):
        scores(sl, diag, True)
        for hh in range(2):
            m_ref[sl, hh] = jnp.broadcast_to(jnp.max(m_ref[sl, hh], axis=-1, keepdims=True), (tq, LANES))

    def weighted(j):
        k0 = pl.multiple_of(j * tq, tq)
        vb = v_ref[0, pl.ds(k0, tq), :]
        for hh in range(2):
            mb = m_ref[slot, hh]
            lt = l_ref[hh]
            ps = []
            for t in range(tiles):
                p = jnp.exp(s_ref[slot, hh, j, :, t * LANES:(t + 1) * LANES] - mb)
                lt = lt + p
                ps.append(p.astype(BF16))
            l_ref[hh] = lt
            acc_ref[hh] += jnp.dot(jnp.concatenate(ps, axis=1), vb, preferred_element_type=F32)

    @pl.when(qi == 0)
    def _():
        start_tile(0, q_ref[0])
        finish_tile(0, 0)

    for hh in range(2):
        l_ref[hh] = jnp.zeros((tq, LANES), F32)
        acc_ref[hh] = jnp.zeros((tq, LANES), F32)

    @pl.when(qi + 1 < nq)
    def _():
        start_tile(nslot, qn_ref[0])

        def pair(jj, carry):
            for d in range(2):
                weighted(2 * jj + d)
                scores(nslot, 2 * jj + d, False)
            return carry

        lax.fori_loop(0, (qi + 1) // 2, pair, 0)

        @pl.when((qi + 1) % 2 == 1)
        def _():
            weighted(qi)
            scores(nslot, qi, False)

        finish_tile(nslot, qi + 1)

    @pl.when(qi + 1 == nq)
    def _():
        def last(jj, carry):
            for d in range(2):
                weighted(2 * jj + d)
            return carry

        lax.fori_loop(0, nq // 2, last, 0)

    outs = [acc_ref[hh] / jnp.sum(l_ref[hh], axis=-1, keepdims=True) for hh in range(2)]
    o_ref[0] = jnp.where(lane < HEAD_DIM, outs[0], outs[1]).astype(BF16)


def _attention(q, k, v, cum):
    bsz, seq, w = q.shape
    tq = ATTN_TQ
    pairs = w // LANES
    nq = seq // tq
    return pl.pallas_call(
        functools.partial(_attn_kernel, tq=tq, nq=nq),
        grid=(bsz, pairs, nq),
        in_specs=[pl.BlockSpec((1, tq, LANES), lambda b, h, i: (b, i, h)),
                  pl.BlockSpec((1, tq, LANES), lambda b, h, i: (b, jnp.minimum(i + 1, nq - 1), h)),
                  pl.BlockSpec((1, seq, LANES), lambda b, h, i: (b, 0, h)),
                  pl.BlockSpec((1, seq, LANES), lambda b, h, i: (b, 0, h)),
                  pl.BlockSpec((1, 8, seq), lambda b, h, i: (b, 0, 0))],
        out_specs=pl.BlockSpec((1, tq, LANES), lambda b, h, i: (b, i, h)),
        out_shape=jax.ShapeDtypeStruct((bsz, seq, w), BF16),
        scratch_shapes=[pltpu.VMEM((2, tq, LANES), BF16), pltpu.VMEM((2, 2, nq, tq, tq), F32),
                        pltpu.VMEM((2, 2, tq, LANES), F32), pltpu.VMEM((2, tq, LANES), F32),
                        pltpu.VMEM((2, tq, LANES), F32)],
        compiler_params=_params(("parallel", "parallel", "arbitrary")),
        name="fox_attention",
    )(q, q, k, v, cum)


def _ssd_kernel(xbc_ref, z_ref, dt_ref, cw_ref, cb_ref, dtb_ref, alog_ref, dw_ref, ng_ref, ex_ref, o_ref,
                buf_ref, h_ref, *, T, taps, chunks):
    ci = pl.program_id(1)
    span = T * chunks

    @pl.when(ci == 0)
    def _():
        buf_ref[0:SSD_HALO, :] = jnp.zeros((SSD_HALO, SSD_CONV_CH), F32)
        h_ref[...] = jnp.zeros_like(h_ref)

    @pl.when(ci > 0)
    def _():
        buf_ref[0:SSD_HALO, :] = buf_ref[span:span + SSD_HALO, :]

    buf_ref[SSD_HALO:SSD_HALO + span, :] = xbc_ref[0]
    for c in range(chunks):
        rows = pl.ds(c * T, T)
        _ssd_chunk(c * T, z_ref.at[0, rows], dt_ref.at[0, rows], cw_ref, cb_ref, dtb_ref, alog_ref, dw_ref,
                   ng_ref, ex_ref, o_ref.at[0, rows], buf_ref, h_ref, T=T, taps=taps)


def _ssd_chunk(base, z_ref, dt_ref, cw_ref, cb_ref, dtb_ref, alog_ref, dw_ref, ng_ref, ex_ref, o_ref,
               buf_ref, h_ref, *, T, taps):
    gw = SSD_W // SSD_GROUPS
    first = base + SSD_HALO - (taps - 1)
    acc = jnp.broadcast_to(cb_ref[...], (T, SSD_CONV_CH))
    for k in range(taps):
        acc = acc + _rows8(cw_ref, k, T) * buf_ref[first + k:first + k + T, :]
    xc = acc * _sigmoid(acc)
    xs = xc[:, 0:SSD_W]
    b_mat = xc[:, SSD_W:SSD_W + LANES]
    c_mat = xc[:, SSD_W + LANES:SSD_W + 2 * LANES]

    dt = _softplus(dt_ref[...] + dtb_ref[...])
    a = dt * (-jnp.exp(alog_ref[...]))
    acum = _select_dot(_lower_tri(T, BF16), a)
    acum_row = acum.T
    expand = ex_ref[...]
    dt_w = _dot_select(dt, expand)
    acum_w = _dot_select(acum, expand)
    last_w = acum_w[T - 1:T, :]
    xdt = xs * dt_w
    xdt_b = xdt.astype(BF16)
    xdec_b = (xdt * jnp.exp(last_w - acum_w)).astype(BF16)
    eacum_w = jnp.exp(acum_w)
    chunk_decay = jnp.exp(last_w)
    bb = b_mat.astype(BF16)
    cc = c_mat.astype(BF16)
    bt = b_mat.T.astype(BF16)

    row = lax.broadcasted_iota(jnp.int32, (T, T), 0)
    col = lax.broadcasted_iota(jnp.int32, (T, T), 1)
    causal = col <= row
    lane = lax.broadcasted_iota(jnp.int32, (T, LANES), 1)
    h_in = h_ref[...]
    h_ref[...] = h_in * chunk_decay + jnp.dot(bt, xdec_b, preferred_element_type=F32)
    h_in_b = h_in.astype(BF16)
    pieces = []
    for g in range(SSD_GROUPS):
        in_group = (lane >= g * SSD_STATE) & (lane < (g + 1) * SSD_STATE)
        cg = jnp.where(in_group, cc, jnp.zeros_like(cc))
        cbm = lax.dot_general(cg, bb, (((1,), (1,)), ((), ())), preferred_element_type=F32)
        yoff = jnp.dot(cg, h_in_b[:, g * gw:(g + 1) * gw], preferred_element_type=F32)
        for pr in range(gw // LANES):
            lo = g * gw + pr * LANES
            xpair = xdt_b[:, lo:lo + LANES]
            res = []
            for hh in range(2):
                head = lo // HEAD_DIM + hh
                seg = acum[:, head:head + 1] - acum_row[head:head + 1, :]
                lmat = jnp.exp(jnp.where(causal, seg, NEG_INF))
                res.append(jnp.dot((cbm * lmat).astype(BF16), xpair, preferred_element_type=F32))
            ydiag = jnp.where(lane < HEAD_DIM, res[0], res[1])
            pieces.append(ydiag + yoff[:, pr * LANES:(pr + 1) * LANES] * eacum_w[:, lo:lo + LANES])
    y = jnp.concatenate(pieces, axis=1) + xs * dw_ref[...]
    zz = z_ref[...]
    gated = y * (zz * _sigmoid(zz))
    ms = jnp.mean(gated * gated, axis=-1, keepdims=True)
    o_ref[...] = (gated * lax.rsqrt(ms + EPS) * ng_ref[...]).astype(BF16)


def _ssd(xbc, z, ds, cw, cb, dtb, alog, dwide, ng, expand):
    bsz, seq, _ = xbc.shape
    T = SSD_T
    taps = cw.shape[0]
    cw = jnp.repeat(cw, SUBLANES, axis=0)
    full = lambda a: pl.BlockSpec(a.shape, lambda bi, ci: (0, 0))
    span = T * SSD_CHUNKS_PER_STEP
    tok = lambda w: pl.BlockSpec((1, span, w), lambda bi, ci: (bi, ci, 0))
    return pl.pallas_call(
        functools.partial(_ssd_kernel, T=T, taps=taps, chunks=SSD_CHUNKS_PER_STEP),
        grid=(bsz, seq // span),
        in_specs=[tok(SSD_CONV_CH), tok(SSD_W), tok(LANES), full(cw), full(cb), full(dtb), full(alog),
                  full(dwide), full(ng), full(expand)],
        out_specs=tok(SSD_W),
        out_shape=jax.ShapeDtypeStruct((bsz, seq, SSD_W), BF16),
        scratch_shapes=[pltpu.VMEM((span + SSD_HALO, SSD_CONV_CH), F32),
                        pltpu.VMEM((SSD_GROUPS * SSD_STATE, SSD_W), F32)],
        compiler_params=_params(("parallel", "arbitrary")),
        name="ssd_mixer",
    )(xbc, z, ds, cw, cb, dtb, alog, dwide, ng, expand)


def _first_max(vals):
    best = vals[0]
    for v in vals[1:]:
        best = jnp.maximum(best, v)
    idx = jnp.full(best.shape, float(len(vals) - 1), F32)
    for j in range(len(vals) - 2, -1, -1):
        idx = jnp.where(vals[j] == best, float(j), idx)
    return best, idx


def _outproj_kernel(yc_ref, ya_ref, ys_ref, x_ref, wo_ref, g_ref, wr_ref, br_ref,
                    x1_ref, hf_ref, rrow_ref, rcol_ref, *, tm):
    y = jnp.dot(yc_ref[...], wo_ref[0:CONV_CH, :], preferred_element_type=F32)
    y = y + jnp.dot(ya_ref[...], wo_ref[CONV_CH:CONV_CH + ATTN_W, :], preferred_element_type=F32)
    y = y + jnp.dot(ys_ref[...], wo_ref[CONV_CH + ATTN_W:, :], preferred_element_type=F32)
    x1 = x_ref[...] + y
    x1_ref[...] = x1
    ms = jnp.mean(x1 * x1, axis=-1, keepdims=True)
    hf = x1 * lax.rsqrt(ms + EPS) * g_ref[...]
    hf_hi = hf.astype(BF16)
    hf_ref[...] = hf_hi
    hf_lo = (hf - hf_hi.astype(F32)).astype(BF16)
    logits = (jnp.dot(hf_hi, wr_ref[0], preferred_element_type=F32)
              + jnp.dot(hf_hi, wr_ref[1], preferred_element_type=F32)
              + jnp.dot(hf_lo, wr_ref[0], preferred_element_type=F32)) + br_ref[...]
    lt = logits.T
    gl = [lt[j:j + 1, :] for j in range(MOE_GROUPS)]
    gmax, gidx = _first_max(gl)
    denom = gl[0] * 0.0
    for v in gl:
        denom = denom + jnp.exp(v - gmax)
    gval = 1.0 / denom
    esel = []
    for j in range(EXPERTS_PER_GROUP):
        erow = lambda g: lt[MOE_GROUPS + g * EXPERTS_PER_GROUP + j:MOE_GROUPS + g * EXPERTS_PER_GROUP + j + 1, :]
        v = erow(MOE_GROUPS - 1)
        for g in range(MOE_GROUPS - 2, -1, -1):
            v = jnp.where(gidx == float(g), erow(g), v)
        esel.append(v)
    v1, i1 = _first_max(esel)
    rest = [jnp.where(i1 == float(j), NEG_INF, esel[j]) for j in range(EXPERTS_PER_GROUP)]
    v2, i2 = _first_max(rest)
    e2 = jnp.exp(v2 - v1)
    w1 = (1.0 / (1.0 + e2)) * gval
    w2 = (e2 / (1.0 + e2)) * gval
    cw = [jnp.where(i1 == float(j), w1, jnp.where(i2 == float(j), w2, 0.0)) for j in range(EXPERTS_PER_GROUP)]
    hi = [c.astype(BF16).astype(F32) for c in cw]
    mid = [(c - h).astype(BF16).astype(F32) for c, h in zip(cw, hi)]
    lo = [(c - h - m).astype(BF16).astype(F32) for c, h, m in zip(cw, hi, mid)]
    rows = [gidx] + hi + mid + lo
    ri = lax.broadcasted_iota(jnp.int32, (LANES, tm), 0)
    packed = jnp.zeros((LANES, tm), F32)
    for j, r in enumerate(rows):
        packed = jnp.where(ri == j, r, packed)
    rrow_ref[...] = packed[0:8, :]
    rcol_ref[...] = packed.T


def _outproj(yc, ya, ys, x, wo, g, wr, br):
    n, d = x.shape
    tm = ROW_TILE
    row = lambda w: pl.BlockSpec((tm, w), lambda i: (i, 0))
    full = lambda a: pl.BlockSpec(a.shape, lambda i: (0, 0))
    return pl.pallas_call(
        functools.partial(_outproj_kernel, tm=tm),
        grid=(n // tm,),
        in_specs=[row(CONV_CH), row(ATTN_W), row(SSD_W), row(d), full(wo), full(g),
                  pl.BlockSpec(wr.shape, lambda i: (0, 0, 0)), full(br)],
        out_specs=[row(d), row(d), pl.BlockSpec((8, tm), lambda i: (0, i)), row(LANES)],
        out_shape=[jax.ShapeDtypeStruct((n, d), F32), jax.ShapeDtypeStruct((n, d), BF16),
                   jax.ShapeDtypeStruct((8, n), F32), jax.ShapeDtypeStruct((n, LANES), F32)],
        compiler_params=_params(("parallel",)),
        name="outproj_router",
    )(yc, ya, ys, x, wo, g, wr, br)


def _moe_kernel(cnt_ref, x1_ref, hf_ref, rrow_ref, rcol_ref, wg_ref, wu_ref, wd_ref, fg_ref, o_ref,
                xg_ref, acc_ref, cw_ref, rankr_ref, *, ts, spans, rows, cap, final_norm):
    i = pl.program_id(0)
    g = pl.program_id(1)
    e = pl.program_id(2)
    gf = g.astype(F32)

    def tok(sb):
        return slice(sb * ts, (sb + 1) * ts)

    def chunk(sb, c, r):
        if isinstance(c, int):
            return slice(sb * cap + c * r, sb * cap + (c + 1) * r)
        return pl.ds(pl.multiple_of(sb * cap + c * r, 16), r)

    def group_rank(sb):
        sub8 = lax.broadcasted_iota(jnp.int32, (8, ts), 0)
        return jnp.sum(jnp.where(sub8 == g, rankr_ref[:, tok(sb)], 0.0), axis=0, keepdims=True)

    def placement(sb, c, r):
        first = float(c * r) if isinstance(c, int) else (c * r).astype(F32)
        slot = lax.broadcasted_iota(jnp.int32, (r, ts), 0).astype(F32) + first
        return jnp.where((group_rank(sb) == slot) & (rrow_ref[0:1, tok(sb)] == gf), 1.0, 0.0).astype(BF16)

    def gather(sb, c, r):
        p = placement(sb, c, r)
        xg_ref[chunk(sb, c, r), :] = jnp.dot(p, hf_ref[tok(sb), :], preferred_element_type=F32).astype(BF16)
        cw_ref[chunk(sb, c, r), :] = jnp.dot(p, rcol_ref[tok(sb), :].astype(BF16), preferred_element_type=F32)
        acc_ref[chunk(sb, c, r), :] = jnp.zeros((r, acc_ref.shape[1]), F32)

    def expert(sb, c, r):
        lane_r = lax.broadcasted_iota(jnp.int32, (r, LANES), 1)
        pick = ((lane_r == 1 + e) | (lane_r == 1 + EXPERTS_PER_GROUP + e)
                | (lane_r == 1 + 2 * EXPERTS_PER_GROUP + e))
        xc = xg_ref[chunk(sb, c, r), :]
        hg = jnp.dot(xc, wg_ref[0], preferred_element_type=F32)
        hu = jnp.dot(xc, wu_ref[0], preferred_element_type=F32)
        hh = (hg * _sigmoid(hg) * hu).astype(BF16)
        y = jnp.dot(hh, wd_ref[0], preferred_element_type=F32)
        cwe = jnp.sum(jnp.where(pick, cw_ref[chunk(sb, c, r), :], 0.0), axis=-1, keepdims=True)
        acc_ref[chunk(sb, c, r), :] += cwe * y

    def scatter(sb, c, r):
        tn = (((0,), (0,)), ((), ()))
        o_ref[tok(sb), :] += lax.dot_general(placement(sb, c, r), acc_ref[chunk(sb, c, r), :].astype(BF16), tn,
                                             preferred_element_type=F32)

    def step(sb, c, r):
        @pl.when(e == 0)
        def _():
            gather(sb, c, r)

        expert(sb, c, r)

        @pl.when(e == EXPERTS_PER_GROUP - 1)
        def _():
            scatter(sb, c, r)

    @pl.when((g == 0) & (e == 0))
    def _():
        o_ref[...] = x1_ref[...]
        sub = lax.broadcasted_iota(jnp.int32, (8, ts), 0).astype(F32)
        for c0 in range(0, ts, RANK_STRIP):
            r_ = lax.broadcasted_iota(jnp.int32, (ts, RANK_STRIP), 0)
            c_ = lax.broadcasted_iota(jnp.int32, (ts, RANK_STRIP), 1) + c0
            before = jnp.where(r_ < c_, 1.0, 0.0).astype(BF16)
            for sb in range(spans):
                m_row = jnp.where(rrow_ref[0:1, tok(sb)] == sub, 1.0, 0.0).astype(BF16)
                rankr_ref[:, sb * ts + c0:sb * ts + c0 + RANK_STRIP] = jnp.dot(
                    m_row, before, preferred_element_type=F32)

    for sb in range(spans):
        cnt = cnt_ref[(i * spans + sb) * MOE_GROUPS + g]
        lo = 0
        for r in rows[:-1]:
            @pl.when((cnt > lo) & (cnt <= r))
            def _(sb=sb, r=r):
                step(sb, 0, r)
            lo = r

        @pl.when(cnt > lo)
        def _(sb=sb, cnt=cnt):
            r = rows[-1]

            def body(c, carry):
                step(sb, c, r)
                return carry

            lax.fori_loop(0, (cnt + r - 1) // r, body, 0)

    if final_norm:
        @pl.when((g == MOE_GROUPS - 1) & (e == EXPERTS_PER_GROUP - 1))
        def _():
            xo = o_ref[...]
            ms = jnp.mean(xo * xo, axis=-1, keepdims=True)
            o_ref[...] = xo * lax.rsqrt(ms + EPS) * fg_ref[...]


def _moe(counts, x1, hf, rrow, rcol, wg, wu, wd, fg, final_norm):
    n, d = x1.shape
    tb, ts, rows = MOE_BLOCK, MOE_SORT, MOE_ROWS
    spans = tb // ts
    de = wg.shape[2]
    cap = -(-ts // rows[-1]) * rows[-1]
    once = pl.Buffered(1)
    tokens = lambda w, **kw: pl.BlockSpec((tb, w), lambda i, g, e, c: (i, 0), **kw)
    expert = lambda shape: pl.BlockSpec(shape, lambda i, g, e, c: (g * EXPERTS_PER_GROUP + e, 0, 0))
    grid_spec = pltpu.PrefetchScalarGridSpec(
        num_scalar_prefetch=1,
        grid=(n // tb, MOE_GROUPS, EXPERTS_PER_GROUP),
        in_specs=[tokens(d, pipeline_mode=once), tokens(d, pipeline_mode=once),
                  pl.BlockSpec((8, tb), lambda i, g, e, c: (0, i)), tokens(LANES),
                  expert((1, d, de)), expert((1, d, de)), expert((1, de, d)),
                  pl.BlockSpec((1, d), lambda i, g, e, c: (0, 0))],
        out_specs=tokens(d),
        scratch_shapes=[pltpu.VMEM((spans * cap, d), BF16), pltpu.VMEM((spans * cap, d), F32),
                        pltpu.VMEM((spans * cap, LANES), F32), pltpu.VMEM((8, tb), F32)],
    )
    return pl.pallas_call(
        functools.partial(_moe_kernel, ts=ts, spans=spans, rows=rows, cap=cap, final_norm=final_norm),
        grid_spec=grid_spec,
        out_shape=jax.ShapeDtypeStruct((n, d), F32),
        compiler_params=_params(("parallel", "arbitrary", "arbitrary")),
        name="moe",
    )(counts, x1, hf, rrow, rcol, wg, wu, wd, fg)


def _pad_lanes(v, width=LANES):
    v = v.reshape(1, -1).astype(F32)
    return jnp.pad(v, ((0, 0), (0, width - v.shape[1])))


def kernel(x, norm_mix, w_in, conv_dw_w, conv_dw_b, conv_ln_g, conv_ln_b, fgate_b, ssd_conv_w, ssd_conv_b,
           ssd_dt_bias, ssd_A_log, ssd_D, ssd_norm_g, w_out, norm_ffn, router_group_w, router_group_b,
           router_expert_w, router_expert_b, expert_w_gate, expert_w_up, expert_w_down, norm_final):
    bsz, seq, d = x.shape
    n = bsz * seq
    depth = w_in.shape[0]
    heads = fgate_b.shape[1]
    ssd_heads = ssd_A_log.shape[1]
    n_exp = expert_w_gate.shape[1]
    assert heads * HEAD_DIM == ATTN_W and ssd_heads * HEAD_DIM == SSD_W
    assert n_exp == MOE_GROUPS * EXPERTS_PER_GROUP and conv_dw_w.shape[1] <= CONV_HALO + 1
    assert n % ROW_TILE == 0 and n % MOE_BLOCK == 0 and seq % max(CONV_TILE, CUM_TILE, 2 * ATTN_TQ, SSD_T) == 0

    sizes = (2 * CONV_CH, ATTN_W, ATTN_W, ATTN_W, heads, SSD_W, SSD_W, SSD_GROUPS * SSD_STATE,
             SSD_GROUPS * SSD_STATE, ssd_heads)
    offs = [0]
    for s in sizes:
        offs.append(offs[-1] + s)
    expand = jnp.repeat(jnp.eye(LANES, SSD_W // HEAD_DIM, dtype=BF16), HEAD_DIM, axis=1)

    xr = x.reshape(n, d)
    for l in range(depth):
        w = w_in[l]
        wa = jnp.concatenate([w[:, offs[0]:offs[4]], w[:, offs[5]:offs[9]]], axis=1).astype(BF16)
        zpad = jnp.zeros((d, LANES - heads), F32)
        ws = jnp.concatenate([w[:, offs[4]:offs[5]], zpad, w[:, offs[9]:offs[10]], zpad], axis=1).astype(BF16)
        glu, q, k, v, z, xbc, fs, ds = _inproj(xr, norm_mix[l].reshape(1, d), wa, ws)

        seq3 = lambda a: a.reshape(bsz, seq, a.shape[-1])
        y_conv = _conv(seq3(glu), conv_dw_w[l], conv_dw_b[l].reshape(1, -1), conv_ln_g[l].reshape(1, -1),
                       conv_ln_b[l].reshape(1, -1))
        cum = _cum(seq3(fs), _pad_lanes(fgate_b[l]))
        y_attn = _attention(seq3(q), seq3(k), seq3(v), cum)
        y_ssd = _ssd(seq3(xbc), seq3(z), seq3(ds), ssd_conv_w[l], ssd_conv_b[l].reshape(1, -1),
                     _pad_lanes(ssd_dt_bias[l]), _pad_lanes(ssd_A_log[l]),
                     jnp.repeat(ssd_D[l].astype(F32), HEAD_DIM).reshape(1, -1), ssd_norm_g[l].reshape(1, -1),
                     expand)

        wr = jnp.concatenate([router_group_w[l], router_expert_w[l],
                              jnp.zeros((d, LANES - MOE_GROUPS - n_exp), F32)], axis=1)
        wr_hi = wr.astype(BF16)
        wr = jnp.stack([wr_hi, (wr - wr_hi.astype(F32)).astype(BF16)])
        br = _pad_lanes(jnp.concatenate([router_group_b[l], router_expert_b[l]]))
        x1, hf, rrow, rcol = _outproj(y_conv.reshape(n, -1), y_attn.reshape(n, -1), y_ssd.reshape(n, -1), xr,
                                      w_out[l].astype(BF16), norm_ffn[l].reshape(1, d), wr, br)

        gidx = rrow[0].astype(jnp.int32).reshape(n // MOE_SORT, MOE_SORT)
        counts = jnp.sum(gidx[:, :, None] == jnp.arange(MOE_GROUPS, dtype=jnp.int32), axis=1,
                         dtype=jnp.int32).reshape(-1)
        xr = _moe(counts, x1, hf, rrow, rcol, expert_w_gate[l].astype(BF16), expert_w_up[l].astype(BF16),
                  expert_w_down[l].astype(BF16), norm_final.reshape(1, d), final_norm=(l == depth - 1))
    return xr.reshape(bsz, seq, d)
```

```python
import functools

import jax
import jax.numpy as jnp
from jax import lax
from jax.experimental import pallas as pl
from jax.experimental.pallas import tpu as pltpu

F32 = jnp.float32
BF16 = jnp.bfloat16
EPS = 1e-6
NEG_INF = float("-inf")

HEAD_DIM = 64
LANES = 128
CONV_CH = 512
ATTN_W = 512
SSD_W = 512
SSD_GROUPS = 2
SSD_STATE = 64
SSD_CONV_CH = SSD_W + 2 * SSD_GROUPS * SSD_STATE
MOE_GROUPS = 4
EXPERTS_PER_GROUP = 4

ROW_TILE = 512
CONV_TILE = 512
CONV_ROWS = 32
CONV_HALO = 32
SUBLANES = 8
CUM_TILE = 512
ATTN_TQ = 512
SSD_T = 256
SSD_CHUNKS_PER_STEP = 2
SSD_HALO = 8
MOE_BLOCK = 2048
MOE_SORT = 1024
MOE_ROWS = (256, 288, 320, 384)
RANK_STRIP = 256
VMEM_LIMIT = 60 * 1024 * 1024


def _params(sem):
    return pltpu.CompilerParams(dimension_semantics=sem, vmem_limit_bytes=VMEM_LIMIT)


def _sigmoid(x):
    return jax.nn.sigmoid(x)


def _softplus(x):
    return jnp.maximum(x, 0.0) + jnp.log1p(jnp.exp(-jnp.abs(x)))


def _split3(x):
    hi = x.astype(BF16)
    r = x - hi.astype(F32)
    mid = r.astype(BF16)
    lo = (r - mid.astype(F32)).astype(BF16)
    return hi, mid, lo


def _select_dot(sel, x):
    hi, mid, lo = _split3(x)
    d = lambda p: jnp.dot(sel, p, preferred_element_type=F32)
    return d(hi) + d(mid) + d(lo)


def _dot_select(x, sel):
    hi, mid, lo = _split3(x)
    d = lambda p: jnp.dot(p, sel, preferred_element_type=F32)
    return d(hi) + d(mid) + d(lo)


def _rows8(w_ref, k, rows):
    return jnp.concatenate([w_ref[k * SUBLANES:(k + 1) * SUBLANES, :]] * (rows // SUBLANES), axis=0)


def _inproj_kernel(x_ref, g_ref, wa_ref, ws_ref, glu_ref, q_ref, k_ref, v_ref, z_ref, xbc_ref,
                   fs_ref, ds_ref):
    xf = x_ref[...]
    ms = jnp.mean(xf * xf, axis=-1, keepdims=True)
    h = (xf * lax.rsqrt(ms + EPS) * g_ref[...]).astype(BF16)

    def mm(lo, hi):
        return jnp.dot(h, wa_ref[:, lo:hi], preferred_element_type=F32)

    glu_ref[...] = mm(0, 512) * _sigmoid(mm(512, 1024))
    q_ref[...] = (mm(1024, 1536) * (HEAD_DIM ** -0.5)).astype(BF16)
    k_ref[...] = mm(1536, 2048).astype(BF16)
    v_ref[...] = mm(2048, 2560).astype(BF16)
    z_ref[...] = mm(2560, 3072)
    xbc_ref[...] = mm(3072, 3840)
    small = jnp.dot(h, ws_ref[...], preferred_element_type=F32)
    fs_ref[...] = small[:, 0:LANES]
    ds_ref[...] = small[:, LANES:2 * LANES]


def _inproj(x, g, wa, ws):
    n, d = x.shape
    tm = ROW_TILE
    row = lambda w: pl.BlockSpec((tm, w), lambda i: (i, 0))
    full = lambda a: pl.BlockSpec(a.shape, lambda i: (0, 0))
    shapes = [(512, F32), (512, BF16), (512, BF16), (512, BF16), (512, F32), (SSD_CONV_CH, F32),
              (LANES, F32), (LANES, F32)]
    return pl.pallas_call(
        _inproj_kernel,
        grid=(n // tm,),
        in_specs=[row(d), full(g), full(wa), full(ws)],
        out_specs=[row(w) for w, _ in shapes],
        out_shape=[jax.ShapeDtypeStruct((n, w), dt) for w, dt in shapes],
        compiler_params=_params(("parallel",)),
        name="inproj",
    )(x, g, wa, ws)


def _conv_kernel(x_ref, w_ref, b_ref, lg_ref, lb_ref, o_ref, buf_ref, *, taps, tl):
    li = pl.program_id(1)

    @pl.when(li == 0)
    def _():
        buf_ref[0:CONV_HALO, :] = jnp.zeros((CONV_HALO, CONV_CH), F32)

    @pl.when(li > 0)
    def _():
        buf_ref[0:CONV_HALO, :] = buf_ref[tl:tl + CONV_HALO, :]

    buf_ref[CONV_HALO:CONV_HALO + tl, :] = x_ref[0]
    first = CONV_HALO - (taps - 1)
    by_shift = [[k for k in range(taps) if (first + k) % SUBLANES == s] for s in range(SUBLANES)]
    for r in range(0, tl, CONV_ROWS):
        acc = jnp.broadcast_to(b_ref[...], (CONV_ROWS, CONV_CH))
        for s, ks in enumerate(by_shift):
            if not ks:
                continue
            span = CONV_ROWS + (SUBLANES if s else 0)
            z = None
            for k in ks:
                base = r + first + k - s
                term = _rows8(w_ref, k, span) * buf_ref[base:base + span, :]
                z = term if z is None else z + term
            acc = acc + z[s:s + CONV_ROWS, :]
        mu = jnp.mean(acc, axis=-1, keepdims=True)
        cen = acc - mu
        var = jnp.mean(cen * cen, axis=-1, keepdims=True)
        y = cen * lax.rsqrt(var + EPS) * lg_ref[...] + lb_ref[...]
        o_ref[0, r:r + CONV_ROWS, :] = (y * _sigmoid(y)).astype(BF16)


def _conv(glu, w, b, lg, lb):
    bsz, seq, c = glu.shape
    tl = CONV_TILE
    taps = w.shape[0]
    w = jnp.repeat(w, SUBLANES, axis=0)
    full = lambda a: pl.BlockSpec(a.shape, lambda bi, li: (0, 0))
    return pl.pallas_call(
        functools.partial(_conv_kernel, taps=taps, tl=tl),
        grid=(bsz, seq // tl),
        in_specs=[pl.BlockSpec((1, tl, c), lambda bi, li: (bi, li, 0)), full(w), full(b), full(lg), full(lb)],
        out_specs=pl.BlockSpec((1, tl, c), lambda bi, li: (bi, li, 0)),
        out_shape=jax.ShapeDtypeStruct((bsz, seq, c), BF16),
        scratch_shapes=[pltpu.VMEM((tl + CONV_HALO, c), F32)],
        compiler_params=_params(("parallel", "arbitrary")),
        name="dwconv",
    )(glu, w, b, lg, lb)


def _lower_tri(n, dtype, strict=False):
    r = lax.broadcasted_iota(jnp.int32, (n, n), 0)
    c = lax.broadcasted_iota(jnp.int32, (n, n), 1)
    keep = (c < r) if strict else (c <= r)
    return jnp.where(keep, 1.0, 0.0).astype(dtype)


def _cum_kernel(f_ref, b_ref, o_ref, carry_ref, *, tl):
    li = pl.program_id(1)

    @pl.when(li == 0)
    def _():
        carry_ref[...] = jnp.zeros_like(carry_ref)

    x = f_ref[0] + b_ref[...]
    lf = jnp.minimum(x, 0.0) - jnp.log1p(jnp.exp(-jnp.abs(x)))
    cum = _select_dot(_lower_tri(tl, BF16), lf) + carry_ref[...]
    carry_ref[...] = cum[tl - 1:tl, :]
    o_ref[0] = cum.T[0:8, :]


def _cum(fs, fb):
    bsz, seq, _ = fs.shape
    tl = CUM_TILE
    return pl.pallas_call(
        functools.partial(_cum_kernel, tl=tl),
        grid=(bsz, seq // tl),
        in_specs=[pl.BlockSpec((1, tl, LANES), lambda bi, li: (bi, li, 0)),
                  pl.BlockSpec((1, LANES), lambda bi, li: (0, 0))],
        out_specs=pl.BlockSpec((1, 8, tl), lambda bi, li: (bi, 0, li)),
        out_shape=jax.ShapeDtypeStruct((bsz, 8, seq), F32),
        scratch_shapes=[pltpu.VMEM((1, LANES), F32)],
        compiler_params=_params(("parallel", "arbitrary")),
        name="fgate_cumsum",
    )(fs, fb)


def _attn_kernel(q_ref, qn_ref, k_ref, v_ref, c_ref, o_ref, qm_ref, s_ref, m_ref, l_ref, acc_ref, *, tq, nq):
    hp = pl.program_id(1)
    qi = pl.program_id(2)
    slot = qi % 2
    nslot = 1 - slot
    lane = lax.broadcasted_iota(jnp.int32, (tq, LANES), 1)
    row = lax.broadcasted_iota(jnp.int32, (tq, tq), 0)
    col = lax.broadcasted_iota(jnp.int32, (tq, tq), 1)
    tiles = tq // LANES

    def start_tile(sl, q):
        for hh in range(2):
            in_head = (lane >= HEAD_DIM) if hh else (lane < HEAD_DIM)
            qm_ref[hh] = jnp.where(in_head, q, jnp.zeros_like(q))
            m_ref[sl, hh] = jnp.full((tq, LANES), NEG_INF, F32)

    def scores(sl, j, masked):
        k0 = pl.multiple_of(j * tq, tq)
        kb = k_ref[0, pl.ds(k0, tq), :]
        for hh in range(2):
            cb = c_ref[0, pl.ds(hp * 2 + hh, 1), pl.ds(k0, tq)]
            s = lax.dot_general(qm_ref[hh], kb, (((1,), (1,)), ((), ())), preferred_element_type=F32) - cb
            if masked:
                s = jnp.where(col <= row, s, NEG_INF)
            s_ref[sl, hh, j] = s
            mt = m_ref[sl, hh]
            for t in range(tiles):
                mt = jnp.maximum(mt, s[:, t * LANES:(t + 1) * LANES])
            m_ref[sl, hh] = mt

    def finish_tile(sl, diag):
        scores(sl, diag, True)
        for hh in range(2):
            m_ref[sl, hh] = jnp.broadcast_to(jnp.max(m_ref[sl, hh], axis=-1, keepdims=True), (tq, LANES))

    def weighted(j):
        k0 = pl.multiple_of(j * tq, tq)
        vb = v_ref[0, pl.ds(k0, tq), :]
        for hh in range(2):
            mb = m_ref[slot, hh]
            lt = l_ref[hh]
            ps = []
            for t in range(tiles):
                p = jnp.exp(s_ref[slot, hh, j, :, t * LANES:(t + 1) * LANES] - mb)
                lt = lt + p
                ps.append(p.astype(BF16))
            l_ref[hh] = lt
            acc_ref[hh] += jnp.dot(jnp.concatenate(ps, axis=1), vb, preferred_element_type=F32)

    @pl.when(qi == 0)
    def _():
        start_tile(0, q_ref[0])
        finish_tile(0, 0)

    for hh in range(2):
        l_ref[hh] = jnp.zeros((tq, LANES), F32)
        acc_ref[hh] = jnp.zeros((tq, LANES), F32)

    @pl.when(qi + 1 < nq)
    def _():
        start_tile(nslot, qn_ref[0])

        def pair(jj, carry):
            for d in range(2):
                weighted(2 * jj + d)
                scores(nslot, 2 * jj + d, False)
            return carry

        lax.fori_loop(0, (qi + 1) // 2, pair, 0)

        @pl.when((qi + 1) % 2 == 1)
        def _():
            weighted(qi)
            scores(nslot, qi, False)

        finish_tile(nslot, qi + 1)

    @pl.when(qi + 1 == nq)
    def _():
        def last(jj, carry):
            for d in range(2):
                weighted(2 * jj + d)
            return carry

        lax.fori_loop(0, nq // 2, last, 0)

    outs = [acc_ref[hh] / jnp.sum(l_ref[hh], axis=-1, keepdims=True) for hh in range(2)]
    o_ref[0] = jnp.where(lane < HEAD_DIM, outs[0], outs[1]).astype(BF16)


def _attention(q, k, v, cum):
    bsz, seq, w = q.shape
    tq = ATTN_TQ
    pairs = w // LANES
    nq = seq // tq
    return pl.pallas_call(
        functools.partial(_attn_kernel, tq=tq, nq=nq),
        grid=(bsz, pairs, nq),
        in_specs=[pl.BlockSpec((1, tq, LANES), lambda b, h, i: (b, i, h)),
                  pl.BlockSpec((1, tq, LANES), lambda b, h, i: (b, jnp.minimum(i + 1, nq - 1), h)),
                  pl.BlockSpec((1, seq, LANES), lambda b, h, i: (b, 0, h)),
                  pl.BlockSpec((1, seq, LANES), lambda b, h, i: (b, 0, h)),
                  pl.BlockSpec((1, 8, seq), lambda b, h, i: (b, 0, 0))],
        out_specs=pl.BlockSpec((1, tq, LANES), lambda b, h, i: (b, i, h)),
        out_shape=jax.ShapeDtypeStruct((bsz, seq, w), BF16),
        scratch_shapes=[pltpu.VMEM((2, tq, LANES), BF16), pltpu.VMEM((2, 2, nq, tq, tq), F32),
                        pltpu.VMEM((2, 2, tq, LANES), F32), pltpu.VMEM((2, tq, LANES), F32),
                        pltpu.VMEM((2, tq, LANES), F32)],
        compiler_params=_params(("parallel", "parallel", "arbitrary")),
        name="fox_attention",
    )(q, q, k, v, cum)


def _ssd_kernel(xbc_ref, z_ref, dt_ref, cw_ref, cb_ref, dtb_ref, alog_ref, dw_ref, ng_ref, ex_ref, o_ref,
                buf_ref, h_ref, *, T, taps, chunks):
    ci = pl.program_id(1)
    span = T * chunks

    @pl.when(ci == 0)
    def _():
        buf_ref[0:SSD_HALO, :] = jnp.zeros((SSD_HALO, SSD_CONV_CH), F32)
        h_ref[...] = jnp.zeros_like(h_ref)

    @pl.when(ci > 0)
    def _():
        buf_ref[0:SSD_HALO, :] = buf_ref[span:span + SSD_HALO, :]

    buf_ref[SSD_HALO:SSD_HALO + span, :] = xbc_ref[0]
    for c in range(chunks):
        rows = pl.ds(c * T, T)
        _ssd_chunk(c * T, z_ref.at[0, rows], dt_ref.at[0, rows], cw_ref, cb_ref, dtb_ref, alog_ref, dw_ref,
                   ng_ref, ex_ref, o_ref.at[0, rows], buf_ref, h_ref, T=T, taps=taps)


def _ssd_chunk(base, z_ref, dt_ref, cw_ref, cb_ref, dtb_ref, alog_ref, dw_ref, ng_ref, ex_ref, o_ref,
               buf_ref, h_ref, *, T, taps):
    gw = SSD_W // SSD_GROUPS
    first = base + SSD_HALO - (taps - 1)
    acc = jnp.broadcast_to(cb_ref[...], (T, SSD_CONV_CH))
    for k in range(taps):
        acc = acc + _rows8(cw_ref, k, T) * buf_ref[first + k:first + k + T, :]
    xc = acc * _sigmoid(acc)
    xs = xc[:, 0:SSD_W]
    b_mat = xc[:, SSD_W:SSD_W + LANES]
    c_mat = xc[:, SSD_W + LANES:SSD_W + 2 * LANES]

    dt = _softplus(dt_ref[...] + dtb_ref[...])
    a = dt * (-jnp.exp(alog_ref[...]))
    acum = _select_dot(_lower_tri(T, BF16), a)
    acum_row = acum.T
    expand = ex_ref[...]
    dt_w = _dot_select(dt, expand)
    acum_w = _dot_select(acum, expand)
    last_w = acum_w[T - 1:T, :]
    xdt = xs * dt_w
    xdt_b = xdt.astype(BF16)
    xdec_b = (xdt * jnp.exp(last_w - acum_w)).astype(BF16)
    eacum_w = jnp.exp(acum_w)
    chunk_decay = jnp.exp(last_w)
    bb = b_mat.astype(BF16)
    cc = c_mat.astype(BF16)
    bt = b_mat.T.astype(BF16)

    row = lax.broadcasted_iota(jnp.int32, (T, T), 0)
    col = lax.broadcasted_iota(jnp.int32, (T, T), 1)
    causal = col <= row
    lane = lax.broadcasted_iota(jnp.int32, (T, LANES), 1)
    h_in = h_ref[...]
    h_ref[...] = h_in * chunk_decay + jnp.dot(bt, xdec_b, preferred_element_type=F32)
    h_in_b = h_in.astype(BF16)
    pieces = []
    for g in range(SSD_GROUPS):
        in_group = (lane >= g * SSD_STATE) & (lane < (g + 1) * SSD_STATE)
        cg = jnp.where(in_group, cc, jnp.zeros_like(cc))
        cbm = lax.dot_general(cg, bb, (((1,), (1,)), ((), ())), preferred_element_type=F32)
        yoff = jnp.dot(cg, h_in_b[:, g * gw:(g + 1) * gw], preferred_element_type=F32)
        for pr in range(gw // LANES):
            lo = g * gw + pr * LANES
            xpair = xdt_b[:, lo:lo + LANES]
            res = []
            for hh in range(2):
                head = lo // HEAD_DIM + hh
                seg = acum[:, head:head + 1] - acum_row[head:head + 1, :]
                lmat = jnp.exp(jnp.where(causal, seg, NEG_INF))
                res.append(jnp.dot((cbm * lmat).astype(BF16), xpair, preferred_element_type=F32))
            ydiag = jnp.where(lane < HEAD_DIM, res[0], res[1])
            pieces.append(ydiag + yoff[:, pr * LANES:(pr + 1) * LANES] * eacum_w[:, lo:lo + LANES])
    y = jnp.concatenate(pieces, axis=1) + xs * dw_ref[...]
    zz = z_ref[...]
    gated = y * (zz * _sigmoid(zz))
    ms = jnp.mean(gated * gated, axis=-1, keepdims=True)
    o_ref[...] = (gated * lax.rsqrt(ms + EPS) * ng_ref[...]).astype(BF16)


def _ssd(xbc, z, ds, cw, cb, dtb, alog, dwide, ng, expand):
    bsz, seq, _ = xbc.shape
    T = SSD_T
    taps = cw.shape[0]
    cw = jnp.repeat(cw, SUBLANES, axis=0)
    full = lambda a: pl.BlockSpec(a.shape, lambda bi, ci: (0, 0))
    span = T * SSD_CHUNKS_PER_STEP
    tok = lambda w: pl.BlockSpec((1, span, w), lambda bi, ci: (bi, ci, 0))
    return pl.pallas_call(
        functools.partial(_ssd_kernel, T=T, taps=taps, chunks=SSD_CHUNKS_PER_STEP),
        grid=(bsz, seq // span),
        in_specs=[tok(SSD_CONV_CH), tok(SSD_W), tok(LANES), full(cw), full(cb), full(dtb), full(alog),
                  full(dwide), full(ng), full(expand)],
        out_specs=tok(SSD_W),
        out_shape=jax.ShapeDtypeStruct((bsz, seq, SSD_W), BF16),
        scratch_shapes=[pltpu.VMEM((span + SSD_HALO, SSD_CONV_CH), F32),
                        pltpu.VMEM((SSD_GROUPS * SSD_STATE, SSD_W), F32)],
        compiler_params=_params(("parallel", "arbitrary")),
        name="ssd_mixer",
    )(xbc, z, ds, cw, cb, dtb, alog, dwide, ng, expand)


def _first_max(vals):
    best = vals[0]
    for v in vals[1:]:
        best = jnp.maximum(best, v)
    idx = jnp.full(best.shape, float(len(vals) - 1), F32)
    for j in range(len(vals) - 2, -1, -1):
        idx = jnp.where(vals[j] == best, float(j), idx)
    return best, idx


def _outproj_kernel(yc_ref, ya_ref, ys_ref, x_ref, wo_ref, g_ref, wr_ref, br_ref,
                    x1_ref, hf_ref, rrow_ref, rcol_ref, *, tm):
    y = jnp.dot(yc_ref[...], wo_ref[0:CONV_CH, :], preferred_element_type=F32)
    y = y + jnp.dot(ya_ref[...], wo_ref[CONV_CH:CONV_CH + ATTN_W, :], preferred_element_type=F32)
    y = y + jnp.dot(ys_ref[...], wo_ref[CONV_CH + ATTN_W:, :], preferred_element_type=F32)
    x1 = x_ref[...] + y
    x1_ref[...] = x1
    ms = jnp.mean(x1 * x1, axis=-1, keepdims=True)
    hf = x1 * lax.rsqrt(ms + EPS) * g_ref[...]
    hf_hi = hf.astype(BF16)
    hf_ref[...] = hf_hi
    hf_lo = (hf - hf_hi.astype(F32)).astype(BF16)
    logits = (jnp.dot(hf_hi, wr_ref[0], preferred_element_type=F32)
              + jnp.dot(hf_hi, wr_ref[1], preferred_element_type=F32)
              + jnp.dot(hf_lo, wr_ref[0], preferred_element_type=F32)) + br_ref[...]
    lt = logits.T
    gl = [lt[j:j + 1, :] for j in range(MOE_GROUPS)]
    gmax, gidx = _first_max(gl)
    denom = gl[0] * 0.0
    for v in gl:
        denom = denom + jnp.exp(v - gmax)
    gval = 1.0 / denom
    esel = []
    for j in range(EXPERTS_PER_GROUP):
        erow = lambda g: lt[MOE_GROUPS + g * EXPERTS_PER_GROUP + j:MOE_GROUPS + g * EXPERTS_PER_GROUP + j + 1, :]
        v = erow(MOE_GROUPS - 1)
        for g in range(MOE_GROUPS - 2, -1, -1):
            v = jnp.where(gidx == float(g), erow(g), v)
        esel.append(v)
    v1, i1 = _first_max(esel)
    rest = [jnp.where(i1 == float(j), NEG_INF, esel[j]) for j in range(EXPERTS_PER_GROUP)]
    v2, i2 = _first_max(rest)
    e2 = jnp.exp(v2 - v1)
    w1 = (1.0 / (1.0 + e2)) * gval
    w2 = (e2 / (1.0 + e2)) * gval
    cw = [jnp.where(i1 == float(j), w1, jnp.where(i2 == float(j), w2, 0.0)) for j in range(EXPERTS_PER_GROUP)]
    hi = [c.astype(BF16).astype(F32) for c in cw]
    mid = [(c - h).astype(BF16).astype(F32) for c, h in zip(cw, hi)]
    lo = [(c - h - m).astype(BF16).astype(F32) for c, h, m in zip(cw, hi, mid)]
    rows = [gidx] + hi + mid + lo
    ri = lax.broadcasted_iota(jnp.int32, (LANES, tm), 0)
    packed = jnp.zeros((LANES, tm), F32)
    for j, r in enumerate(rows):
        packed = jnp.where(ri == j, r, packed)
    rrow_ref[...] = packed[0:8, :]
    rcol_ref[...] = packed.T


def _outproj(yc, ya, ys, x, wo, g, wr, br):
    n, d = x.shape
    tm = ROW_TILE
    row = lambda w: pl.BlockSpec((tm, w), lambda i: (i, 0))
    full = lambda a: pl.BlockSpec(a.shape, lambda i: (0, 0))
    return pl.pallas_call(
        functools.partial(_outproj_kernel, tm=tm),
        grid=(n // tm,),
        in_specs=[row(CONV_CH), row(ATTN_W), row(SSD_W), row(d), full(wo), full(g),
                  pl.BlockSpec(wr.shape, lambda i: (0, 0, 0)), full(br)],
        out_specs=[row(d), row(d), pl.BlockSpec((8, tm), lambda i: (0, i)), row(LANES)],
        out_shape=[jax.ShapeDtypeStruct((n, d), F32), jax.ShapeDtypeStruct((n, d), BF16),
                   jax.ShapeDtypeStruct((8, n), F32), jax.ShapeDtypeStruct((n, LANES), F32)],
        compiler_params=_params(("parallel",)),
        name="outproj_router",
    )(yc, ya, ys, x, wo, g, wr, br)


def _moe_kernel(cnt_ref, x1_ref, hf_ref, rrow_ref, rcol_ref, wg_ref, wu_ref, wd_ref, fg_ref, o_ref,
                xg_ref, acc_ref, cw_ref, rankr_ref, *, ts, spans, rows, cap, final_norm):
    i = pl.program_id(0)
    g = pl.program_id(1)
    e = pl.program_id(2)
    gf = g.astype(F32)

    def tok(sb):
        return slice(sb * ts, (sb + 1) * ts)

    def chunk(sb, c, r):
        if isinstance(c, int):
            return slice(sb * cap + c * r, sb * cap + (c + 1) * r)
        return pl.ds(pl.multiple_of(sb * cap + c * r, 16), r)

    def group_rank(sb):
        sub8 = lax.broadcasted_iota(jnp.int32, (8, ts), 0)
        return jnp.sum(jnp.where(sub8 == g, rankr_ref[:, tok(sb)], 0.0), axis=0, keepdims=True)

    def placement(sb, c, r):
        first = float(c * r) if isinstance(c, int) else (c * r).astype(F32)
        slot = lax.broadcasted_iota(jnp.int32, (r, ts), 0).astype(F32) + first
        return jnp.where((group_rank(sb) == slot) & (rrow_ref[0:1, tok(sb)] == gf), 1.0, 0.0).astype(BF16)

    def gather(sb, c, r):
        p = placement(sb, c, r)
        xg_ref[chunk(sb, c, r), :] = jnp.dot(p, hf_ref[tok(sb), :], preferred_element_type=F32).astype(BF16)
        cw_ref[chunk(sb, c, r), :] = jnp.dot(p, rcol_ref[tok(sb), :].astype(BF16), preferred_element_type=F32)
        acc_ref[chunk(sb, c, r), :] = jnp.zeros((r, acc_ref.shape[1]), F32)

    def expert(sb, c, r):
        lane_r = lax.broadcasted_iota(jnp.int32, (r, LANES), 1)
        pick = ((lane_r == 1 + e) | (lane_r == 1 + EXPERTS_PER_GROUP + e)
                | (lane_r == 1 + 2 * EXPERTS_PER_GROUP + e))
        xc = xg_ref[chunk(sb, c, r), :]
        hg = jnp.dot(xc, wg_ref[0], preferred_element_type=F32)
        hu = jnp.dot(xc, wu_ref[0], preferred_element_type=F32)
        hh = (hg * _sigmoid(hg) * hu).astype(BF16)
        y = jnp.dot(hh, wd_ref[0], preferred_element_type=F32)
        cwe = jnp.sum(jnp.where(pick, cw_ref[chunk(sb, c, r), :], 0.0), axis=-1, keepdims=True)
        acc_ref[chunk(sb, c, r), :] += cwe * y

    def scatter(sb, c, r):
        tn = (((0,), (0,)), ((), ()))
        o_ref[tok(sb), :] += lax.dot_general(placement(sb, c, r), acc_ref[chunk(sb, c, r), :].astype(BF16), tn,
                                             preferred_element_type=F32)

    def step(sb, c, r):
        @pl.when(e == 0)
        def _():
            gather(sb, c, r)

        expert(sb, c, r)

        @pl.when(e == EXPERTS_PER_GROUP - 1)
        def _():
            scatter(sb, c, r)

    @pl.when((g == 0) & (e == 0))
    def _():
        o_ref[...] = x1_ref[...]
        sub = lax.broadcasted_iota(jnp.int32, (8, ts), 0).astype(F32)
        for c0 in range(0, ts, RANK_STRIP):
            r_ = lax.broadcasted_iota(jnp.int32, (ts, RANK_STRIP), 0)
            c_ = lax.broadcasted_iota(jnp.int32, (ts, RANK_STRIP), 1) + c0
            before = jnp.where(r_ < c_, 1.0, 0.0).astype(BF16)
            for sb in range(spans):
                m_row = jnp.where(rrow_ref[0:1, tok(sb)] == sub, 1.0, 0.0).astype(BF16)
                rankr_ref[:, sb * ts + c0:sb * ts + c0 + RANK_STRIP] = jnp.dot(
                    m_row, before, preferred_element_type=F32)

    for sb in range(spans):
        cnt = cnt_ref[(i * spans + sb) * MOE_GROUPS + g]
        lo = 0
        for r in rows[:-1]:
            @pl.when((cnt > lo) & (cnt <= r))
            def _(sb=sb, r=r):
                step(sb, 0, r)
            lo = r

        @pl.when(cnt > lo)
        def _(sb=sb, cnt=cnt):
            r = rows[-1]

            def body(c, carry):
                step(sb, c, r)
                return carry

            lax.fori_loop(0, (cnt + r - 1) // r, body, 0)

    if final_norm:
        @pl.when((g == MOE_GROUPS - 1) & (e == EXPERTS_PER_GROUP - 1))
        def _():
            xo = o_ref[...]
            ms = jnp.mean(xo * xo, axis=-1, keepdims=True)
            o_ref[...] = xo * lax.rsqrt(ms + EPS) * fg_ref[...]


def _moe(counts, x1, hf, rrow, rcol, wg, wu, wd, fg, final_norm):
    n, d = x1.shape
    tb, ts, rows = MOE_BLOCK, MOE_SORT, MOE_ROWS
    spans = tb // ts
    de = wg.shape[2]
    cap = -(-ts // rows[-1]) * rows[-1]
    once = pl.Buffered(1)
    tokens = lambda w, **kw: pl.BlockSpec((tb, w), lambda i, g, e, c: (i, 0), **kw)
    expert = lambda shape: pl.BlockSpec(shape, lambda i, g, e, c: (g * EXPERTS_PER_GROUP + e, 0, 0))
    grid_spec = pltpu.PrefetchScalarGridSpec(
        num_scalar_prefetch=1,
        grid=(n // tb, MOE_GROUPS, EXPERTS_PER_GROUP),
        in_specs=[tokens(d, pipeline_mode=once), tokens(d),
                  pl.BlockSpec((8, tb), lambda i, g, e, c: (0, i)), tokens(LANES),
                  expert((1, d, de)), expert((1, d, de)), expert((1, de, d)),
                  pl.BlockSpec((1, d), lambda i, g, e, c: (0, 0))],
        out_specs=tokens(d),
        scratch_shapes=[pltpu.VMEM((spans * cap, d), BF16), pltpu.VMEM((spans * cap, d), F32),
                        pltpu.VMEM((spans * cap, LANES), F32), pltpu.VMEM((8, tb), F32)],
    )
    return pl.pallas_call(
        functools.partial(_moe_kernel, ts=ts, spans=spans, rows=rows, cap=cap, final_norm=final_norm),
        grid_spec=grid_spec,
        out_shape=jax.ShapeDtypeStruct((n, d), F32),
        compiler_params=_params(("parallel", "arbitrary", "arbitrary")),
        name="moe",
    )(counts, x1, hf, rrow, rcol, wg, wu, wd, fg)


def _pad_lanes(v, width=LANES):
    v = v.reshape(1, -1).astype(F32)
    return jnp.pad(v, ((0, 0), (0, width - v.shape[1])))


def kernel(x, norm_mix, w_in, conv_dw_w, conv_dw_b, conv_ln_g, conv_ln_b, fgate_b, ssd_conv_w, ssd_conv_b,
           ssd_dt_bias, ssd_A_log, ssd_D, ssd_norm_g, w_out, norm_ffn, router_group_w, router_group_b,
           router_expert_w, router_expert_b, expert_w_gate, expert_w_up, expert_w_down, norm_final):
    bsz, seq, d = x.shape
    n = bsz * seq
    depth = w_in.shape[0]
    heads = fgate_b.shape[1]
    ssd_heads = ssd_A_log.shape[1]
    n_exp = expert_w_gate.shape[1]
    assert heads * HEAD_DIM == ATTN_W and ssd_heads * HEAD_DIM == SSD_W
    assert n_exp == MOE_GROUPS * EXPERTS_PER_GROUP and conv_dw_w.shape[1] <= CONV_HALO + 1
    assert n % ROW_TILE == 0 and n % MOE_BLOCK == 0 and seq % max(CONV_TILE, CUM_TILE, 2 * ATTN_TQ, SSD_T) == 0

    sizes = (2 * CONV_CH, ATTN_W, ATTN_W, ATTN_W, heads, SSD_W, SSD_W, SSD_GROUPS * SSD_STATE,
             SSD_GROUPS * SSD_STATE, ssd_heads)
    offs = [0]
    for s in sizes:
        offs.append(offs[-1] + s)
    expand = jnp.repeat(jnp.eye(LANES, SSD_W // HEAD_DIM, dtype=BF16), HEAD_DIM, axis=1)

    xr = x.reshape(n, d)
    for l in range(depth):
        w = w_in[l]
        wa = jnp.concatenate([w[:, offs[0]:offs[4]], w[:, offs[5]:offs[9]]], axis=1).astype(BF16)
        zpad = jnp.zeros((d, LANES - heads), F32)
        ws = jnp.concatenate([w[:, offs[4]:offs[5]], zpad, w[:, offs[9]:offs[10]], zpad], axis=1).astype(BF16)
        glu, q, k, v, z, xbc, fs, ds = _inproj(xr, norm_mix[l].reshape(1, d), wa, ws)

        seq3 = lambda a: a.reshape(bsz, seq, a.shape[-1])
        y_conv = _conv(seq3(glu), conv_dw_w[l], conv_dw_b[l].reshape(1, -1), conv_ln_g[l].reshape(1, -1),
                       conv_ln_b[l].reshape(1, -1))
        cum = _cum(seq3(fs), _pad_lanes(fgate_b[l]))
        y_attn = _attention(seq3(q), seq3(k), seq3(v), cum)
        y_ssd = _ssd(seq3(xbc), seq3(z), seq3(ds), ssd_conv_w[l], ssd_conv_b[l].reshape(1, -1),
                     _pad_lanes(ssd_dt_bias[l]), _pad_lanes(ssd_A_log[l]),
                     jnp.repeat(ssd_D[l].astype(F32), HEAD_DIM).reshape(1, -1), ssd_norm_g[l].reshape(1, -1),
                     expand)

        wr = jnp.concatenate([router_group_w[l], router_expert_w[l],
                              jnp.zeros((d, LANES - MOE_GROUPS - n_exp), F32)], axis=1)
        wr_hi = wr.astype(BF16)
        wr = jnp.stack([wr_hi, (wr - wr_hi.astype(F32)).astype(BF16)])
        br = _pad_lanes(jnp.concatenate([router_group_b[l], router_expert_b[l]]))
        x1, hf, rrow, rcol = _outproj(y_conv.reshape(n, -1), y_attn.reshape(n, -1), y_ssd.reshape(n, -1), xr,
                                      w_out[l].astype(BF16), norm_ffn[l].reshape(1, d), wr, br)

        gidx = rrow[0].astype(jnp.int32).reshape(n // MOE_SORT, MOE_SORT)
        counts = jnp.sum(gidx[:, :, None] == jnp.arange(MOE_GROUPS, dtype=jnp.int32), axis=1,
                         dtype=jnp.int32).reshape(-1)
        xr = _moe(counts, x1, hf, rrow, rcol, expert_w_gate[l].astype(BF16), expert_w_up[l].astype(BF16),
                  expert_w_down[l].astype(BF16), norm_final.reshape(1, d), final_norm=(l == depth - 1))
    return xr.reshape(bsz, seq, d)
```

```python
import functools

import jax
import jax.numpy as jnp
from jax import lax
from jax.experimental import pallas as pl
from jax.experimental.pallas import tpu as pltpu

F32 = jnp.float32
BF16 = jnp.bfloat16
EPS = 1e-6
NEG_INF = float("-inf")

HEAD_DIM = 64
LANES = 128
CONV_CH = 512
ATTN_W = 512
SSD_W = 512
SSD_GROUPS = 2
SSD_STATE = 64
SSD_CONV_CH = SSD_W + 2 * SSD_GROUPS * SSD_STATE
MOE_GROUPS = 4
EXPERTS_PER_GROUP = 4

ROW_TILE = 1024
CONV_TILE = 512
CONV_ROWS = 32
CONV_HALO = 32
SUBLANES = 8
CUM_TILE = 512
ATTN_TQ = 512
SSD_T = 256
SSD_CHUNKS_PER_STEP = 2
SSD_HALO = 8
MOE_BLOCK = 2048
MOE_SORT = 1024
MOE_ROWS = (256, 288, 320, 384)
RANK_STRIP = 256
VMEM_LIMIT = 60 * 1024 * 1024


def _params(sem):
    return pltpu.CompilerParams(dimension_semantics=sem, vmem_limit_bytes=VMEM_LIMIT)


def _sigmoid(x):
    return jax.nn.sigmoid(x)


def _softplus(x):
    return jnp.maximum(x, 0.0) + jnp.log1p(jnp.exp(-jnp.abs(x)))


def _split3(x):
    hi = x.astype(BF16)
    r = x - hi.astype(F32)
    mid = r.astype(BF16)
    lo = (r - mid.astype(F32)).astype(BF16)
    return hi, mid, lo


def _select_dot(sel, x):
    hi, mid, lo = _split3(x)
    d = lambda p: jnp.dot(sel, p, preferred_element_type=F32)
    return d(hi) + d(mid) + d(lo)


def _dot_select(x, sel):
    hi, mid, lo = _split3(x)
    d = lambda p: jnp.dot(p, sel, preferred_element_type=F32)
    return d(hi) + d(mid) + d(lo)


def _rows8(w_ref, k, rows):
    return jnp.concatenate([w_ref[k * SUBLANES:(k + 1) * SUBLANES, :]] * (rows // SUBLANES), axis=0)


def _inproj_kernel(x_ref, g_ref, wa_ref, ws_ref, glu_ref, q_ref, k_ref, v_ref, z_ref, xbc_ref,
                   fs_ref, ds_ref):
    xf = x_ref[...]
    ms = jnp.mean(xf * xf, axis=-1, keepdims=True)
    h = (xf * lax.rsqrt(ms + EPS) * g_ref[...]).astype(BF16)

    def mm(lo, hi):
        return jnp.dot(h, wa_ref[:, lo:hi], preferred_element_type=F32)

    glu_ref[...] = mm(0, 512) * _sigmoid(mm(512, 1024))
    q_ref[...] = (mm(1024, 1536) * (HEAD_DIM ** -0.5)).astype(BF16)
    k_ref[...] = mm(1536, 2048).astype(BF16)
    v_ref[...] = mm(2048, 2560).astype(BF16)
    z_ref[...] = mm(2560, 3072)
    xbc_ref[...] = mm(3072, 3840)
    small = jnp.dot(h, ws_ref[...], preferred_element_type=F32)
    fs_ref[...] = small[:, 0:LANES]
    ds_ref[...] = small[:, LANES:2 * LANES]


def _inproj(x, g, wa, ws):
    n, d = x.shape
    tm = ROW_TILE
    row = lambda w: pl.BlockSpec((tm, w), lambda i: (i, 0))
    full = lambda a: pl.BlockSpec(a.shape, lambda i: (0, 0))
    shapes = [(512, F32), (512, BF16), (512, BF16), (512, BF16), (512, F32), (SSD_CONV_CH, F32),
              (LANES, F32), (LANES, F32)]
    return pl.pallas_call(
        _inproj_kernel,
        grid=(n // tm,),
        in_specs=[row(d), full(g), full(wa), full(ws)],
        out_specs=[row(w) for w, _ in shapes],
        out_shape=[jax.ShapeDtypeStruct((n, w), dt) for w, dt in shapes],
        compiler_params=_params(("parallel",)),
        name="inproj",
    )(x, g, wa, ws)


def _conv_kernel(x_ref, w_ref, b_ref, lg_ref, lb_ref, o_ref, buf_ref, *, taps, tl):
    li = pl.program_id(1)

    @pl.when(li == 0)
    def _():
        buf_ref[0:CONV_HALO, :] = jnp.zeros((CONV_HALO, CONV_CH), F32)

    @pl.when(li > 0)
    def _():
        buf_ref[0:CONV_HALO, :] = buf_ref[tl:tl + CONV_HALO, :]

    buf_ref[CONV_HALO:CONV_HALO + tl, :] = x_ref[0]
    first = CONV_HALO - (taps - 1)
    by_shift = [[k for k in range(taps) if (first + k) % SUBLANES == s] for s in range(SUBLANES)]
    for r in range(0, tl, CONV_ROWS):
        acc = jnp.broadcast_to(b_ref[...], (CONV_ROWS, CONV_CH))
        for s, ks in enumerate(by_shift):
            if not ks:
                continue
            span = CONV_ROWS + (SUBLANES if s else 0)
            z = None
            for k in ks:
                base = r + first + k - s
                term = _rows8(w_ref, k, span) * buf_ref[base:base + span, :]
                z = term if z is None else z + term
            acc = acc + z[s:s + CONV_ROWS, :]
        mu = jnp.mean(acc, axis=-1, keepdims=True)
        cen = acc - mu
        var = jnp.mean(cen * cen, axis=-1, keepdims=True)
        y = cen * lax.rsqrt(var + EPS) * lg_ref[...] + lb_ref[...]
        o_ref[0, r:r + CONV_ROWS, :] = (y * _sigmoid(y)).astype(BF16)


def _conv(glu, w, b, lg, lb):
    bsz, seq, c = glu.shape
    tl = CONV_TILE
    taps = w.shape[0]
    w = jnp.repeat(w, SUBLANES, axis=0)
    full = lambda a: pl.BlockSpec(a.shape, lambda bi, li: (0, 0))
    return pl.pallas_call(
        functools.partial(_conv_kernel, taps=taps, tl=tl),
        grid=(bsz, seq // tl),
        in_specs=[pl.BlockSpec((1, tl, c), lambda bi, li: (bi, li, 0)), full(w), full(b), full(lg), full(lb)],
        out_specs=pl.BlockSpec((1, tl, c), lambda bi, li: (bi, li, 0)),
        out_shape=jax.ShapeDtypeStruct((bsz, seq, c), BF16),
        scratch_shapes=[pltpu.VMEM((tl + CONV_HALO, c), F32)],
        compiler_params=_params(("parallel", "arbitrary")),
        name="dwconv",
    )(glu, w, b, lg, lb)


def _lower_tri(n, dtype, strict=False):
    r = lax.broadcasted_iota(jnp.int32, (n, n), 0)
    c = lax.broadcasted_iota(jnp.int32, (n, n), 1)
    keep = (c < r) if strict else (c <= r)
    return jnp.where(keep, 1.0, 0.0).astype(dtype)


def _cum_kernel(f_ref, b_ref, o_ref, carry_ref, *, tl):
    li = pl.program_id(1)

    @pl.when(li == 0)
    def _():
        carry_ref[...] = jnp.zeros_like(carry_ref)

    x = f_ref[0] + b_ref[...]
    lf = jnp.minimum(x, 0.0) - jnp.log1p(jnp.exp(-jnp.abs(x)))
    cum = _select_dot(_lower_tri(tl, BF16), lf) + carry_ref[...]
    carry_ref[...] = cum[tl - 1:tl, :]
    o_ref[0] = cum.T[0:8, :]


def _cum(fs, fb):
    bsz, seq, _ = fs.shape
    tl = CUM_TILE
    return pl.pallas_call(
        functools.partial(_cum_kernel, tl=tl),
        grid=(bsz, seq // tl),
        in_specs=[pl.BlockSpec((1, tl, LANES), lambda bi, li: (bi, li, 0)),
                  pl.BlockSpec((1, LANES), lambda bi, li: (0, 0))],
        out_specs=pl.BlockSpec((1, 8, tl), lambda bi, li: (bi, 0, li)),
        out_shape=jax.ShapeDtypeStruct((bsz, 8, seq), F32),
        scratch_shapes=[pltpu.VMEM((1, LANES), F32)],
        compiler_params=_params(("parallel", "arbitrary")),
        name="fgate_cumsum",
    )(fs, fb)


def _attn_kernel(q_ref, qn_ref, k_ref, v_ref, c_ref, o_ref, qm_ref, s_ref, m_ref, l_ref, acc_ref, *, tq, nq):
    hp = pl.program_id(1)
    qi = pl.program_id(2)
    slot = qi % 2
    nslot = 1 - slot
    lane = lax.broadcasted_iota(jnp.int32, (tq, LANES), 1)
    row = lax.broadcasted_iota(jnp.int32, (tq, tq), 0)
    col = lax.broadcasted_iota(jnp.int32, (tq, tq), 1)
    tiles = tq // LANES

    def start_tile(sl, q):
        for hh in range(2):
            in_head = (lane >= HEAD_DIM) if hh else (lane < HEAD_DIM)
            qm_ref[hh] = jnp.where(in_head, q, jnp.zeros_like(q))
            m_ref[sl, hh] = jnp.full((tq, LANES), NEG_INF, F32)

    def scores(sl, j, masked):
        k0 = pl.multiple_of(j * tq, tq)
        kb = k_ref[0, pl.ds(k0, tq), :]
        for hh in range(2):
            cb = c_ref[0, pl.ds(hp * 2 + hh, 1), pl.ds(k0, tq)]
            s = lax.dot_general(qm_ref[hh], kb, (((1,), (1,)), ((), ())), preferred_element_type=F32) - cb
            if masked:
                s = jnp.where(col <= row, s, NEG_INF)
            s_ref[sl, hh, j] = s
            mt = m_ref[sl, hh]
            for t in range(tiles):
                mt = jnp.maximum(mt, s[:, t * LANES:(t + 1) * LANES])
            m_ref[sl, hh] = mt

    def finish_tile(sl, diag):
        scores(sl, diag, True)
        for hh in range(2):
            m_ref[sl, hh] = jnp.broadcast_to(jnp.max(m_ref[sl, hh], axis=-1, keepdims=True), (tq, LANES))

    def weighted(j):
        k0 = pl.multiple_of(j * tq, tq)
        vb = v_ref[0, pl.ds(k0, tq), :]
        for hh in range(2):
            mb = m_ref[slot, hh]
            lt = l_ref[hh]
            ps = []
            for t in range(tiles):
                p = jnp.exp(s_ref[slot, hh, j, :, t * LANES:(t + 1) * LANES] - mb)
                lt = lt + p
                ps.append(p.astype(BF16))
            l_ref[hh] = lt
            acc_ref[hh] += jnp.dot(jnp.concatenate(ps, axis=1), vb, preferred_element_type=F32)

    @pl.when(qi == 0)
    def _():
        start_tile(0, q_ref[0])
        finish_tile(0, 0)

    for hh in range(2):
        l_ref[hh] = jnp.zeros((tq, LANES), F32)
        acc_ref[hh] = jnp.zeros((tq, LANES), F32)

    @pl.when(qi + 1 < nq)
    def _():
        start_tile(nslot, qn_ref[0])

        def pair(jj, carry):
            for d in range(2):
                weighted(2 * jj + d)
                scores(nslot, 2 * jj + d, False)
            return carry

        lax.fori_loop(0, (qi + 1) // 2, pair, 0)

        @pl.when((qi + 1) % 2 == 1)
        def _():
            weighted(qi)
            scores(nslot, qi, False)

        finish_tile(nslot, qi + 1)

    @pl.when(qi + 1 == nq)
    def _():
        def last(jj, carry):
            for d in range(2):
                weighted(2 * jj + d)
            return carry

        lax.fori_loop(0, nq // 2, last, 0)

    outs = [acc_ref[hh] / jnp.sum(l_ref[hh], axis=-1, keepdims=True) for hh in range(2)]
    o_ref[0] = jnp.where(lane < HEAD_DIM, outs[0], outs[1]).astype(BF16)


def _attention(q, k, v, cum):
    bsz, seq, w = q.shape
    tq = ATTN_TQ
    pairs = w // LANES
    nq = seq // tq
    return pl.pallas_call(
        functools.partial(_attn_kernel, tq=tq, nq=nq),
        grid=(bsz, pairs, nq),
        in_specs=[pl.BlockSpec((1, tq, LANES), lambda b, h, i: (b, i, h)),
                  pl.BlockSpec((1, tq, LANES), lambda b, h, i: (b, jnp.minimum(i + 1, nq - 1), h)),
                  pl.BlockSpec((1, seq, LANES), lambda b, h, i: (b, 0, h)),
                  pl.BlockSpec((1, seq, LANES), lambda b, h, i: (b, 0, h)),
                  pl.BlockSpec((1, 8, seq), lambda b, h, i: (b, 0, 0))],
        out_specs=pl.BlockSpec((1, tq, LANES), lambda b, h, i: (b, i, h)),
        out_shape=jax.ShapeDtypeStruct((bsz, seq, w), BF16),
        scratch_shapes=[pltpu.VMEM((2, tq, LANES), BF16), pltpu.VMEM((2, 2, nq, tq, tq), F32),
                        pltpu.VMEM((2, 2, tq, LANES), F32), pltpu.VMEM((2, tq, LANES), F32),
                        pltpu.VMEM((2, tq, LANES), F32)],
        compiler_params=_params(("parallel", "parallel", "arbitrary")),
        name="fox_attention",
    )(q, q, k, v, cum)


def _ssd_kernel(xbc_ref, z_ref, dt_ref, cw_ref, cb_ref, dtb_ref, alog_ref, dw_ref, ng_ref, ex_ref, o_ref,
                buf_ref, h_ref, *, T, taps, chunks):
    ci = pl.program_id(1)
    span = T * chunks

    @pl.when(ci == 0)
    def _():
        buf_ref[0:SSD_HALO, :] = jnp.zeros((SSD_HALO, SSD_CONV_CH), F32)
        h_ref[...] = jnp.zeros_like(h_ref)

    @pl.when(ci > 0)
    def _():
        buf_ref[0:SSD_HALO, :] = buf_ref[span:span + SSD_HALO, :]

    buf_ref[SSD_HALO:SSD_HALO + span, :] = xbc_ref[0]
    for c in range(chunks):
        rows = pl.ds(c * T, T)
        _ssd_chunk(c * T, z_ref.at[0, rows], dt_ref.at[0, rows], cw_ref, cb_ref, dtb_ref, alog_ref, dw_ref,
                   ng_ref, ex_ref, o_ref.at[0, rows], buf_ref, h_ref, T=T, taps=taps)


def _ssd_chunk(base, z_ref, dt_ref, cw_ref, cb_ref, dtb_ref, alog_ref, dw_ref, ng_ref, ex_ref, o_ref,
               buf_ref, h_ref, *, T, taps):
    gw = SSD_W // SSD_GROUPS
    first = base + SSD_HALO - (taps - 1)
    acc = jnp.broadcast_to(cb_ref[...], (T, SSD_CONV_CH))
    for k in range(taps):
        acc = acc + _rows8(cw_ref, k, T) * buf_ref[first + k:first + k + T, :]
    xc = acc * _sigmoid(acc)
    xs = xc[:, 0:SSD_W]
    b_mat = xc[:, SSD_W:SSD_W + LANES]
    c_mat = xc[:, SSD_W + LANES:SSD_W + 2 * LANES]

    dt = _softplus(dt_ref[...] + dtb_ref[...])
    a = dt * (-jnp.exp(alog_ref[...]))
    acum = _select_dot(_lower_tri(T, BF16), a)
    acum_row = acum.T
    expand = ex_ref[...]
    dt_w = _dot_select(dt, expand)
    acum_w = _dot_select(acum, expand)
    last_w = acum_w[T - 1:T, :]
    xdt = xs * dt_w
    xdt_b = xdt.astype(BF16)
    xdec_b = (xdt * jnp.exp(last_w - acum_w)).astype(BF16)
    eacum_w = jnp.exp(acum_w)
    chunk_decay = jnp.exp(last_w)
    bb = b_mat.astype(BF16)
    cc = c_mat.astype(BF16)
    bt = b_mat.T.astype(BF16)

    row = lax.broadcasted_iota(jnp.int32, (T, T), 0)
    col = lax.broadcasted_iota(jnp.int32, (T, T), 1)
    causal = col <= row
    lane = lax.broadcasted_iota(jnp.int32, (T, LANES), 1)
    h_in = h_ref[...]
    h_ref[...] = h_in * chunk_decay + jnp.dot(bt, xdec_b, preferred_element_type=F32)
    h_in_b = h_in.astype(BF16)
    pieces = []
    for g in range(SSD_GROUPS):
        in_group = (lane >= g * SSD_STATE) & (lane < (g + 1) * SSD_STATE)
        cg = jnp.where(in_group, cc, jnp.zeros_like(cc))
        cbm = lax.dot_general(cg, bb, (((1,), (1,)), ((), ())), preferred_element_type=F32)
        yoff = jnp.dot(cg, h_in_b[:, g * gw:(g + 1) * gw], preferred_element_type=F32)
        for pr in range(gw // LANES):
            lo = g * gw + pr * LANES
            xpair = xdt_b[:, lo:lo + LANES]
            res = []
            for hh in range(2):
                head = lo // HEAD_DIM + hh
                seg = acum[:, head:head + 1] - acum_row[head:head + 1, :]
                lmat = jnp.exp(jnp.where(causal, seg, NEG_INF))
                res.append(jnp.dot((cbm * lmat).astype(BF16), xpair, preferred_element_type=F32))
            ydiag = jnp.where(lane < HEAD_DIM, res[0], res[1])
            pieces.append(ydiag + yoff[:, pr * LANES:(pr + 1) * LANES] * eacum_w[:, lo:lo + LANES])
    y = jnp.concatenate(pieces, axis=1) + xs * dw_ref[...]
    zz = z_ref[...]
    gated = y * (zz * _sigmoid(zz))
    ms = jnp.mean(gated * gated, axis=-1, keepdims=True)
    o_ref[...] = (gated * lax.rsqrt(ms + EPS) * ng_ref[...]).astype(BF16)


def _ssd(xbc, z, ds, cw, cb, dtb, alog, dwide, ng, expand):
    bsz, seq, _ = xbc.shape
    T = SSD_T
    taps = cw.shape[0]
    cw = jnp.repeat(cw, SUBLANES, axis=0)
    full = lambda a: pl.BlockSpec(a.shape, lambda bi, ci: (0, 0))
    span = T * SSD_CHUNKS_PER_STEP
    tok = lambda w: pl.BlockSpec((1, span, w), lambda bi, ci: (bi, ci, 0))
    return pl.pallas_call(
        functools.partial(_ssd_kernel, T=T, taps=taps, chunks=SSD_CHUNKS_PER_STEP),
        grid=(bsz, seq // span),
        in_specs=[tok(SSD_CONV_CH), tok(SSD_W), tok(LANES), full(cw), full(cb), full(dtb), full(alog),
                  full(dwide), full(ng), full(expand)],
        out_specs=tok(SSD_W),
        out_shape=jax.ShapeDtypeStruct((bsz, seq, SSD_W), BF16),
        scratch_shapes=[pltpu.VMEM((span + SSD_HALO, SSD_CONV_CH), F32),
                        pltpu.VMEM((SSD_GROUPS * SSD_STATE, SSD_W), F32)],
        compiler_params=_params(("parallel", "arbitrary")),
        name="ssd_mixer",
    )(xbc, z, ds, cw, cb, dtb, alog, dwide, ng, expand)


def _first_max(vals):
    best = vals[0]
    for v in vals[1:]:
        best = jnp.maximum(best, v)
    idx = jnp.full(best.shape, float(len(vals) - 1), F32)
    for j in range(len(vals) - 2, -1, -1):
        idx = jnp.where(vals[j] == best, float(j), idx)
    return best, idx


def _outproj_kernel(yc_ref, ya_ref, ys_ref, x_ref, wo_ref, g_ref, wr_ref, br_ref,
                    x1_ref, hf_ref, rrow_ref, rcol_ref, *, tm):
    y = jnp.dot(yc_ref[...], wo_ref[0:CONV_CH, :], preferred_element_type=F32)
    y = y + jnp.dot(ya_ref[...], wo_ref[CONV_CH:CONV_CH + ATTN_W, :], preferred_element_type=F32)
    y = y + jnp.dot(ys_ref[...], wo_ref[CONV_CH + ATTN_W:, :], preferred_element_type=F32)
    x1 = x_ref[...] + y
    x1_ref[...] = x1
    ms = jnp.mean(x1 * x1, axis=-1, keepdims=True)
    hf = x1 * lax.rsqrt(ms + EPS) * g_ref[...]
    hf_hi = hf.astype(BF16)
    hf_ref[...] = hf_hi
    hf_lo = (hf - hf_hi.astype(F32)).astype(BF16)
    logits = (jnp.dot(hf_hi, wr_ref[0], preferred_element_type=F32)
              + jnp.dot(hf_hi, wr_ref[1], preferred_element_type=F32)
              + jnp.dot(hf_lo, wr_ref[0], preferred_element_type=F32)) + br_ref[...]
    lt = logits.T
    gl = [lt[j:j + 1, :] for j in range(MOE_GROUPS)]
    gmax, gidx = _first_max(gl)
    denom = gl[0] * 0.0
    for v in gl:
        denom = denom + jnp.exp(v - gmax)
    gval = 1.0 / denom
    esel = []
    for j in range(EXPERTS_PER_GROUP):
        erow = lambda g: lt[MOE_GROUPS + g * EXPERTS_PER_GROUP + j:MOE_GROUPS + g * EXPERTS_PER_GROUP + j + 1, :]
        v = erow(MOE_GROUPS - 1)
        for g in range(MOE_GROUPS - 2, -1, -1):
            v = jnp.where(gidx == float(g), erow(g), v)
        esel.append(v)
    v1, i1 = _first_max(esel)
    rest = [jnp.where(i1 == float(j), NEG_INF, esel[j]) for j in range(EXPERTS_PER_GROUP)]
    v2, i2 = _first_max(rest)
    e2 = jnp.exp(v2 - v1)
    w1 = (1.0 / (1.0 + e2)) * gval
    w2 = (e2 / (1.0 + e2)) * gval
    cw = [jnp.where(i1 == float(j), w1, jnp.where(i2 == float(j), w2, 0.0)) for j in range(EXPERTS_PER_GROUP)]
    hi = [c.astype(BF16).astype(F32) for c in cw]
    mid = [(c - h).astype(BF16).astype(F32) for c, h in zip(cw, hi)]
    lo = [(c - h - m).astype(BF16).astype(F32) for c, h, m in zip(cw, hi, mid)]
    rows = [gidx] + hi + mid + lo
    ri = lax.broadcasted_iota(jnp.int32, (LANES, tm), 0)
    packed = jnp.zeros((LANES, tm), F32)
    for j, r in enumerate(rows):
        packed = jnp.where(ri == j, r, packed)
    rrow_ref[...] = packed[0:8, :]
    rcol_ref[...] = packed.T


def _outproj(yc, ya, ys, x, wo, g, wr, br):
    n, d = x.shape
    tm = ROW_TILE
    row = lambda w: pl.BlockSpec((tm, w), lambda i: (i, 0))
    full = lambda a: pl.BlockSpec(a.shape, lambda i: (0, 0))
    return pl.pallas_call(
        functools.partial(_outproj_kernel, tm=tm),
        grid=(n // tm,),
        in_specs=[row(CONV_CH), row(ATTN_W), row(SSD_W), row(d), full(wo), full(g),
                  pl.BlockSpec(wr.shape, lambda i: (0, 0, 0)), full(br)],
        out_specs=[row(d), row(d), pl.BlockSpec((8, tm), lambda i: (0, i)), row(LANES)],
        out_shape=[jax.ShapeDtypeStruct((n, d), F32), jax.ShapeDtypeStruct((n, d), BF16),
                   jax.ShapeDtypeStruct((8, n), F32), jax.ShapeDtypeStruct((n, LANES), F32)],
        compiler_params=_params(("parallel",)),
        name="outproj_router",
    )(yc, ya, ys, x, wo, g, wr, br)


def _moe_kernel(cnt_ref, x1_ref, hf_ref, rrow_ref, rcol_ref, wg_ref, wu_ref, wd_ref, fg_ref, o_ref,
                xg_ref, acc_ref, cw_ref, rankr_ref, *, ts, spans, rows, cap, final_norm):
    i = pl.program_id(0)
    g = pl.program_id(1)
    e = pl.program_id(2)
    gf = g.astype(F32)

    def tok(sb):
        return slice(sb * ts, (sb + 1) * ts)

    def chunk(sb, c, r):
        if isinstance(c, int):
            return slice(sb * cap + c * r, sb * cap + (c + 1) * r)
        return pl.ds(pl.multiple_of(sb * cap + c * r, 16), r)

    def group_rank(sb):
        sub8 = lax.broadcasted_iota(jnp.int32, (8, ts), 0)
        return jnp.sum(jnp.where(sub8 == g, rankr_ref[:, tok(sb)], 0.0), axis=0, keepdims=True)

    def placement(sb, c, r):
        first = float(c * r) if isinstance(c, int) else (c * r).astype(F32)
        slot = lax.broadcasted_iota(jnp.int32, (r, ts), 0).astype(F32) + first
        return jnp.where((group_rank(sb) == slot) & (rrow_ref[0:1, tok(sb)] == gf), 1.0, 0.0).astype(BF16)

    def gather(sb, c, r):
        p = placement(sb, c, r)
        xg_ref[chunk(sb, c, r), :] = jnp.dot(p, hf_ref[tok(sb), :], preferred_element_type=F32).astype(BF16)
        cw_ref[chunk(sb, c, r), :] = jnp.dot(p, rcol_ref[tok(sb), :].astype(BF16), preferred_element_type=F32)
        acc_ref[chunk(sb, c, r), :] = jnp.zeros((r, acc_ref.shape[1]), F32)

    def expert(sb, c, r):
        lane_r = lax.broadcasted_iota(jnp.int32, (r, LANES), 1)
        pick = ((lane_r == 1 + e) | (lane_r == 1 + EXPERTS_PER_GROUP + e)
                | (lane_r == 1 + 2 * EXPERTS_PER_GROUP + e))
        xc = xg_ref[chunk(sb, c, r), :]
        hg = jnp.dot(xc, wg_ref[0], preferred_element_type=F32)
        hu = jnp.dot(xc, wu_ref[0], preferred_element_type=F32)
        hh = (hg * _sigmoid(hg) * hu).astype(BF16)
        y = jnp.dot(hh, wd_ref[0], preferred_element_type=F32)
        cwe = jnp.sum(jnp.where(pick, cw_ref[chunk(sb, c, r), :], 0.0), axis=-1, keepdims=True)
        acc_ref[chunk(sb, c, r), :] += cwe * y

    def scatter(sb, c, r):
        tn = (((0,), (0,)), ((), ()))
        o_ref[tok(sb), :] += lax.dot_general(placement(sb, c, r), acc_ref[chunk(sb, c, r), :].astype(BF16), tn,
                                             preferred_element_type=F32)

    def step(sb, c, r):
        @pl.when(e == 0)
        def _():
            gather(sb, c, r)

        expert(sb, c, r)

        @pl.when(e == EXPERTS_PER_GROUP - 1)
        def _():
            scatter(sb, c, r)

    @pl.when((g == 0) & (e == 0))
    def _():
        o_ref[...] = x1_ref[...]
        sub = lax.broadcasted_iota(jnp.int32, (8, ts), 0).astype(F32)
        for c0 in range(0, ts, RANK_STRIP):
            r_ = lax.broadcasted_iota(jnp.int32, (ts, RANK_STRIP), 0)
            c_ = lax.broadcasted_iota(jnp.int32, (ts, RANK_STRIP), 1) + c0
            before = jnp.where(r_ < c_, 1.0, 0.0).astype(BF16)
            for sb in range(spans):
                m_row = jnp.where(rrow_ref[0:1, tok(sb)] == sub, 1.0, 0.0).astype(BF16)
                rankr_ref[:, sb * ts + c0:sb * ts + c0 + RANK_STRIP] = jnp.dot(
                    m_row, before, preferred_element_type=F32)

    for sb in range(spans):
        cnt = cnt_ref[(i * spans + sb) * MOE_GROUPS + g]
        lo = 0
        for r in rows[:-1]:
            @pl.when((cnt > lo) & (cnt <= r))
            def _(sb=sb, r=r):
                step(sb, 0, r)
            lo = r

        @pl.when(cnt > lo)
        def _(sb=sb, cnt=cnt):
            r = rows[-1]

            def body(c, carry):
                step(sb, c, r)
                return carry

            lax.fori_loop(0, (cnt + r - 1) // r, body, 0)

    if final_norm:
        @pl.when((g == MOE_GROUPS - 1) & (e == EXPERTS_PER_GROUP - 1))
        def _():
            xo = o_ref[...]
            ms = jnp.mean(xo * xo, axis=-1, keepdims=True)
            o_ref[...] = xo * lax.rsqrt(ms + EPS) * fg_ref[...]


def _moe(counts, x1, hf, rrow, rcol, wg, wu, wd, fg, final_norm):
    n, d = x1.shape
    tb, ts, rows = MOE_BLOCK, MOE_SORT, MOE_ROWS
    spans = tb // ts
    de = wg.shape[2]
    cap = -(-ts // rows[-1]) * rows[-1]
    once = pl.Buffered(1)
    tokens = lambda w, **kw: pl.BlockSpec((tb, w), lambda i, g, e, c: (i, 0), **kw)
    expert = lambda shape: pl.BlockSpec(shape, lambda i, g, e, c: (g * EXPERTS_PER_GROUP + e, 0, 0))
    grid_spec = pltpu.PrefetchScalarGridSpec(
        num_scalar_prefetch=1,
        grid=(n // tb, MOE_GROUPS, EXPERTS_PER_GROUP),
        in_specs=[tokens(d, pipeline_mode=once), tokens(d),
                  pl.BlockSpec((8, tb), lambda i, g, e, c: (0, i)), tokens(LANES),
                  expert((1, d, de)), expert((1, d, de)), expert((1, de, d)),
                  pl.BlockSpec((1, d), lambda i, g, e, c: (0, 0))],
        out_specs=tokens(d),
        scratch_shapes=[pltpu.VMEM((spans * cap, d), BF16), pltpu.VMEM((spans * cap, d), F32),
                        pltpu.VMEM((spans * cap, LANES), F32), pltpu.VMEM((8, tb), F32)],
    )
    return pl.pallas_call(
        functools.partial(_moe_kernel, ts=ts, spans=spans, rows=rows, cap=cap, final_norm=final_norm),
        grid_spec=grid_spec,
        out_shape=jax.ShapeDtypeStruct((n, d), F32),
        compiler_params=_params(("parallel", "arbitrary", "arbitrary")),
        name="moe",
    )(counts, x1, hf, rrow, rcol, wg, wu, wd, fg)


def _pad_lanes(v, width=LANES):
    v = v.reshape(1, -1).astype(F32)
    return jnp.pad(v, ((0, 0), (0, width - v.shape[1])))


def kernel(x, norm_mix, w_in, conv_dw_w, conv_dw_b, conv_ln_g, conv_ln_b, fgate_b, ssd_conv_w, ssd_conv_b,
           ssd_dt_bias, ssd_A_log, ssd_D, ssd_norm_g, w_out, norm_ffn, router_group_w, router_group_b,
           router_expert_w, router_expert_b, expert_w_gate, expert_w_up, expert_w_down, norm_final):
    bsz, seq, d = x.shape
    n = bsz * seq
    depth = w_in.shape[0]
    heads = fgate_b.shape[1]
    ssd_heads = ssd_A_log.shape[1]
    n_exp = expert_w_gate.shape[1]
    assert heads * HEAD_DIM == ATTN_W and ssd_heads * HEAD_DIM == SSD_W
    assert n_exp == MOE_GROUPS * EXPERTS_PER_GROUP and conv_dw_w.shape[1] <= CONV_HALO + 1
    assert n % ROW_TILE == 0 and n % MOE_BLOCK == 0 and seq % max(CONV_TILE, CUM_TILE, 2 * ATTN_TQ, SSD_T) == 0

    sizes = (2 * CONV_CH, ATTN_W, ATTN_W, ATTN_W, heads, SSD_W, SSD_W, SSD_GROUPS * SSD_STATE,
             SSD_GROUPS * SSD_STATE, ssd_heads)
    offs = [0]
    for s in sizes:
        offs.append(offs[-1] + s)
    expand = jnp.repeat(jnp.eye(LANES, SSD_W // HEAD_DIM, dtype=BF16), HEAD_DIM, axis=1)

    xr = x.reshape(n, d)
    for l in range(depth):
        w = w_in[l]
        wa = jnp.concatenate([w[:, offs[0]:offs[4]], w[:, offs[5]:offs[9]]], axis=1).astype(BF16)
        zpad = jnp.zeros((d, LANES - heads), F32)
        ws = jnp.concatenate([w[:, offs[4]:offs[5]], zpad, w[:, offs[9]:offs[10]], zpad], axis=1).astype(BF16)
        glu, q, k, v, z, xbc, fs, ds = _inproj(xr, norm_mix[l].reshape(1, d), wa, ws)

        seq3 = lambda a: a.reshape(bsz, seq, a.shape[-1])
        y_conv = _conv(seq3(glu), conv_dw_w[l], conv_dw_b[l].reshape(1, -1), conv_ln_g[l].reshape(1, -1),
                       conv_ln_b[l].reshape(1, -1))
        cum = _cum(seq3(fs), _pad_lanes(fgate_b[l]))
        y_attn = _attention(seq3(q), seq3(k), seq3(v), cum)
        y_ssd = _ssd(seq3(xbc), seq3(z), seq3(ds), ssd_conv_w[l], ssd_conv_b[l].reshape(1, -1),
                     _pad_lanes(ssd_dt_bias[l]), _pad_lanes(ssd_A_log[l]),
                     jnp.repeat(ssd_D[l].astype(F32), HEAD_DIM).reshape(1, -1), ssd_norm_g[l].reshape(1, -1),
                     expand)

        wr = jnp.concatenate([router_group_w[l], router_expert_w[l],
                              jnp.zeros((d, LANES - MOE_GROUPS - n_exp), F32)], axis=1)
        wr_hi = wr.astype(BF16)
        wr = jnp.stack([wr_hi, (wr - wr_hi.astype(F32)).astype(BF16)])
        br = _pad_lanes(jnp.concatenate([router_group_b[l], router_expert_b[l]]))
        x1, hf, rrow, rcol = _outproj(y_conv.reshape(n, -1), y_attn.reshape(n, -1), y_ssd.reshape(n, -1), xr,
                                      w_out[l].astype(BF16), norm_ffn[l].reshape(1, d), wr, br)

        gidx = rrow[0].astype(jnp.int32).reshape(n // MOE_SORT, MOE_SORT)
        counts = jnp.sum(gidx[:, :, None] == jnp.arange(MOE_GROUPS, dtype=jnp.int32), axis=1,
                         dtype=jnp.int32).reshape(-1)
        xr = _moe(counts, x1, hf, rrow, rcol, expert_w_gate[l].astype(BF16), expert_w_up[l].astype(BF16),
                  expert_w_down[l].astype(BF16), norm_final.reshape(1, d), final_norm=(l == depth - 1))
    return xr.reshape(bsz, seq, d)
```

```python
import functools

import jax
import jax.numpy as jnp
from jax import lax
from jax.experimental import pallas as pl
from jax.experimental.pallas import tpu as pltpu

F32 = jnp.float32
BF16 = jnp.bfloat16
EPS = 1e-6
NEG_INF = float("-inf")

HEAD_DIM = 64
LANES = 128
CONV_CH = 512
ATTN_W = 512
SSD_W = 512
SSD_GROUPS = 2
SSD_STATE = 64
SSD_CONV_CH = SSD_W + 2 * SSD_GROUPS * SSD_STATE
MOE_GROUPS = 4
EXPERTS_PER_GROUP = 4

ROW_TILE = 1024
CONV_TILE = 512
CONV_ROWS = 32
CONV_HALO = 32
SUBLANES = 8
CUM_TILE = 512
ATTN_TQ = 512
SSD_T = 256
SSD_CHUNKS_PER_STEP = 2
SSD_HALO = 8
MOE_BLOCK = 2048
MOE_SORT = 1024
MOE_ROWS = (256, 288, 320, 384)
RANK_STRIP = 256
VMEM_LIMIT = 60 * 1024 * 1024


def _params(sem):
    return pltpu.CompilerParams(dimension_semantics=sem, vmem_limit_bytes=VMEM_LIMIT)


def _sigmoid(x):
    return jax.nn.sigmoid(x)


def _softplus(x):
    return jnp.maximum(x, 0.0) + jnp.log1p(jnp.exp(-jnp.abs(x)))


def _split3(x):
    hi = x.astype(BF16)
    r = x - hi.astype(F32)
    mid = r.astype(BF16)
    lo = (r - mid.astype(F32)).astype(BF16)
    return hi, mid, lo


def _select_dot(sel, x):
    hi, mid, lo = _split3(x)
    d = lambda p: jnp.dot(sel, p, preferred_element_type=F32)
    return d(hi) + d(mid) + d(lo)


def _dot_select(x, sel):
    hi, mid, lo = _split3(x)
    d = lambda p: jnp.dot(p, sel, preferred_element_type=F32)
    return d(hi) + d(mid) + d(lo)


def _rows8(w_ref, k, rows):
    return jnp.concatenate([w_ref[k * SUBLANES:(k + 1) * SUBLANES, :]] * (rows // SUBLANES), axis=0)


def _inproj_kernel(x_ref, g_ref, wa_ref, ws_ref, glu_ref, q_ref, k_ref, v_ref, z_ref, xbc_ref,
                   fs_ref, ds_ref):
    xf = x_ref[...]
    ms = jnp.mean(xf * xf, axis=-1, keepdims=True)
    h = (xf * lax.rsqrt(ms + EPS) * g_ref[...]).astype(BF16)

    def mm(lo, hi):
        return jnp.dot(h, wa_ref[:, lo:hi], preferred_element_type=F32)

    glu_ref[...] = mm(0, 512) * _sigmoid(mm(512, 1024))
    q_ref[...] = (mm(1024, 1536) * (HEAD_DIM ** -0.5)).astype(BF16)
    k_ref[...] = mm(1536, 2048).astype(BF16)
    v_ref[...] = mm(2048, 2560).astype(BF16)
    z_ref[...] = mm(2560, 3072)
    xbc_ref[...] = mm(3072, 3840)
    small = jnp.dot(h, ws_ref[...], preferred_element_type=F32)
    fs_ref[...] = small[:, 0:LANES]
    ds_ref[...] = small[:, LANES:2 * LANES]


def _inproj(x, g, wa, ws):
    n, d = x.shape
    tm = ROW_TILE
    row = lambda w: pl.BlockSpec((tm, w), lambda i: (i, 0))
    full = lambda a: pl.BlockSpec(a.shape, lambda i: (0, 0))
    shapes = [(512, F32), (512, BF16), (512, BF16), (512, BF16), (512, F32), (SSD_CONV_CH, F32),
              (LANES, F32), (LANES, F32)]
    return pl.pallas_call(
        _inproj_kernel,
        grid=(n // tm,),
        in_specs=[row(d), full(g), full(wa), full(ws)],
        out_specs=[row(w) for w, _ in shapes],
        out_shape=[jax.ShapeDtypeStruct((n, w), dt) for w, dt in shapes],
        compiler_params=_params(("parallel",)),
        name="inproj",
    )(x, g, wa, ws)


def _conv_kernel(x_ref, w_ref, b_ref, lg_ref, lb_ref, o_ref, buf_ref, *, taps, tl):
    li = pl.program_id(1)

    @pl.when(li == 0)
    def _():
        buf_ref[0:CONV_HALO, :] = jnp.zeros((CONV_HALO, CONV_CH), F32)

    @pl.when(li > 0)
    def _():
        buf_ref[0:CONV_HALO, :] = buf_ref[tl:tl + CONV_HALO, :]

    buf_ref[CONV_HALO:CONV_HALO + tl, :] = x_ref[0]
    first = CONV_HALO - (taps - 1)
    by_shift = [[k for k in range(taps) if (first + k) % SUBLANES == s] for s in range(SUBLANES)]
    for r in range(0, tl, CONV_ROWS):
        acc = jnp.broadcast_to(b_ref[...], (CONV_ROWS, CONV_CH))
        for s, ks in enumerate(by_shift):
            if not ks:
                continue
            span = CONV_ROWS + (SUBLANES if s else 0)
            z = None
            for k in ks:
                base = r + first + k - s
                term = _rows8(w_ref, k, span) * buf_ref[base:base + span, :]
                z = term if z is None else z + term
            acc = acc + z[s:s + CONV_ROWS, :]
        mu = jnp.mean(acc, axis=-1, keepdims=True)
        cen = acc - mu
        var = jnp.mean(cen * cen, axis=-1, keepdims=True)
        y = cen * lax.rsqrt(var + EPS) * lg_ref[...] + lb_ref[...]
        o_ref[0, r:r + CONV_ROWS, :] = (y * _sigmoid(y)).astype(BF16)


def _conv(glu, w, b, lg, lb):
    bsz, seq, c = glu.shape
    tl = CONV_TILE
    taps = w.shape[0]
    w = jnp.repeat(w, SUBLANES, axis=0)
    full = lambda a: pl.BlockSpec(a.shape, lambda bi, li: (0, 0))
    return pl.pallas_call(
        functools.partial(_conv_kernel, taps=taps, tl=tl),
        grid=(bsz, seq // tl),
        in_specs=[pl.BlockSpec((1, tl, c), lambda bi, li: (bi, li, 0)), full(w), full(b), full(lg), full(lb)],
        out_specs=pl.BlockSpec((1, tl, c), lambda bi, li: (bi, li, 0)),
        out_shape=jax.ShapeDtypeStruct((bsz, seq, c), BF16),
        scratch_shapes=[pltpu.VMEM((tl + CONV_HALO, c), F32)],
        compiler_params=_params(("parallel", "arbitrary")),
        name="dwconv",
    )(glu, w, b, lg, lb)


def _lower_tri(n, dtype, strict=False):
    r = lax.broadcasted_iota(jnp.int32, (n, n), 0)
    c = lax.broadcasted_iota(jnp.int32, (n, n), 1)
    keep = (c < r) if strict else (c <= r)
    return jnp.where(keep, 1.0, 0.0).astype(dtype)


def _cum_kernel(f_ref, b_ref, o_ref, carry_ref, *, tl):
    li = pl.program_id(1)

    @pl.when(li == 0)
    def _():
        carry_ref[...] = jnp.zeros_like(carry_ref)

    x = f_ref[0] + b_ref[...]
    lf = jnp.minimum(x, 0.0) - jnp.log1p(jnp.exp(-jnp.abs(x)))
    cum = _select_dot(_lower_tri(tl, BF16), lf) + carry_ref[...]
    carry_ref[...] = cum[tl - 1:tl, :]
    o_ref[0] = cum.T[0:8, :]


def _cum(fs, fb):
    bsz, seq, _ = fs.shape
    tl = CUM_TILE
    return pl.pallas_call(
        functools.partial(_cum_kernel, tl=tl),
        grid=(bsz, seq // tl),
        in_specs=[pl.BlockSpec((1, tl, LANES), lambda bi, li: (bi, li, 0)),
                  pl.BlockSpec((1, LANES), lambda bi, li: (0, 0))],
        out_specs=pl.BlockSpec((1, 8, tl), lambda bi, li: (bi, 0, li)),
        out_shape=jax.ShapeDtypeStruct((bsz, 8, seq), F32),
        scratch_shapes=[pltpu.VMEM((1, LANES), F32)],
        compiler_params=_params(("parallel", "arbitrary")),
        name="fgate_cumsum",
    )(fs, fb)


def _attn_kernel(q_ref, qn_ref, k_ref, v_ref, c_ref, o_ref, qm_ref, s_ref, m_ref, l_ref, acc_ref, *, tq, nq):
    hp = pl.program_id(1)
    qi = pl.program_id(2)
    slot = qi % 2
    nslot = 1 - slot
    lane = lax.broadcasted_iota(jnp.int32, (tq, LANES), 1)
    row = lax.broadcasted_iota(jnp.int32, (tq, tq), 0)
    col = lax.broadcasted_iota(jnp.int32, (tq, tq), 1)
    tiles = tq // LANES

    def start_tile(sl, q):
        for hh in range(2):
            in_head = (lane >= HEAD_DIM) if hh else (lane < HEAD_DIM)
            qm_ref[hh] = jnp.where(in_head, q, jnp.zeros_like(q))
            m_ref[sl, hh] = jnp.full((tq, LANES), NEG_INF, F32)

    def scores(sl, j):
        k0 = pl.multiple_of(j * tq, tq)
        kb = k_ref[0, pl.ds(k0, tq), :]
        for hh in range(2):
            cb = c_ref[0, pl.ds(hp * 2 + hh, 1), pl.ds(k0, tq)]
            s = lax.dot_general(qm_ref[hh], kb, (((1,), (1,)), ((), ())), preferred_element_type=F32) - cb
            s_ref[sl, hh, j] = s
            mt = m_ref[sl, hh]
            for t in range(tiles):
                mt = jnp.maximum(mt, s[:, t * LANES:(t + 1) * LANES])
            m_ref[sl, hh] = mt

    def diag_scores(sl, j):
        half = tq // 2
        k0 = pl.multiple_of(j * tq, tq)
        kb = k_ref[0, pl.ds(k0, tq), :]
        for hh in range(2):
            cb = c_ref[0, pl.ds(hp * 2 + hh, 1), pl.ds(k0, tq)]
            cb_lo = c_ref[0, pl.ds(hp * 2 + hh, 1), pl.ds(k0, half)]
            nt = (((1,), (1,)), ((), ()))
            top = lax.dot_general(qm_ref[hh, 0:half, :], k_ref[0, pl.ds(k0, half), :], nt,
                                  preferred_element_type=F32) - cb_lo
            top = jnp.where(lax.broadcasted_iota(jnp.int32, (half, half), 1)
                            <= lax.broadcasted_iota(jnp.int32, (half, half), 0), top, NEG_INF)
            bot = lax.dot_general(qm_ref[hh, half:tq, :], kb, nt, preferred_element_type=F32) - cb
            bot = jnp.where(lax.broadcasted_iota(jnp.int32, (half, tq), 1)
                            <= lax.broadcasted_iota(jnp.int32, (half, tq), 0) + half, bot, NEG_INF)
            s_ref[sl, hh, j, 0:half, 0:half] = top
            s_ref[sl, hh, j, 0:half, half:tq] = jnp.full((half, half), NEG_INF, F32)
            s_ref[sl, hh, j, half:tq, :] = bot
            for lo, blk in ((0, top), (half, bot)):
                mt = m_ref[sl, hh, lo:lo + half, :]
                for t in range(blk.shape[1] // LANES):
                    mt = jnp.maximum(mt, blk[:, t * LANES:(t + 1) * LANES])
                m_ref[sl, hh, lo:lo + half, :] = mt

    def finish_tile(sl, diag):
        diag_scores(sl, diag)
        for hh in range(2):
            m_ref[sl, hh] = jnp.broadcast_to(jnp.max(m_ref[sl, hh], axis=-1, keepdims=True), (tq, LANES))

    def weighted(j):
        k0 = pl.multiple_of(j * tq, tq)
        vb = v_ref[0, pl.ds(k0, tq), :]
        for hh in range(2):
            mb = m_ref[slot, hh]
            lt = l_ref[hh]
            ps = []
            for t in range(tiles):
                p = jnp.exp(s_ref[slot, hh, j, :, t * LANES:(t + 1) * LANES] - mb)
                lt = lt + p
                ps.append(p.astype(BF16))
            l_ref[hh] = lt
            acc_ref[hh] += jnp.dot(jnp.concatenate(ps, axis=1), vb, preferred_element_type=F32)

    @pl.when(qi == 0)
    def _():
        start_tile(0, q_ref[0])
        finish_tile(0, 0)

    for hh in range(2):
        l_ref[hh] = jnp.zeros((tq, LANES), F32)
        acc_ref[hh] = jnp.zeros((tq, LANES), F32)

    @pl.when(qi + 1 < nq)
    def _():
        start_tile(nslot, qn_ref[0])

        def group(first, count):
            for d in range(count):
                weighted(first + d)
                scores(nslot, first + d)

        def pair(jj, carry):
            group(2 * jj, 2)
            return carry

        n_full = qi + 1
        lax.fori_loop(0, n_full // 2, pair, 0)

        @pl.when(n_full % 2 == 1)
        def _():
            group(qi, 1)

        finish_tile(nslot, qi + 1)

    @pl.when(qi + 1 == nq)
    def _():
        def last(jj, carry):
            for d in range(2):
                weighted(2 * jj + d)
            return carry

        lax.fori_loop(0, nq // 2, last, 0)

    outs = [acc_ref[hh] / jnp.sum(l_ref[hh], axis=-1, keepdims=True) for hh in range(2)]
    o_ref[0] = jnp.where(lane < HEAD_DIM, outs[0], outs[1]).astype(BF16)


def _attention(q, k, v, cum):
    bsz, seq, w = q.shape
    tq = ATTN_TQ
    pairs = w // LANES
    nq = seq // tq
    return pl.pallas_call(
        functools.partial(_attn_kernel, tq=tq, nq=nq),
        grid=(bsz, pairs, nq),
        in_specs=[pl.BlockSpec((1, tq, LANES), lambda b, h, i: (b, i, h)),
                  pl.BlockSpec((1, tq, LANES), lambda b, h, i: (b, jnp.minimum(i + 1, nq - 1), h)),
                  pl.BlockSpec((1, seq, LANES), lambda b, h, i: (b, 0, h)),
                  pl.BlockSpec((1, seq, LANES), lambda b, h, i: (b, 0, h)),
                  pl.BlockSpec((1, 8, seq), lambda b, h, i: (b, 0, 0))],
        out_specs=pl.BlockSpec((1, tq, LANES), lambda b, h, i: (b, i, h)),
        out_shape=jax.ShapeDtypeStruct((bsz, seq, w), BF16),
        scratch_shapes=[pltpu.VMEM((2, tq, LANES), BF16), pltpu.VMEM((2, 2, nq, tq, tq), F32),
                        pltpu.VMEM((2, 2, tq, LANES), F32), pltpu.VMEM((2, tq, LANES), F32),
                        pltpu.VMEM((2, tq, LANES), F32)],
        compiler_params=_params(("parallel", "parallel", "arbitrary")),
        name="fox_attention",
    )(q, q, k, v, cum)


def _ssd_kernel(xbc_ref, z_ref, dt_ref, cw_ref, cb_ref, dtb_ref, alog_ref, dw_ref, ng_ref, ex_ref, o_ref,
                buf_ref, h_ref, *, T, taps, chunks):
    ci = pl.program_id(1)
    span = T * chunks

    @pl.when(ci == 0)
    def _():
        buf_ref[0:SSD_HALO, :] = jnp.zeros((SSD_HALO, SSD_CONV_CH), F32)
        h_ref[...] = jnp.zeros_like(h_ref)

    @pl.when(ci > 0)
    def _():
        buf_ref[0:SSD_HALO, :] = buf_ref[span:span + SSD_HALO, :]

    buf_ref[SSD_HALO:SSD_HALO + span, :] = xbc_ref[0]
    for c in range(chunks):
        rows = pl.ds(c * T, T)
        _ssd_chunk(c * T, z_ref.at[0, rows], dt_ref.at[0, rows], cw_ref, cb_ref, dtb_ref, alog_ref, dw_ref,
                   ng_ref, ex_ref, o_ref.at[0, rows], buf_ref, h_ref, T=T, taps=taps)


def _ssd_chunk(base, z_ref, dt_ref, cw_ref, cb_ref, dtb_ref, alog_ref, dw_ref, ng_ref, ex_ref, o_ref,
               buf_ref, h_ref, *, T, taps):
    gw = SSD_W // SSD_GROUPS
    first = base + SSD_HALO - (taps - 1)
    acc = jnp.broadcast_to(cb_ref[...], (T, SSD_CONV_CH))
    for k in range(taps):
        acc = acc + _rows8(cw_ref, k, T) * buf_ref[first + k:first + k + T, :]
    xc = acc * _sigmoid(acc)
    xs = xc[:, 0:SSD_W]
    b_mat = xc[:, SSD_W:SSD_W + LANES]
    c_mat = xc[:, SSD_W + LANES:SSD_W + 2 * LANES]

    dt = _softplus(dt_ref[...] + dtb_ref[...])
    a = dt * (-jnp.exp(alog_ref[...]))
    acum = _select_dot(_lower_tri(T, BF16), a)
    acum_row = acum.T
    expand = ex_ref[...]
    dt_w = _dot_select(dt, expand)
    acum_w = _dot_select(acum, expand)
    last_w = acum_w[T - 1:T, :]
    xdt = xs * dt_w
    xdt_b = xdt.astype(BF16)
    xdec_b = (xdt * jnp.exp(last_w - acum_w)).astype(BF16)
    eacum_w = jnp.exp(acum_w)
    chunk_decay = jnp.exp(last_w)
    bb = b_mat.astype(BF16)
    cc = c_mat.astype(BF16)
    bt = b_mat.T.astype(BF16)

    row = lax.broadcasted_iota(jnp.int32, (T, T), 0)
    col = lax.broadcasted_iota(jnp.int32, (T, T), 1)
    causal = col <= row
    lane = lax.broadcasted_iota(jnp.int32, (T, LANES), 1)
    h_in = h_ref[...]
    h_ref[...] = h_in * chunk_decay + jnp.dot(bt, xdec_b, preferred_element_type=F32)
    h_in_b = h_in.astype(BF16)
    pieces = []
    for g in range(SSD_GROUPS):
        in_group = (lane >= g * SSD_STATE) & (lane < (g + 1) * SSD_STATE)
        cg = jnp.where(in_group, cc, jnp.zeros_like(cc))
        cbm = lax.dot_general(cg, bb, (((1,), (1,)), ((), ())), preferred_element_type=F32)
        yoff = jnp.dot(cg, h_in_b[:, g * gw:(g + 1) * gw], preferred_element_type=F32)
        for pr in range(gw // LANES):
            lo = g * gw + pr * LANES
            xpair = xdt_b[:, lo:lo + LANES]
            res = []
            for hh in range(2):
                head = lo // HEAD_DIM + hh
                seg = acum[:, head:head + 1] - acum_row[head:head + 1, :]
                lmat = jnp.exp(jnp.where(causal, seg, NEG_INF))
                res.append(jnp.dot((cbm * lmat).astype(BF16), xpair, preferred_element_type=F32))
            ydiag = jnp.where(lane < HEAD_DIM, res[0], res[1])
            pieces.append(ydiag + yoff[:, pr * LANES:(pr + 1) * LANES] * eacum_w[:, lo:lo + LANES])
    y = jnp.concatenate(pieces, axis=1) + xs * dw_ref[...]
    zz = z_ref[...]
    gated = y * (zz * _sigmoid(zz))
    ms = jnp.mean(gated * gated, axis=-1, keepdims=True)
    o_ref[...] = (gated * lax.rsqrt(ms + EPS) * ng_ref[...]).astype(BF16)


def _ssd(xbc, z, ds, cw, cb, dtb, alog, dwide, ng, expand):
    bsz, seq, _ = xbc.shape
    T = SSD_T
    taps = cw.shape[0]
    cw = jnp.repeat(cw, SUBLANES, axis=0)
    full = lambda a: pl.BlockSpec(a.shape, lambda bi, ci: (0, 0))
    span = T * SSD_CHUNKS_PER_STEP
    tok = lambda w: pl.BlockSpec((1, span, w), lambda bi, ci: (bi, ci, 0))
    return pl.pallas_call(
        functools.partial(_ssd_kernel, T=T, taps=taps, chunks=SSD_CHUNKS_PER_STEP),
        grid=(bsz, seq // span),
        in_specs=[tok(SSD_CONV_CH), tok(SSD_W), tok(LANES), full(cw), full(cb), full(dtb), full(alog),
                  full(dwide), full(ng), full(expand)],
        out_specs=tok(SSD_W),
        out_shape=jax.ShapeDtypeStruct((bsz, seq, SSD_W), BF16),
        scratch_shapes=[pltpu.VMEM((span + SSD_HALO, SSD_CONV_CH), F32),
                        pltpu.VMEM((SSD_GROUPS * SSD_STATE, SSD_W), F32)],
        compiler_params=_params(("parallel", "arbitrary")),
        name="ssd_mixer",
    )(xbc, z, ds, cw, cb, dtb, alog, dwide, ng, expand)


def _first_max(vals):
    best = vals[0]
    for v in vals[1:]:
        best = jnp.maximum(best, v)
    idx = jnp.full(best.shape, float(len(vals) - 1), F32)
    for j in range(len(vals) - 2, -1, -1):
        idx = jnp.where(vals[j] == best, float(j), idx)
    return best, idx


def _outproj_kernel(yc_ref, ya_ref, ys_ref, x_ref, wo_ref, g_ref, wr_ref, br_ref,
                    x1_ref, hf_ref, rrow_ref, rcol_ref, *, tm):
    y = jnp.dot(yc_ref[...], wo_ref[0:CONV_CH, :], preferred_element_type=F32)
    y = y + jnp.dot(ya_ref[...], wo_ref[CONV_CH:CONV_CH + ATTN_W, :], preferred_element_type=F32)
    y = y + jnp.dot(ys_ref[...], wo_ref[CONV_CH + ATTN_W:, :], preferred_element_type=F32)
    x1 = x_ref[...] + y
    x1_ref[...] = x1
    ms = jnp.mean(x1 * x1, axis=-1, keepdims=True)
    hf = x1 * lax.rsqrt(ms + EPS) * g_ref[...]
    hf_hi = hf.astype(BF16)
    hf_ref[...] = hf_hi
    hf_lo = (hf - hf_hi.astype(F32)).astype(BF16)
    logits = (jnp.dot(hf_hi, wr_ref[0], preferred_element_type=F32)
              + jnp.dot(hf_hi, wr_ref[1], preferred_element_type=F32)
              + jnp.dot(hf_lo, wr_ref[0], preferred_element_type=F32)) + br_ref[...]
    lt = logits.T
    gl = [lt[j:j + 1, :] for j in range(MOE_GROUPS)]
    gmax, gidx = _first_max(gl)
    denom = gl[0] * 0.0
    for v in gl:
        denom = denom + jnp.exp(v - gmax)
    gval = 1.0 / denom
    esel = []
    for j in range(EXPERTS_PER_GROUP):
        erow = lambda g: lt[MOE_GROUPS + g * EXPERTS_PER_GROUP + j:MOE_GROUPS + g * EXPERTS_PER_GROUP + j + 1, :]
        v = erow(MOE_GROUPS - 1)
        for g in range(MOE_GROUPS - 2, -1, -1):
            v = jnp.where(gidx == float(g), erow(g), v)
        esel.append(v)
    v1, i1 = _first_max(esel)
    rest = [jnp.where(i1 == float(j), NEG_INF, esel[j]) for j in range(EXPERTS_PER_GROUP)]
    v2, i2 = _first_max(rest)
    e2 = jnp.exp(v2 - v1)
    w1 = (1.0 / (1.0 + e2)) * gval
    w2 = (e2 / (1.0 + e2)) * gval
    cw = [jnp.where(i1 == float(j), w1, jnp.where(i2 == float(j), w2, 0.0)) for j in range(EXPERTS_PER_GROUP)]
    hi = [c.astype(BF16).astype(F32) for c in cw]
    mid = [(c - h).astype(BF16).astype(F32) for c, h in zip(cw, hi)]
    lo = [(c - h - m).astype(BF16).astype(F32) for c, h, m in zip(cw, hi, mid)]
    rows = [gidx] + hi + mid + lo
    ri = lax.broadcasted_iota(jnp.int32, (LANES, tm), 0)
    packed = jnp.zeros((LANES, tm), F32)
    for j, r in enumerate(rows):
        packed = jnp.where(ri == j, r, packed)
    rrow_ref[...] = packed[0:8, :]
    rcol_ref[...] = packed.T


def _outproj(yc, ya, ys, x, wo, g, wr, br):
    n, d = x.shape
    tm = ROW_TILE
    row = lambda w: pl.BlockSpec((tm, w), lambda i: (i, 0))
    full = lambda a: pl.BlockSpec(a.shape, lambda i: (0, 0))
    return pl.pallas_call(
        functools.partial(_outproj_kernel, tm=tm),
        grid=(n // tm,),
        in_specs=[row(CONV_CH), row(ATTN_W), row(SSD_W), row(d), full(wo), full(g),
                  pl.BlockSpec(wr.shape, lambda i: (0, 0, 0)), full(br)],
        out_specs=[row(d), row(d), pl.BlockSpec((8, tm), lambda i: (0, i)), row(LANES)],
        out_shape=[jax.ShapeDtypeStruct((n, d), F32), jax.ShapeDtypeStruct((n, d), BF16),
                   jax.ShapeDtypeStruct((8, n), F32), jax.ShapeDtypeStruct((n, LANES), F32)],
        compiler_params=_params(("parallel",)),
        name="outproj_router",
    )(yc, ya, ys, x, wo, g, wr, br)


def _moe_kernel(cnt_ref, x1_ref, hf_ref, rrow_ref, rcol_ref, wg_ref, wu_ref, wd_ref, fg_ref, o_ref,
                xg_ref, acc_ref, cw_ref, rankr_ref, *, ts, spans, rows, cap, final_norm):
    i = pl.program_id(0)
    g = pl.program_id(1)
    e = pl.program_id(2)
    gf = g.astype(F32)

    def tok(sb):
        return slice(sb * ts, (sb + 1) * ts)

    def chunk(sb, c, r):
        if isinstance(c, int):
            return slice(sb * cap + c * r, sb * cap + (c + 1) * r)
        return pl.ds(pl.multiple_of(sb * cap + c * r, 16), r)

    def group_rank(sb):
        sub8 = lax.broadcasted_iota(jnp.int32, (8, ts), 0)
        return jnp.sum(jnp.where(sub8 == g, rankr_ref[:, tok(sb)], 0.0), axis=0, keepdims=True)

    def placement(sb, c, r):
        first = float(c * r) if isinstance(c, int) else (c * r).astype(F32)
        slot = lax.broadcasted_iota(jnp.int32, (r, ts), 0).astype(F32) + first
        return jnp.where((group_rank(sb) == slot) & (rrow_ref[0:1, tok(sb)] == gf), 1.0, 0.0).astype(BF16)

    def gather(sb, c, r):
        p = placement(sb, c, r)
        xg_ref[chunk(sb, c, r), :] = jnp.dot(p, hf_ref[tok(sb), :], preferred_element_type=F32).astype(BF16)
        cw_ref[chunk(sb, c, r), :] = jnp.dot(p, rcol_ref[tok(sb), :].astype(BF16), preferred_element_type=F32)
        acc_ref[chunk(sb, c, r), :] = jnp.zeros((r, acc_ref.shape[1]), F32)

    def expert(sb, c, r):
        lane_r = lax.broadcasted_iota(jnp.int32, (r, LANES), 1)
        pick = ((lane_r == 1 + e) | (lane_r == 1 + EXPERTS_PER_GROUP + e)
                | (lane_r == 1 + 2 * EXPERTS_PER_GROUP + e))
        xc = xg_ref[chunk(sb, c, r), :]
        hg = jnp.dot(xc, wg_ref[0], preferred_element_type=F32)
        hu = jnp.dot(xc, wu_ref[0], preferred_element_type=F32)
        hh = (hg * _sigmoid(hg) * hu).astype(BF16)
        y = jnp.dot(hh, wd_ref[0], preferred_element_type=F32)
        cwe = jnp.sum(jnp.where(pick, cw_ref[chunk(sb, c, r), :], 0.0), axis=-1, keepdims=True)
        acc_ref[chunk(sb, c, r), :] += cwe * y

    def scatter(sb, c, r):
        tn = (((0,), (0,)), ((), ()))
        o_ref[tok(sb), :] += lax.dot_general(placement(sb, c, r), acc_ref[chunk(sb, c, r), :].astype(BF16), tn,
                                             preferred_element_type=F32)

    def step(sb, c, r):
        @pl.when(e == 0)
        def _():
            gather(sb, c, r)

        expert(sb, c, r)

        @pl.when(e == EXPERTS_PER_GROUP - 1)
        def _():
            scatter(sb, c, r)

    @pl.when((g == 0) & (e == 0))
    def _():
        o_ref[...] = x1_ref[...]
        sub = lax.broadcasted_iota(jnp.int32, (8, ts), 0).astype(F32)
        for c0 in range(0, ts, RANK_STRIP):
            r_ = lax.broadcasted_iota(jnp.int32, (ts, RANK_STRIP), 0)
            c_ = lax.broadcasted_iota(jnp.int32, (ts, RANK_STRIP), 1) + c0
            before = jnp.where(r_ < c_, 1.0, 0.0).astype(BF16)
            for sb in range(spans):
                m_row = jnp.where(rrow_ref[0:1, tok(sb)] == sub, 1.0, 0.0).astype(BF16)
                rankr_ref[:, sb * ts + c0:sb * ts + c0 + RANK_STRIP] = jnp.dot(
                    m_row, before, preferred_element_type=F32)

    for sb in range(spans):
        cnt = cnt_ref[(i * spans + sb) * MOE_GROUPS + g]
        lo = 0
        for r in rows[:-1]:
            @pl.when((cnt > lo) & (cnt <= r))
            def _(sb=sb, r=r):
                step(sb, 0, r)
            lo = r

        @pl.when(cnt > lo)
        def _(sb=sb, cnt=cnt):
            r = rows[-1]

            def body(c, carry):
                step(sb, c, r)
                return carry

            lax.fori_loop(0, (cnt + r - 1) // r, body, 0)

    if final_norm:
        @pl.when((g == MOE_GROUPS - 1) & (e == EXPERTS_PER_GROUP - 1))
        def _():
            xo = o_ref[...]
            ms = jnp.mean(xo * xo, axis=-1, keepdims=True)
            o_ref[...] = xo * lax.rsqrt(ms + EPS) * fg_ref[...]


def _moe(counts, x1, hf, rrow, rcol, wg, wu, wd, fg, final_norm):
    n, d = x1.shape
    tb, ts, rows = MOE_BLOCK, MOE_SORT, MOE_ROWS
    spans = tb // ts
    de = wg.shape[2]
    cap = -(-ts // rows[-1]) * rows[-1]
    once = pl.Buffered(1)
    tokens = lambda w, **kw: pl.BlockSpec((tb, w), lambda i, g, e, c: (i, 0), **kw)
    expert = lambda shape: pl.BlockSpec(shape, lambda i, g, e, c: (g * EXPERTS_PER_GROUP + e, 0, 0))
    grid_spec = pltpu.PrefetchScalarGridSpec(
        num_scalar_prefetch=1,
        grid=(n // tb, MOE_GROUPS, EXPERTS_PER_GROUP),
        in_specs=[tokens(d, pipeline_mode=once), tokens(d),
                  pl.BlockSpec((8, tb), lambda i, g, e, c: (0, i)), tokens(LANES),
                  expert((1, d, de)), expert((1, d, de)), expert((1, de, d)),
                  pl.BlockSpec((1, d), lambda i, g, e, c: (0, 0))],
        out_specs=tokens(d),
        scratch_shapes=[pltpu.VMEM((spans * cap, d), BF16), pltpu.VMEM((spans * cap, d), F32),
                        pltpu.VMEM((spans * cap, LANES), F32), pltpu.VMEM((8, tb), F32)],
    )
    return pl.pallas_call(
        functools.partial(_moe_kernel, ts=ts, spans=spans, rows=rows, cap=cap, final_norm=final_norm),
        grid_spec=grid_spec,
        out_shape=jax.ShapeDtypeStruct((n, d), F32),
        compiler_params=_params(("parallel", "arbitrary", "arbitrary")),
        name="moe",
    )(counts, x1, hf, rrow, rcol, wg, wu, wd, fg)


def _pad_lanes(v, width=LANES):
    v = v.reshape(1, -1).astype(F32)
    return jnp.pad(v, ((0, 0), (0, width - v.shape[1])))


def kernel(x, norm_mix, w_in, conv_dw_w, conv_dw_b, conv_ln_g, conv_ln_b, fgate_b, ssd_conv_w, ssd_conv_b,
           ssd_dt_bias, ssd_A_log, ssd_D, ssd_norm_g, w_out, norm_ffn, router_group_w, router_group_b,
           router_expert_w, router_expert_b, expert_w_gate, expert_w_up, expert_w_down, norm_final):
    bsz, seq, d = x.shape
    n = bsz * seq
    depth = w_in.shape[0]
    heads = fgate_b.shape[1]
    ssd_heads = ssd_A_log.shape[1]
    n_exp = expert_w_gate.shape[1]
    assert heads * HEAD_DIM == ATTN_W and ssd_heads * HEAD_DIM == SSD_W
    assert n_exp == MOE_GROUPS * EXPERTS_PER_GROUP and conv_dw_w.shape[1] <= CONV_HALO + 1
    assert n % ROW_TILE == 0 and n % MOE_BLOCK == 0 and seq % max(CONV_TILE, CUM_TILE, 2 * ATTN_TQ, SSD_T) == 0

    sizes = (2 * CONV_CH, ATTN_W, ATTN_W, ATTN_W, heads, SSD_W, SSD_W, SSD_GROUPS * SSD_STATE,
             SSD_GROUPS * SSD_STATE, ssd_heads)
    offs = [0]
    for s in sizes:
        offs.append(offs[-1] + s)
    expand = jnp.repeat(jnp.eye(LANES, SSD_W // HEAD_DIM, dtype=BF16), HEAD_DIM, axis=1)

    xr = x.reshape(n, d)
    for l in range(depth):
        w = w_in[l]
        wa = jnp.concatenate([w[:, offs[0]:offs[4]], w[:, offs[5]:offs[9]]], axis=1).astype(BF16)
        zpad = jnp.zeros((d, LANES - heads), F32)
        ws = jnp.concatenate([w[:, offs[4]:offs[5]], zpad, w[:, offs[9]:offs[10]], zpad], axis=1).astype(BF16)
        glu, q, k, v, z, xbc, fs, ds = _inproj(xr, norm_mix[l].reshape(1, d), wa, ws)

        seq3 = lambda a: a.reshape(bsz, seq, a.shape[-1])
        y_conv = _conv(seq3(glu), conv_dw_w[l], conv_dw_b[l].reshape(1, -1), conv_ln_g[l].reshape(1, -1),
                       conv_ln_b[l].reshape(1, -1))
        cum = _cum(seq3(fs), _pad_lanes(fgate_b[l]))
        y_attn = _attention(seq3(q), seq3(k), seq3(v), cum)
        y_ssd = _ssd(seq3(xbc), seq3(z), seq3(ds), ssd_conv_w[l], ssd_conv_b[l].reshape(1, -1),
                     _pad_lanes(ssd_dt_bias[l]), _pad_lanes(ssd_A_log[l]),
                     jnp.repeat(ssd_D[l].astype(F32), HEAD_DIM).reshape(1, -1), ssd_norm_g[l].reshape(1, -1),
                     expand)

        wr = jnp.concatenate([router_group_w[l], router_expert_w[l],
                              jnp.zeros((d, LANES - MOE_GROUPS - n_exp), F32)], axis=1)
        wr_hi = wr.astype(BF16)
        wr = jnp.stack([wr_hi, (wr - wr_hi.astype(F32)).astype(BF16)])
        br = _pad_lanes(jnp.concatenate([router_group_b[l], router_expert_b[l]]))
        x1, hf, rrow, rcol = _outproj(y_conv.reshape(n, -1), y_attn.reshape(n, -1), y_ssd.reshape(n, -1), xr,
                                      w_out[l].astype(BF16), norm_ffn[l].reshape(1, d), wr, br)

        gidx = rrow[0].astype(jnp.int32).reshape(n // MOE_SORT, MOE_SORT)
        counts = jnp.sum(gidx[:, :, None] == jnp.arange(MOE_GROUPS, dtype=jnp.int32), axis=1,
                         dtype=jnp.int32).reshape(-1)
        xr = _moe(counts, x1, hf, rrow, rcol, expert_w_gate[l].astype(BF16), expert_w_up[l].astype(BF16),
                  expert_w_down[l].astype(BF16), norm_final.reshape(1, d), final_norm=(l == depth - 1))
    return xr.reshape(bsz, seq, d)
```

```python
import functools

import jax
import jax.numpy as jnp
from jax import lax
from jax.experimental import pallas as pl
from jax.experimental.pallas import tpu as pltpu

F32 = jnp.float32
BF16 = jnp.bfloat16
EPS = 1e-6
NEG_INF = float("-inf")

HEAD_DIM = 64
LANES = 128
CONV_CH = 512
ATTN_W = 512
SSD_W = 512
SSD_GROUPS = 2
SSD_STATE = 64
SSD_CONV_CH = SSD_W + 2 * SSD_GROUPS * SSD_STATE
MOE_GROUPS = 4
EXPERTS_PER_GROUP = 4

ROW_TILE = 1024
CONV_TILE = 512
CONV_ROWS = 32
CONV_HALO = 32
SUBLANES = 8
CUM_TILE = 512
ATTN_TQ = 512
SSD_T = 128
SSD_CHUNKS_PER_STEP = 4
SSD_HALO = 8
MOE_BLOCK = 2048
MOE_SORT = 1024
MOE_ROWS = (256, 288, 320, 384)
RANK_STRIP = 256
VMEM_LIMIT = 60 * 1024 * 1024


def _params(sem):
    return pltpu.CompilerParams(dimension_semantics=sem, vmem_limit_bytes=VMEM_LIMIT)


def _sigmoid(x):
    return jax.nn.sigmoid(x)


def _softplus(x):
    return jnp.maximum(x, 0.0) + jnp.log1p(jnp.exp(-jnp.abs(x)))


def _split3(x):
    hi = x.astype(BF16)
    r = x - hi.astype(F32)
    mid = r.astype(BF16)
    lo = (r - mid.astype(F32)).astype(BF16)
    return hi, mid, lo


def _select_dot(sel, x):
    hi, mid, lo = _split3(x)
    d = lambda p: jnp.dot(sel, p, preferred_element_type=F32)
    return d(hi) + d(mid) + d(lo)


def _dot_select(x, sel):
    hi, mid, lo = _split3(x)
    d = lambda p: jnp.dot(p, sel, preferred_element_type=F32)
    return d(hi) + d(mid) + d(lo)


def _rows8(w_ref, k, rows):
    return jnp.concatenate([w_ref[k * SUBLANES:(k + 1) * SUBLANES, :]] * (rows // SUBLANES), axis=0)


def _inproj_kernel(x_ref, g_ref, wa_ref, ws_ref, glu_ref, q_ref, k_ref, v_ref, z_ref, xbc_ref,
                   fs_ref, ds_ref):
    xf = x_ref[...]
    ms = jnp.mean(xf * xf, axis=-1, keepdims=True)
    h = (xf * lax.rsqrt(ms + EPS) * g_ref[...]).astype(BF16)

    def mm(lo, hi):
        return jnp.dot(h, wa_ref[:, lo:hi], preferred_element_type=F32)

    glu_ref[...] = mm(0, 512) * _sigmoid(mm(512, 1024))
    q_ref[...] = (mm(1024, 1536) * (HEAD_DIM ** -0.5)).astype(BF16)
    k_ref[...] = mm(1536, 2048).astype(BF16)
    v_ref[...] = mm(2048, 2560).astype(BF16)
    z_ref[...] = mm(2560, 3072)
    xbc_ref[...] = mm(3072, 3840)
    small = jnp.dot(h, ws_ref[...], preferred_element_type=F32)
    fs_ref[...] = small[:, 0:LANES]
    ds_ref[...] = small[:, LANES:2 * LANES]


def _inproj(x, g, wa, ws):
    n, d = x.shape
    tm = ROW_TILE
    row = lambda w: pl.BlockSpec((tm, w), lambda i: (i, 0))
    full = lambda a: pl.BlockSpec(a.shape, lambda i: (0, 0))
    shapes = [(512, F32), (512, BF16), (512, BF16), (512, BF16), (512, F32), (SSD_CONV_CH, F32),
              (LANES, F32), (LANES, F32)]
    return pl.pallas_call(
        _inproj_kernel,
        grid=(n // tm,),
        in_specs=[row(d), full(g), full(wa), full(ws)],
        out_specs=[row(w) for w, _ in shapes],
        out_shape=[jax.ShapeDtypeStruct((n, w), dt) for w, dt in shapes],
        compiler_params=_params(("parallel",)),
        name="inproj",
    )(x, g, wa, ws)


def _conv_kernel(x_ref, w_ref, b_ref, lg_ref, lb_ref, o_ref, buf_ref, *, taps, tl):
    li = pl.program_id(1)

    @pl.when(li == 0)
    def _():
        buf_ref[0:CONV_HALO, :] = jnp.zeros((CONV_HALO, CONV_CH), F32)

    @pl.when(li > 0)
    def _():
        buf_ref[0:CONV_HALO, :] = buf_ref[tl:tl + CONV_HALO, :]

    buf_ref[CONV_HALO:CONV_HALO + tl, :] = x_ref[0]
    first = CONV_HALO - (taps - 1)
    by_shift = [[k for k in range(taps) if (first + k) % SUBLANES == s] for s in range(SUBLANES)]
    for r in range(0, tl, CONV_ROWS):
        acc = jnp.broadcast_to(b_ref[...], (CONV_ROWS, CONV_CH))
        for s, ks in enumerate(by_shift):
            if not ks:
                continue
            span = CONV_ROWS + (SUBLANES if s else 0)
            z = None
            for k in ks:
                base = r + first + k - s
                term = _rows8(w_ref, k, span) * buf_ref[base:base + span, :]
                z = term if z is None else z + term
            acc = acc + z[s:s + CONV_ROWS, :]
        mu = jnp.mean(acc, axis=-1, keepdims=True)
        cen = acc - mu
        var = jnp.mean(cen * cen, axis=-1, keepdims=True)
        y = cen * lax.rsqrt(var + EPS) * lg_ref[...] + lb_ref[...]
        o_ref[0, r:r + CONV_ROWS, :] = (y * _sigmoid(y)).astype(BF16)


def _conv(glu, w, b, lg, lb):
    bsz, seq, c = glu.shape
    tl = CONV_TILE
    taps = w.shape[0]
    w = jnp.repeat(w, SUBLANES, axis=0)
    full = lambda a: pl.BlockSpec(a.shape, lambda bi, li: (0, 0))
    return pl.pallas_call(
        functools.partial(_conv_kernel, taps=taps, tl=tl),
        grid=(bsz, seq // tl),
        in_specs=[pl.BlockSpec((1, tl, c), lambda bi, li: (bi, li, 0)), full(w), full(b), full(lg), full(lb)],
        out_specs=pl.BlockSpec((1, tl, c), lambda bi, li: (bi, li, 0)),
        out_shape=jax.ShapeDtypeStruct((bsz, seq, c), BF16),
        scratch_shapes=[pltpu.VMEM((tl + CONV_HALO, c), F32)],
        compiler_params=_params(("parallel", "arbitrary")),
        name="dwconv",
    )(glu, w, b, lg, lb)


def _lower_tri(n, dtype, strict=False):
    r = lax.broadcasted_iota(jnp.int32, (n, n), 0)
    c = lax.broadcasted_iota(jnp.int32, (n, n), 1)
    keep = (c < r) if strict else (c <= r)
    return jnp.where(keep, 1.0, 0.0).astype(dtype)


def _cum_kernel(f_ref, b_ref, o_ref, carry_ref, *, tl):
    li = pl.program_id(1)

    @pl.when(li == 0)
    def _():
        carry_ref[...] = jnp.zeros_like(carry_ref)

    x = f_ref[0] + b_ref[...]
    lf = jnp.minimum(x, 0.0) - jnp.log1p(jnp.exp(-jnp.abs(x)))
    cum = _select_dot(_lower_tri(tl, BF16), lf) + carry_ref[...]
    carry_ref[...] = cum[tl - 1:tl, :]
    o_ref[0] = cum.T[0:8, :]


def _cum(fs, fb):
    bsz, seq, _ = fs.shape
    tl = CUM_TILE
    return pl.pallas_call(
        functools.partial(_cum_kernel, tl=tl),
        grid=(bsz, seq // tl),
        in_specs=[pl.BlockSpec((1, tl, LANES), lambda bi, li: (bi, li, 0)),
                  pl.BlockSpec((1, LANES), lambda bi, li: (0, 0))],
        out_specs=pl.BlockSpec((1, 8, tl), lambda bi, li: (bi, 0, li)),
        out_shape=jax.ShapeDtypeStruct((bsz, 8, seq), F32),
        scratch_shapes=[pltpu.VMEM((1, LANES), F32)],
        compiler_params=_params(("parallel", "arbitrary")),
        name="fgate_cumsum",
    )(fs, fb)


def _attn_kernel(q_ref, qn_ref, k_ref, v_ref, c_ref, o_ref, qm_ref, s_ref, m_ref, l_ref, acc_ref, *, tq, nq):
    hp = pl.program_id(1)
    qi = pl.program_id(2)
    slot = qi % 2
    nslot = 1 - slot
    lane = lax.broadcasted_iota(jnp.int32, (tq, LANES), 1)
    row = lax.broadcasted_iota(jnp.int32, (tq, tq), 0)
    col = lax.broadcasted_iota(jnp.int32, (tq, tq), 1)
    tiles = tq // LANES

    def start_tile(sl, q):
        for hh in range(2):
            in_head = (lane >= HEAD_DIM) if hh else (lane < HEAD_DIM)
            qm_ref[hh] = jnp.where(in_head, q, jnp.zeros_like(q))
            m_ref[sl, hh] = jnp.full((tq, LANES), NEG_INF, F32)

    def scores(sl, j):
        k0 = pl.multiple_of(j * tq, tq)
        kb = k_ref[0, pl.ds(k0, tq), :]
        for hh in range(2):
            cb = c_ref[0, pl.ds(hp * 2 + hh, 1), pl.ds(k0, tq)]
            s = lax.dot_general(qm_ref[hh], kb, (((1,), (1,)), ((), ())), preferred_element_type=F32) - cb
            s_ref[sl, hh, j] = s
            mt = m_ref[sl, hh]
            for t in range(tiles):
                mt = jnp.maximum(mt, s[:, t * LANES:(t + 1) * LANES])
            m_ref[sl, hh] = mt

    def diag_scores(sl, j):
        half = tq // 2
        k0 = pl.multiple_of(j * tq, tq)
        kb = k_ref[0, pl.ds(k0, tq), :]
        for hh in range(2):
            cb = c_ref[0, pl.ds(hp * 2 + hh, 1), pl.ds(k0, tq)]
            cb_lo = c_ref[0, pl.ds(hp * 2 + hh, 1), pl.ds(k0, half)]
            nt = (((1,), (1,)), ((), ()))
            top = lax.dot_general(qm_ref[hh, 0:half, :], k_ref[0, pl.ds(k0, half), :], nt,
                                  preferred_element_type=F32) - cb_lo
            top = jnp.where(lax.broadcasted_iota(jnp.int32, (half, half), 1)
                            <= lax.broadcasted_iota(jnp.int32, (half, half), 0), top, NEG_INF)
            bot = lax.dot_general(qm_ref[hh, half:tq, :], kb, nt, preferred_element_type=F32) - cb
            bot = jnp.where(lax.broadcasted_iota(jnp.int32, (half, tq), 1)
                            <= lax.broadcasted_iota(jnp.int32, (half, tq), 0) + half, bot, NEG_INF)
            s_ref[sl, hh, j, 0:half, 0:half] = top
            s_ref[sl, hh, j, 0:half, half:tq] = jnp.full((half, half), NEG_INF, F32)
            s_ref[sl, hh, j, half:tq, :] = bot
            for lo, blk in ((0, top), (half, bot)):
                mt = m_ref[sl, hh, lo:lo + half, :]
                for t in range(blk.shape[1] // LANES):
                    mt = jnp.maximum(mt, blk[:, t * LANES:(t + 1) * LANES])
                m_ref[sl, hh, lo:lo + half, :] = mt

    def finish_tile(sl, diag):
        diag_scores(sl, diag)
        for hh in range(2):
            m_ref[sl, hh] = jnp.broadcast_to(jnp.max(m_ref[sl, hh], axis=-1, keepdims=True), (tq, LANES))

    def weighted(j):
        k0 = pl.multiple_of(j * tq, tq)
        vb = v_ref[0, pl.ds(k0, tq), :]
        for hh in range(2):
            mb = m_ref[slot, hh]
            lt = l_ref[hh]
            ps = []
            for t in range(tiles):
                p = jnp.exp(s_ref[slot, hh, j, :, t * LANES:(t + 1) * LANES] - mb)
                lt = lt + p
                ps.append(p.astype(BF16))
            l_ref[hh] = lt
            acc_ref[hh] += jnp.dot(jnp.concatenate(ps, axis=1), vb, preferred_element_type=F32)

    @pl.when(qi == 0)
    def _():
        start_tile(0, q_ref[0])
        finish_tile(0, 0)

    for hh in range(2):
        l_ref[hh] = jnp.zeros((tq, LANES), F32)
        acc_ref[hh] = jnp.zeros((tq, LANES), F32)

    @pl.when(qi + 1 < nq)
    def _():
        start_tile(nslot, qn_ref[0])

        def group(first, count):
            for d in range(count):
                weighted(first + d)
                scores(nslot, first + d)

        def pair(jj, carry):
            group(2 * jj, 2)
            return carry

        n_full = qi + 1
        lax.fori_loop(0, n_full // 2, pair, 0)

        @pl.when(n_full % 2 == 1)
        def _():
            group(qi, 1)

        finish_tile(nslot, qi + 1)

    @pl.when(qi + 1 == nq)
    def _():
        def last(jj, carry):
            for d in range(2):
                weighted(2 * jj + d)
            return carry

        lax.fori_loop(0, nq // 2, last, 0)

    outs = [acc_ref[hh] / jnp.sum(l_ref[hh], axis=-1, keepdims=True) for hh in range(2)]
    o_ref[0] = jnp.where(lane < HEAD_DIM, outs[0], outs[1]).astype(BF16)


def _attention(q, k, v, cum):
    bsz, seq, w = q.shape
    tq = ATTN_TQ
    pairs = w // LANES
    nq = seq // tq
    return pl.pallas_call(
        functools.partial(_attn_kernel, tq=tq, nq=nq),
        grid=(bsz, pairs, nq),
        in_specs=[pl.BlockSpec((1, tq, LANES), lambda b, h, i: (b, i, h)),
                  pl.BlockSpec((1, tq, LANES), lambda b, h, i: (b, jnp.minimum(i + 1, nq - 1), h)),
                  pl.BlockSpec((1, seq, LANES), lambda b, h, i: (b, 0, h)),
                  pl.BlockSpec((1, seq, LANES), lambda b, h, i: (b, 0, h)),
                  pl.BlockSpec((1, 8, seq), lambda b, h, i: (b, 0, 0))],
        out_specs=pl.BlockSpec((1, tq, LANES), lambda b, h, i: (b, i, h)),
        out_shape=jax.ShapeDtypeStruct((bsz, seq, w), BF16),
        scratch_shapes=[pltpu.VMEM((2, tq, LANES), BF16), pltpu.VMEM((2, 2, nq, tq, tq), F32),
                        pltpu.VMEM((2, 2, tq, LANES), F32), pltpu.VMEM((2, tq, LANES), F32),
                        pltpu.VMEM((2, tq, LANES), F32)],
        compiler_params=_params(("parallel", "parallel", "arbitrary")),
        name="fox_attention",
    )(q, q, k, v, cum)


def _ssd_kernel(xbc_ref, z_ref, dt_ref, cw_ref, cb_ref, dtb_ref, alog_ref, dw_ref, ng_ref, ex_ref, o_ref,
                buf_ref, h_ref, *, T, taps, chunks):
    ci = pl.program_id(1)
    span = T * chunks

    @pl.when(ci == 0)
    def _():
        buf_ref[0:SSD_HALO, :] = jnp.zeros((SSD_HALO, SSD_CONV_CH), F32)
        h_ref[...] = jnp.zeros_like(h_ref)

    @pl.when(ci > 0)
    def _():
        buf_ref[0:SSD_HALO, :] = buf_ref[span:span + SSD_HALO, :]

    buf_ref[SSD_HALO:SSD_HALO + span, :] = xbc_ref[0]
    for c in range(chunks):
        rows = pl.ds(c * T, T)
        _ssd_chunk(c * T, z_ref.at[0, rows], dt_ref.at[0, rows], cw_ref, cb_ref, dtb_ref, alog_ref, dw_ref,
                   ng_ref, ex_ref, o_ref.at[0, rows], buf_ref, h_ref, T=T, taps=taps)


def _ssd_chunk(base, z_ref, dt_ref, cw_ref, cb_ref, dtb_ref, alog_ref, dw_ref, ng_ref, ex_ref, o_ref,
               buf_ref, h_ref, *, T, taps):
    gw = SSD_W // SSD_GROUPS
    first = base + SSD_HALO - (taps - 1)
    acc = jnp.broadcast_to(cb_ref[...], (T, SSD_CONV_CH))
    for k in range(taps):
        acc = acc + _rows8(cw_ref, k, T) * buf_ref[first + k:first + k + T, :]
    xc = acc * _sigmoid(acc)
    xs = xc[:, 0:SSD_W]
    b_mat = xc[:, SSD_W:SSD_W + LANES]
    c_mat = xc[:, SSD_W + LANES:SSD_W + 2 * LANES]

    dt = _softplus(dt_ref[...] + dtb_ref[...])
    a = dt * (-jnp.exp(alog_ref[...]))
    acum = _select_dot(_lower_tri(T, BF16), a)
    acum_row = acum.T
    expand = ex_ref[...]
    dt_w = _dot_select(dt, expand)
    acum_w = _dot_select(acum, expand)
    last_w = acum_w[T - 1:T, :]
    xdt = xs * dt_w
    xdt_b = xdt.astype(BF16)
    xdec_b = (xdt * jnp.exp(last_w - acum_w)).astype(BF16)
    eacum_w = jnp.exp(acum_w)
    chunk_decay = jnp.exp(last_w)
    bb = b_mat.astype(BF16)
    cc = c_mat.astype(BF16)
    bt = b_mat.T.astype(BF16)

    row = lax.broadcasted_iota(jnp.int32, (T, T), 0)
    col = lax.broadcasted_iota(jnp.int32, (T, T), 1)
    causal = col <= row
    lane = lax.broadcasted_iota(jnp.int32, (T, LANES), 1)
    h_in = h_ref[...]
    h_ref[...] = h_in * chunk_decay + jnp.dot(bt, xdec_b, preferred_element_type=F32)
    h_in_b = h_in.astype(BF16)
    pieces = []
    for g in range(SSD_GROUPS):
        in_group = (lane >= g * SSD_STATE) & (lane < (g + 1) * SSD_STATE)
        cg = jnp.where(in_group, cc, jnp.zeros_like(cc))
        cbm = lax.dot_general(cg, bb, (((1,), (1,)), ((), ())), preferred_element_type=F32)
        yoff = jnp.dot(cg, h_in_b[:, g * gw:(g + 1) * gw], preferred_element_type=F32)
        for pr in range(gw // LANES):
            lo = g * gw + pr * LANES
            xpair = xdt_b[:, lo:lo + LANES]
            res = []
            for hh in range(2):
                head = lo // HEAD_DIM + hh
                seg = acum[:, head:head + 1] - acum_row[head:head + 1, :]
                lmat = jnp.exp(jnp.where(causal, seg, NEG_INF))
                res.append(jnp.dot((cbm * lmat).astype(BF16), xpair, preferred_element_type=F32))
            ydiag = jnp.where(lane < HEAD_DIM, res[0], res[1])
            pieces.append(ydiag + yoff[:, pr * LANES:(pr + 1) * LANES] * eacum_w[:, lo:lo + LANES])
    y = jnp.concatenate(pieces, axis=1) + xs * dw_ref[...]
    zz = z_ref[...]
    gated = y * (zz * _sigmoid(zz))
    ms = jnp.mean(gated * gated, axis=-1, keepdims=True)
    o_ref[...] = (gated * lax.rsqrt(ms + EPS) * ng_ref[...]).astype(BF16)


def _ssd(xbc, z, ds, cw, cb, dtb, alog, dwide, ng, expand):
    bsz, seq, _ = xbc.shape
    T = SSD_T
    taps = cw.shape[0]
    cw = jnp.repeat(cw, SUBLANES, axis=0)
    full = lambda a: pl.BlockSpec(a.shape, lambda bi, ci: (0, 0))
    span = T * SSD_CHUNKS_PER_STEP
    tok = lambda w: pl.BlockSpec((1, span, w), lambda bi, ci: (bi, ci, 0))
    return pl.pallas_call(
        functools.partial(_ssd_kernel, T=T, taps=taps, chunks=SSD_CHUNKS_PER_STEP),
        grid=(bsz, seq // span),
        in_specs=[tok(SSD_CONV_CH), tok(SSD_W), tok(LANES), full(cw), full(cb), full(dtb), full(alog),
                  full(dwide), full(ng), full(expand)],
        out_specs=tok(SSD_W),
        out_shape=jax.ShapeDtypeStruct((bsz, seq, SSD_W), BF16),
        scratch_shapes=[pltpu.VMEM((span + SSD_HALO, SSD_CONV_CH), F32),
                        pltpu.VMEM((SSD_GROUPS * SSD_STATE, SSD_W), F32)],
        compiler_params=_params(("parallel", "arbitrary")),
        name="ssd_mixer",
    )(xbc, z, ds, cw, cb, dtb, alog, dwide, ng, expand)


def _first_max(vals):
    best = vals[0]
    for v in vals[1:]:
        best = jnp.maximum(best, v)
    idx = jnp.full(best.shape, float(len(vals) - 1), F32)
    for j in range(len(vals) - 2, -1, -1):
        idx = jnp.where(vals[j] == best, float(j), idx)
    return best, idx


def _outproj_kernel(yc_ref, ya_ref, ys_ref, x_ref, wo_ref, g_ref, wr_ref, br_ref,
                    x1_ref, hf_ref, rrow_ref, rcol_ref, *, tm):
    y = jnp.dot(yc_ref[...], wo_ref[0:CONV_CH, :], preferred_element_type=F32)
    y = y + jnp.dot(ya_ref[...], wo_ref[CONV_CH:CONV_CH + ATTN_W, :], preferred_element_type=F32)
    y = y + jnp.dot(ys_ref[...], wo_ref[CONV_CH + ATTN_W:, :], preferred_element_type=F32)
    x1 = x_ref[...] + y
    x1_ref[...] = x1
    ms = jnp.mean(x1 * x1, axis=-1, keepdims=True)
    hf = x1 * lax.rsqrt(ms + EPS) * g_ref[...]
    hf_hi = hf.astype(BF16)
    hf_ref[...] = hf_hi
    hf_lo = (hf - hf_hi.astype(F32)).astype(BF16)
    logits = (jnp.dot(hf_hi, wr_ref[0], preferred_element_type=F32)
              + jnp.dot(hf_hi, wr_ref[1], preferred_element_type=F32)
              + jnp.dot(hf_lo, wr_ref[0], preferred_element_type=F32)) + br_ref[...]
    lt = logits.T
    gl = [lt[j:j + 1, :] for j in range(MOE_GROUPS)]
    gmax, gidx = _first_max(gl)
    denom = gl[0] * 0.0
    for v in gl:
        denom = denom + jnp.exp(v - gmax)
    gval = 1.0 / denom
    esel = []
    for j in range(EXPERTS_PER_GROUP):
        erow = lambda g: lt[MOE_GROUPS + g * EXPERTS_PER_GROUP + j:MOE_GROUPS + g * EXPERTS_PER_GROUP + j + 1, :]
        v = erow(MOE_GROUPS - 1)
        for g in range(MOE_GROUPS - 2, -1, -1):
            v = jnp.where(gidx == float(g), erow(g), v)
        esel.append(v)
    v1, i1 = _first_max(esel)
    rest = [jnp.where(i1 == float(j), NEG_INF, esel[j]) for j in range(EXPERTS_PER_GROUP)]
    v2, i2 = _first_max(rest)
    e2 = jnp.exp(v2 - v1)
    w1 = (1.0 / (1.0 + e2)) * gval
    w2 = (e2 / (1.0 + e2)) * gval
    cw = [jnp.where(i1 == float(j), w1, jnp.where(i2 == float(j), w2, 0.0)) for j in range(EXPERTS_PER_GROUP)]
    hi = [c.astype(BF16).astype(F32) for c in cw]
    mid = [(c - h).astype(BF16).astype(F32) for c, h in zip(cw, hi)]
    lo = [(c - h - m).astype(BF16).astype(F32) for c, h, m in zip(cw, hi, mid)]
    rows = [gidx] + hi + mid + lo
    ri = lax.broadcasted_iota(jnp.int32, (LANES, tm), 0)
    packed = jnp.zeros((LANES, tm), F32)
    for j, r in enumerate(rows):
        packed = jnp.where(ri == j, r, packed)
    rrow_ref[...] = packed[0:8, :]
    rcol_ref[...] = packed.T


def _outproj(yc, ya, ys, x, wo, g, wr, br):
    n, d = x.shape
    tm = ROW_TILE
    row = lambda w: pl.BlockSpec((tm, w), lambda i: (i, 0))
    full = lambda a: pl.BlockSpec(a.shape, lambda i: (0, 0))
    return pl.pallas_call(
        functools.partial(_outproj_kernel, tm=tm),
        grid=(n // tm,),
        in_specs=[row(CONV_CH), row(ATTN_W), row(SSD_W), row(d), full(wo), full(g),
                  pl.BlockSpec(wr.shape, lambda i: (0, 0, 0)), full(br)],
        out_specs=[row(d), row(d), pl.BlockSpec((8, tm), lambda i: (0, i)), row(LANES)],
        out_shape=[jax.ShapeDtypeStruct((n, d), F32), jax.ShapeDtypeStruct((n, d), BF16),
                   jax.ShapeDtypeStruct((8, n), F32), jax.ShapeDtypeStruct((n, LANES), F32)],
        compiler_params=_params(("parallel",)),
        name="outproj_router",
    )(yc, ya, ys, x, wo, g, wr, br)


def _moe_kernel(cnt_ref, x1_ref, hf_ref, rrow_ref, rcol_ref, wg_ref, wu_ref, wd_ref, fg_ref, o_ref,
                xg_ref, acc_ref, cw_ref, rankr_ref, *, ts, spans, rows, cap, final_norm):
    i = pl.program_id(0)
    g = pl.program_id(1)
    e = pl.program_id(2)
    gf = g.astype(F32)

    def tok(sb):
        return slice(sb * ts, (sb + 1) * ts)

    def chunk(sb, c, r):
        if isinstance(c, int):
            return slice(sb * cap + c * r, sb * cap + (c + 1) * r)
        return pl.ds(pl.multiple_of(sb * cap + c * r, 16), r)

    def group_rank(sb):
        sub8 = lax.broadcasted_iota(jnp.int32, (8, ts), 0)
        return jnp.sum(jnp.where(sub8 == g, rankr_ref[:, tok(sb)], 0.0), axis=0, keepdims=True)

    def placement(sb, c, r):
        first = float(c * r) if isinstance(c, int) else (c * r).astype(F32)
        slot = lax.broadcasted_iota(jnp.int32, (r, ts), 0).astype(F32) + first
        return jnp.where((group_rank(sb) == slot) & (rrow_ref[0:1, tok(sb)] == gf), 1.0, 0.0).astype(BF16)

    def gather(sb, c, r):
        p = placement(sb, c, r)
        xg_ref[chunk(sb, c, r), :] = jnp.dot(p, hf_ref[tok(sb), :], preferred_element_type=F32).astype(BF16)
        cw_ref[chunk(sb, c, r), :] = jnp.dot(p, rcol_ref[tok(sb), :].astype(BF16), preferred_element_type=F32)
        acc_ref[chunk(sb, c, r), :] = jnp.zeros((r, acc_ref.shape[1]), F32)

    def expert(sb, c, r):
        lane_r = lax.broadcasted_iota(jnp.int32, (r, LANES), 1)
        pick = ((lane_r == 1 + e) | (lane_r == 1 + EXPERTS_PER_GROUP + e)
                | (lane_r == 1 + 2 * EXPERTS_PER_GROUP + e))
        xc = xg_ref[chunk(sb, c, r), :]
        hg = jnp.dot(xc, wg_ref[0], preferred_element_type=F32)
        hu = jnp.dot(xc, wu_ref[0], preferred_element_type=F32)
        hh = (hg * _sigmoid(hg) * hu).astype(BF16)
        y = jnp.dot(hh, wd_ref[0], preferred_element_type=F32)
        cwe = jnp.sum(jnp.where(pick, cw_ref[chunk(sb, c, r), :], 0.0), axis=-1, keepdims=True)
        acc_ref[chunk(sb, c, r), :] += cwe * y

    def scatter(sb, c, r):
        tn = (((0,), (0,)), ((), ()))
        o_ref[tok(sb), :] += lax.dot_general(placement(sb, c, r), acc_ref[chunk(sb, c, r), :].astype(BF16), tn,
                                             preferred_element_type=F32)

    def step(sb, c, r):
        @pl.when(e == 0)
        def _():
            gather(sb, c, r)

        expert(sb, c, r)

        @pl.when(e == EXPERTS_PER_GROUP - 1)
        def _():
            scatter(sb, c, r)

    @pl.when((g == 0) & (e == 0))
    def _():
        o_ref[...] = x1_ref[...]
        sub = lax.broadcasted_iota(jnp.int32, (8, ts), 0).astype(F32)
        for c0 in range(0, ts, RANK_STRIP):
            r_ = lax.broadcasted_iota(jnp.int32, (ts, RANK_STRIP), 0)
            c_ = lax.broadcasted_iota(jnp.int32, (ts, RANK_STRIP), 1) + c0
            before = jnp.where(r_ < c_, 1.0, 0.0).astype(BF16)
            for sb in range(spans):
                m_row = jnp.where(rrow_ref[0:1, tok(sb)] == sub, 1.0, 0.0).astype(BF16)
                rankr_ref[:, sb * ts + c0:sb * ts + c0 + RANK_STRIP] = jnp.dot(
                    m_row, before, preferred_element_type=F32)

    for sb in range(spans):
        cnt = cnt_ref[(i * spans + sb) * MOE_GROUPS + g]
        lo = 0
        for r in rows[:-1]:
            @pl.when((cnt > lo) & (cnt <= r))
            def _(sb=sb, r=r):
                step(sb, 0, r)
            lo = r

        @pl.when(cnt > lo)
        def _(sb=sb, cnt=cnt):
            r = rows[-1]

            def body(c, carry):
                step(sb, c, r)
                return carry

            lax.fori_loop(0, (cnt + r - 1) // r, body, 0)

    if final_norm:
        @pl.when((g == MOE_GROUPS - 1) & (e == EXPERTS_PER_GROUP - 1))
        def _():
            xo = o_ref[...]
            ms = jnp.mean(xo * xo, axis=-1, keepdims=True)
            o_ref[...] = xo * lax.rsqrt(ms + EPS) * fg_ref[...]


def _moe(counts, x1, hf, rrow, rcol, wg, wu, wd, fg, final_norm):
    n, d = x1.shape
    tb, ts, rows = MOE_BLOCK, MOE_SORT, MOE_ROWS
    spans = tb // ts
    de = wg.shape[2]
    cap = -(-ts // rows[-1]) * rows[-1]
    once = pl.Buffered(1)
    tokens = lambda w, **kw: pl.BlockSpec((tb, w), lambda i, g, e, c: (i, 0), **kw)
    expert = lambda shape: pl.BlockSpec(shape, lambda i, g, e, c: (g * EXPERTS_PER_GROUP + e, 0, 0))
    grid_spec = pltpu.PrefetchScalarGridSpec(
        num_scalar_prefetch=1,
        grid=(n // tb, MOE_GROUPS, EXPERTS_PER_GROUP),
        in_specs=[tokens(d, pipeline_mode=once), tokens(d),
                  pl.BlockSpec((8, tb), lambda i, g, e, c: (0, i)), tokens(LANES),
                  expert((1, d, de)), expert((1, d, de)), expert((1, de, d)),
                  pl.BlockSpec((1, d), lambda i, g, e, c: (0, 0))],
        out_specs=tokens(d),
        scratch_shapes=[pltpu.VMEM((spans * cap, d), BF16), pltpu.VMEM((spans * cap, d), F32),
                        pltpu.VMEM((spans * cap, LANES), F32), pltpu.VMEM((8, tb), F32)],
    )
    return pl.pallas_call(
        functools.partial(_moe_kernel, ts=ts, spans=spans, rows=rows, cap=cap, final_norm=final_norm),
        grid_spec=grid_spec,
        out_shape=jax.ShapeDtypeStruct((n, d), F32),
        compiler_params=_params(("parallel", "arbitrary", "arbitrary")),
        name="moe",
    )(counts, x1, hf, rrow, rcol, wg, wu, wd, fg)


def _pad_lanes(v, width=LANES):
    v = v.reshape(1, -1).astype(F32)
    return jnp.pad(v, ((0, 0), (0, width - v.shape[1])))


def kernel(x, norm_mix, w_in, conv_dw_w, conv_dw_b, conv_ln_g, conv_ln_b, fgate_b, ssd_conv_w, ssd_conv_b,
           ssd_dt_bias, ssd_A_log, ssd_D, ssd_norm_g, w_out, norm_ffn, router_group_w, router_group_b,
           router_expert_w, router_expert_b, expert_w_gate, expert_w_up, expert_w_down, norm_final):
    bsz, seq, d = x.shape
    n = bsz * seq
    depth = w_in.shape[0]
    heads = fgate_b.shape[1]
    ssd_heads = ssd_A_log.shape[1]
    n_exp = expert_w_gate.shape[1]
    assert heads * HEAD_DIM == ATTN_W and ssd_heads * HEAD_DIM == SSD_W
    assert n_exp == MOE_GROUPS * EXPERTS_PER_GROUP and conv_dw_w.shape[1] <= CONV_HALO + 1
    assert n % ROW_TILE == 0 and n % MOE_BLOCK == 0 and seq % max(CONV_TILE, CUM_TILE, 2 * ATTN_TQ, SSD_T) == 0

    sizes = (2 * CONV_CH, ATTN_W, ATTN_W, ATTN_W, heads, SSD_W, SSD_W, SSD_GROUPS * SSD_STATE,
             SSD_GROUPS * SSD_STATE, ssd_heads)
    offs = [0]
    for s in sizes:
        offs.append(offs[-1] + s)
    expand = jnp.repeat(jnp.eye(LANES, SSD_W // HEAD_DIM, dtype=BF16), HEAD_DIM, axis=1)

    xr = x.reshape(n, d)
    for l in range(depth):
        w = w_in[l]
        wa = jnp.concatenate([w[:, offs[0]:offs[4]], w[:, offs[5]:offs[9]]], axis=1).astype(BF16)
        zpad = jnp.zeros((d, LANES - heads), F32)
        ws = jnp.concatenate([w[:, offs[4]:offs[5]], zpad, w[:, offs[9]:offs[10]], zpad], axis=1).astype(BF16)
        glu, q, k, v, z, xbc, fs, ds = _inproj(xr, norm_mix[l].reshape(1, d), wa, ws)

        seq3 = lambda a: a.reshape(bsz, seq, a.shape[-1])
        y_conv = _conv(seq3(glu), conv_dw_w[l], conv_dw_b[l].reshape(1, -1), conv_ln_g[l].reshape(1, -1),
                       conv_ln_b[l].reshape(1, -1))
        cum = _cum(seq3(fs), _pad_lanes(fgate_b[l]))
        y_attn = _attention(seq3(q), seq3(k), seq3(v), cum)
        y_ssd = _ssd(seq3(xbc), seq3(z), seq3(ds), ssd_conv_w[l], ssd_conv_b[l].reshape(1, -1),
                     _pad_lanes(ssd_dt_bias[l]), _pad_lanes(ssd_A_log[l]),
                     jnp.repeat(ssd_D[l].astype(F32), HEAD_DIM).reshape(1, -1), ssd_norm_g[l].reshape(1, -1),
                     expand)

        wr = jnp.concatenate([router_group_w[l], router_expert_w[l],
                              jnp.zeros((d, LANES - MOE_GROUPS - n_exp), F32)], axis=1)
        wr_hi = wr.astype(BF16)
        wr = jnp.stack([wr_hi, (wr - wr_hi.astype(F32)).astype(BF16)])
        br = _pad_lanes(jnp.concatenate([router_group_b[l], router_expert_b[l]]))
        x1, hf, rrow, rcol = _outproj(y_conv.reshape(n, -1), y_attn.reshape(n, -1), y_ssd.reshape(n, -1), xr,
                                      w_out[l].astype(BF16), norm_ffn[l].reshape(1, d), wr, br)

        gidx = rrow[0].astype(jnp.int32).reshape(n // MOE_SORT, MOE_SORT)
        counts = jnp.sum(gidx[:, :, None] == jnp.arange(MOE_GROUPS, dtype=jnp.int32), axis=1,
                         dtype=jnp.int32).reshape(-1)
        xr = _moe(counts, x1, hf, rrow, rcol, expert_w_gate[l].astype(BF16), expert_w_up[l].astype(BF16),
                  expert_w_down[l].astype(BF16), norm_final.reshape(1, d), final_norm=(l == depth - 1))
    return xr.reshape(bsz, seq, d)
```
